```python
import math
import jax, jax.numpy as jnp
from jax import lax
import numpy as np

D_MODEL = 1024
BATCH = 8
SEQ = 4096
DEPTH = 4

GRID_W = 64
CTX_LEN = 256
N_MIXERS = 3
HEAD_DIM = 64
ROT_FREQS = HEAD_DIM // 4
ROPE_THETA = 10000.0
Q_BLOCK = 128
DIFF_HEADS = D_MODEL // (2 * HEAD_DIM)
DIFF_V_DIM = 2 * HEAD_DIM
SWA_Q_HEADS = D_MODEL // HEAD_DIM
SWA_KV_HEADS = 4
SWA_GROUP = SWA_Q_HEADS // SWA_KV_HEADS
WINDOW = 128
FOURIER_GROUPS = 4
FOURIER_GROUP_DIM = D_MODEL // FOURIER_GROUPS
N_GROUPS = 4
EXPERTS_PER_GROUP = 8
N_EXPERTS = N_GROUPS * EXPERTS_PER_GROUP
TOP_K = 2
D_EXPERT = 512
MOE_BLOCK = 128
EPS = 1e-6
NEG_INF = -1e30

kernel_name = 'hybrid_diffusion_trunk_diffattn_swa_fnet_hmoe'


def _rmsnorm(x, g):
    x32 = x.astype(jnp.float32)
    y = x32 * lax.rsqrt(jnp.mean(x32 * x32, axis=-1, keepdims=True) + EPS)
    return (y * g.astype(jnp.float32)).astype(x.dtype)


def _modulate(h, shift, scale):
    return h * (1 + scale) + shift


def _rope_tables(rows):
    row = jnp.repeat(jnp.arange(rows, dtype=jnp.float32), GRID_W)
    col = jnp.tile(jnp.arange(GRID_W, dtype=jnp.float32), rows)
    inv = ROPE_THETA ** (-jnp.arange(ROT_FREQS, dtype=jnp.float32) / ROT_FREQS)
    ang = jnp.stack([row[:, None] * inv, col[:, None] * inv], axis=1)
    return jnp.cos(ang), jnp.sin(ang)


def _apply_rope(x, cs, sn):
    shp = x.shape
    xr = x.reshape(shp[0], shp[1], -1, 2, 2, ROT_FREQS).astype(jnp.float32)
    x1, x2 = xr[..., 0, :], xr[..., 1, :]
    c, s = cs[None, :, None], sn[None, :, None]
    out = jnp.stack([x1 * c - x2 * s, x2 * c + x1 * s], axis=-2)
    return out.reshape(shp).astype(x.dtype)


def _diff_attention(h, hc, wqkv, wo, q_g, k_g, lq1, lk1, lq2, lk2, sub_g, lam_init, cs, sn, need_ctx):
    def project(u):
        b, n, _ = u.shape
        q, k, v = jnp.split(u @ wqkv, 3, axis=-1)
        q = _rmsnorm(q.reshape(b, n, DIFF_HEADS, 2, HEAD_DIM), q_g)
        k = _rmsnorm(k.reshape(b, n, DIFF_HEADS, 2, HEAD_DIM), k_g)
        return q, k, v.reshape(b, n, DIFF_HEADS, DIFF_V_DIM)

    q, k, v = project(h)
    q, k = _apply_rope(q, cs, sn), _apply_rope(k, cs, sn)
    qc, kc, vc = project(hc)
    f32 = jnp.float32
    lam = (jnp.exp(jnp.sum(lq1.astype(f32) * lk1.astype(f32)))
           - jnp.exp(jnp.sum(lq2.astype(f32) * lk2.astype(f32))) + lam_init)
    scale = HEAD_DIM ** -0.5

    def attend(qb, kk, vv):
        s = jnp.einsum('bqhmd,bkhmd->bhmqk', qb, kk).astype(f32) * scale
        p = jax.nn.softmax(s, axis=-1)
        a = p[:, :, 0] - lam * p[:, :, 1]
        return jnp.einsum('bhqk,bkhe->bqhe', a.astype(vv.dtype), vv)

    def finish(o):
        o = _rmsnorm(o, sub_g) * (1 - lam_init)
        return o.reshape(o.shape[0], o.shape[1], -1) @ wo

    b, n = h.shape[:2]
    nblk = n // Q_BLOCK
    k_all = jnp.concatenate([k, kc], axis=1)
    v_all = jnp.concatenate([v, vc], axis=1)
    qb = q.reshape(b, nblk, Q_BLOCK, DIFF_HEADS, 2, HEAD_DIM).swapaxes(0, 1)
    o = lax.map(lambda blk: attend(blk, k_all, v_all), qb)
    o = o.swapaxes(0, 1).reshape(b, n, DIFF_HEADS, DIFF_V_DIM)
    y = finish(o)
    yc = finish(attend(qc, kc, vc)) if need_ctx else None
    return y, yc


def _swa_sink_attention(h, hc, wqkv, wo, q_g, k_g, sink, cs, sn, need_ctx):
    nq, nk = SWA_Q_HEADS * HEAD_DIM, SWA_KV_HEADS * HEAD_DIM

    def project(u):
        b, n, _ = u.shape
        qkv = u @ wqkv
        q = qkv[..., :nq].reshape(b, n, SWA_KV_HEADS, SWA_GROUP, HEAD_DIM)
        k = qkv[..., nq:nq + nk].reshape(b, n, SWA_KV_HEADS, HEAD_DIM)
        v = qkv[..., nq + nk:].reshape(b, n, SWA_KV_HEADS, HEAD_DIM)
        return _rmsnorm(q, q_g), _rmsnorm(k, k_g), v

    q, k, v = project(h)
    q, k = _apply_rope(q, cs, sn), _apply_rope(k, cs, sn)
    qc, kc, vc = project(hc)
    f32 = jnp.float32
    scale = HEAD_DIM ** -0.5
    sink_l = sink.astype(f32).reshape(SWA_KV_HEADS, SWA_GROUP)[None, :, :, None, None]

    def sink_softmax(s):
        sk = jnp.broadcast_to(sink_l, s.shape[:-1] + (1,))
        return jax.nn.softmax(jnp.concatenate([s, sk], axis=-1), axis=-1)[..., :-1]

    b, n = h.shape[:2]
    nblk = n // Q_BLOCK
    span = Q_BLOCK + 2 * WINDOW
    pad = ((0, 0), (WINDOW, WINDOW), (0, 0), (0, 0))
    kp, vp = jnp.pad(k, pad), jnp.pad(v, pad)
    rel = jnp.arange(span)[None, :] - WINDOW - jnp.arange(Q_BLOCK)[:, None]
    band = jnp.abs(rel) <= WINDOW

    def block(bi):
        start = bi * Q_BLOCK
        qb = lax.dynamic_slice_in_dim(q, start, Q_BLOCK, axis=1)
        kb = lax.dynamic_slice_in_dim(kp, start, span, axis=1)
        vb = lax.dynamic_slice_in_dim(vp, start, span, axis=1)
        kpos = start - WINDOW + jnp.arange(span)
        valid = band & ((kpos >= 0) & (kpos < n))[None, :]
        s_loc = jnp.einsum('bqhgd,bkhd->bhgqk', qb, kb).astype(f32) * scale
        s_loc = jnp.where(valid, s_loc, NEG_INF)
        s_ctx = jnp.einsum('bqhgd,bkhd->bhgqk', qb, kc).astype(f32) * scale
        p = sink_softmax(jnp.concatenate([s_loc, s_ctx], axis=-1)).astype(v.dtype)
        return (jnp.einsum('bhgqk,bkhd->bqhgd', p[..., :span], vb)
                + jnp.einsum('bhgqk,bkhd->bqhgd', p[..., span:], vc))

    o = lax.map(block, jnp.arange(nblk))
    y = o.swapaxes(0, 1).reshape(b, n, nq) @ wo
    yc = None
    if need_ctx:
        s = jnp.einsum('bqhgd,bkhd->bhgqk', qc, kc).astype(f32) * scale
        p = sink_softmax(s).astype(vc.dtype)
        oc = jnp.einsum('bhgqk,bkhd->bqhgd', p, vc)
        yc = oc.reshape(oc.shape[0], oc.shape[1], nq) @ wo
    return y, yc


def _fourier_mix(h, hc, wo, need_ctx):
    def mix(u):
        b, n, _ = u.shape
        ug = u.astype(jnp.float32).reshape(b, n, FOURIER_GROUPS, FOURIER_GROUP_DIM)
        f = jnp.fft.fft2(ug, axes=(1, 3), norm='ortho').real
        return f.reshape(b, n, D_MODEL).astype(u.dtype) @ wo
    y = mix(h)
    yc = mix(hc) if need_ctx else None
    return y, yc


def _hier_moe(h, w_r1, b_r1, w_r2, b_r2, w13, w2):
    n_tok, d = h.shape
    f32 = jnp.float32
    rows = jnp.arange(n_tok)
    p_grp = jax.nn.softmax((h @ w_r1).astype(f32) + b_r1.astype(f32), axis=-1)
    grp = jnp.argmax(p_grp, axis=-1)
    pg = p_grp[rows, grp]
    logits2 = jnp.einsum('nd,gde->nge', h, w_r2).astype(f32) + b_r2.astype(f32)
    p_in = jax.nn.softmax(logits2[rows, grp], axis=-1)
    top_p, top_j = lax.top_k(p_in, TOP_K)
    top_p = top_p / jnp.sum(top_p, axis=-1, keepdims=True)
    expert = grp[:, None] * EXPERTS_PER_GROUP + top_j
    wts = pg[:, None] * top_p

    n_asg = n_tok * TOP_K
    e_flat = expert.reshape(-1)
    t_flat = jnp.repeat(rows, TOP_K)
    w_flat = wts.reshape(-1)
    order = jnp.argsort(e_flat)
    e_s, t_s, w_s = e_flat[order], t_flat[order], w_flat[order]
    counts = jnp.bincount(e_flat, length=N_EXPERTS)
    starts = jnp.cumsum(counts) - counts
    padded = ((counts + MOE_BLOCK - 1) // MOE_BLOCK) * MOE_BLOCK
    pend = jnp.cumsum(padded)
    pstarts = pend - padded
    dest = pstarts[e_s] + jnp.arange(n_asg) - starts[e_s]
    n_blocks = (n_asg + MOE_BLOCK - 1) // MOE_BLOCK + N_EXPERTS
    n_rows = n_blocks * MOE_BLOCK
    tok_buf = jnp.zeros((n_rows,), jnp.int32).at[dest].set(t_s.astype(jnp.int32))
    wt_buf = jnp.zeros((n_rows,), f32).at[dest].set(w_s)
    blk_exp = jnp.clip(jnp.searchsorted(pend, jnp.arange(n_blocks) * MOE_BLOCK, side='right'), 0, N_EXPERTS - 1)

    def expert_block(args):
        tok, e = args
        gu = h[tok] @ w13[e]
        g, u = jnp.split(gu, 2, axis=-1)
        return (jax.nn.silu(g) * u) @ w2[e]

    ys = lax.map(expert_block, (tok_buf.reshape(n_blocks, MOE_BLOCK), blk_exp))
    y = ys.reshape(n_rows, d) * wt_buf[:, None].astype(ys.dtype)
    return jnp.zeros_like(h).at[tok_buf].add(y)


def setup_inputs(seed: int = 0) -> dict:
    key = jax.random.key(seed)
    ks = iter(jax.random.split(key, 40))

    def nrm(shape, scale):
        return scale * jax.random.normal(next(ks), shape, jnp.float32)

    d = D_MODEL
    n_a = len(range(0, DEPTH, N_MIXERS))
    n_b = len(range(1, DEPTH, N_MIXERS))
    n_f = len(range(2, DEPTH, N_MIXERS))
    return {
        'x': nrm((BATCH, SEQ, d), 1.0),
        'c': nrm((BATCH, d), 1.0),
        'ctx': nrm((BATCH, CTX_LEN, d), 1.0),
        'c_ctx': nrm((d,), 1.0),
        'ada_w': nrm((DEPTH, d, 6 * d), 0.5 * d ** -0.5),
        'ada_b': nrm((DEPTH, 6 * d), 0.02),
        'norm1_g': 1.0 + nrm((DEPTH, d), 0.05),
        'norm2_g': 1.0 + nrm((DEPTH, d), 0.05),
        'a_wqkv': nrm((n_a, d, 3 * DIFF_HEADS * 2 * HEAD_DIM), d ** -0.5),
        'a_wo': nrm((n_a, DIFF_HEADS * DIFF_V_DIM, d), (DIFF_HEADS * DIFF_V_DIM) ** -0.5),
        'a_q_norm': 1.0 + nrm((n_a, HEAD_DIM), 0.05),
        'a_k_norm': 1.0 + nrm((n_a, HEAD_DIM), 0.05),
        'a_lambda_q1': nrm((n_a, HEAD_DIM), 0.1),
        'a_lambda_k1': nrm((n_a, HEAD_DIM), 0.1),
        'a_lambda_q2': nrm((n_a, HEAD_DIM), 0.1),
        'a_lambda_k2': nrm((n_a, HEAD_DIM), 0.1),
        'a_subln_g': 1.0 + nrm((n_a, DIFF_V_DIM), 0.05),
        'b_wqkv': nrm((n_b, d, (SWA_Q_HEADS + 2 * SWA_KV_HEADS) * HEAD_DIM), d ** -0.5),
        'b_wo': nrm((n_b, SWA_Q_HEADS * HEAD_DIM, d), (SWA_Q_HEADS * HEAD_DIM) ** -0.5),
        'b_q_norm': 1.0 + nrm((n_b, HEAD_DIM), 0.05),
        'b_k_norm': 1.0 + nrm((n_b, HEAD_DIM), 0.05),
        'b_sink': nrm((n_b, SWA_Q_HEADS), 0.5),
        'f_wo': nrm((n_f, d, d), d ** -0.5),
        'r_w1': nrm((DEPTH, d, N_GROUPS), d ** -0.5),
        'r_b1': nrm((DEPTH, N_GROUPS), 0.01),
        'r_w2': nrm((DEPTH, N_GROUPS, d, EXPERTS_PER_GROUP), d ** -0.5),
        'r_b2': nrm((DEPTH, N_GROUPS, EXPERTS_PER_GROUP), 0.01),
        'e_w13': nrm((DEPTH, N_EXPERTS, d, 2 * D_EXPERT), d ** -0.5),
        'e_w2': nrm((DEPTH, N_EXPERTS, D_EXPERT, d), D_EXPERT ** -0.5),
    }


def reference(x, c, ctx, c_ctx, ada_w, ada_b, norm1_g, norm2_g,
              a_wqkv, a_wo, a_q_norm, a_k_norm, a_lambda_q1, a_lambda_k1, a_lambda_q2, a_lambda_k2, a_subln_g,
              b_wqkv, b_wo, b_q_norm, b_k_norm, b_sink,
              f_wo,
              r_w1, r_b1, r_w2, r_b2, e_w13, e_w2):
    b, n, d = x.shape
    rows = n // GRID_W
    cs, sn = _rope_tables(rows)
    silu_c = jax.nn.silu(c)
    silu_cc = jax.nn.silu(c_ctx)
    for i in range(DEPTH):
        kind, j = i % N_MIXERS, i // N_MIXERS
        need_ctx = i < DEPTH - 1
        sh1, sc1, g1, sh2, sc2, g2 = [m[:, None, :] for m in jnp.split(silu_c @ ada_w[i] + ada_b[i], 6, axis=-1)]
        sh1c, sc1c, g1c, sh2c, sc2c, g2c = jnp.split(silu_cc @ ada_w[i] + ada_b[i], 6, axis=-1)
        h = _modulate(_rmsnorm(x, norm1_g[i]), sh1, sc1)
        hc = _modulate(_rmsnorm(ctx, norm1_g[i]), sh1c, sc1c)
        if kind == 0:
            lam_init = 0.8 - 0.6 * math.exp(-0.3 * i)
            y, yc = _diff_attention(h, hc, a_wqkv[j], a_wo[j], a_q_norm[j], a_k_norm[j],
                                    a_lambda_q1[j], a_lambda_k1[j], a_lambda_q2[j], a_lambda_k2[j],
                                    a_subln_g[j], lam_init, cs, sn, need_ctx)
        elif kind == 1:
            y, yc = _swa_sink_attention(h, hc, b_wqkv[j], b_wo[j], b_q_norm[j], b_k_norm[j], b_sink[j],
                                        cs, sn, need_ctx)
        else:
            y, yc = _fourier_mix(h, hc, f_wo[j], need_ctx)
        x = x + g1 * y
        h2 = _modulate(_rmsnorm(x, norm2_g[i]), sh2, sc2)
        if need_ctx:
            ctx = ctx + g1c * yc
            h2c = _modulate(_rmsnorm(ctx, norm2_g[i]), sh2c, sc2c)
            tokens = jnp.concatenate([h2.reshape(-1, d), h2c.reshape(-1, d)], axis=0)
        else:
            tokens = h2.reshape(-1, d)
        out = _hier_moe(tokens, r_w1[i], r_b1[i], r_w2[i], r_b2[i], e_w13[i], e_w2[i])
        x = x + g2 * out[:b * n].reshape(b, n, d)
        if need_ctx:
            ctx = ctx + g2c * out[b * n:].reshape(ctx.shape)
    return x
```

```python
import functools
import math

import numpy as np
import jax
import jax.numpy as jnp
from jax import lax
from jax.experimental import pallas as pl
from jax.experimental.pallas import tpu as pltpu

F32 = jnp.float32
BF16 = jnp.bfloat16
HIGHEST = lax.Precision.HIGHEST

GRID_W = 64
HEAD_DIM = 64
ROT_FREQS = HEAD_DIM // 4
ROPE_THETA = 10000.0
WINDOW = 128
N_MIXERS = 3
SWA_KV_HEADS = 4
FOURIER_GROUPS = 4
N_GROUPS = 4
EXPERTS_PER_GROUP = 8
N_EXPERTS = N_GROUPS * EXPERTS_PER_GROUP
EPS = 1e-6
NEG_INF = -1e30

LANES = 128
SUBLANES = 8
MXU_DIM = 256
TOKEN_TILE = 256
MOE_TILE = 256
ROUTER_TILE = 512
VMEM_LIMIT = 48 * 1024 * 1024


def _cparams(*sem):
    return pltpu.CompilerParams(dimension_semantics=sem, vmem_limit_bytes=VMEM_LIMIT)


def _mod_row(i, tiles_per_batch, latent_tiles, batch):
    return jnp.where(i % tiles_per_batch >= latent_tiles, batch, i // tiles_per_batch)


def _ada_kernel(c_ref, w_ref, b_ref, o_ref):
    c = c_ref[...]
    s = c * (1.0 / (1.0 + jnp.exp(-c)))
    o_ref[...] = jnp.dot(s, w_ref[...], precision=HIGHEST, preferred_element_type=F32) + b_ref[...]


def _ada_table(cvec, ada_w, ada_b):
    depth, d, n6 = ada_w.shape
    rows = cvec.shape[0]
    tn = n6 // 4
    return pl.pallas_call(
        _ada_kernel,
        grid=(depth, n6 // tn),
        in_specs=[pl.BlockSpec((rows, d), lambda l, j: (0, 0)),
                  pl.BlockSpec((None, d, tn), lambda l, j: (l, 0, j)),
                  pl.BlockSpec((None, 1, tn), lambda l, j: (l, 0, j))],
        out_specs=pl.BlockSpec((None, rows, tn), lambda l, j: (l, 0, j)),
        out_shape=jax.ShapeDtypeStruct((depth, rows, n6), F32),
        compiler_params=_cparams("parallel", "parallel"),
        name="ada_table",
    )(cvec, ada_w, ada_b.reshape(depth, 1, n6))


def _head_mean_matrix():
    r = np.arange(LANES)
    return jnp.asarray((r[:, None] // HEAD_DIM == r[None, :] // HEAD_DIM).astype(np.float32) / HEAD_DIM)


def _proj_in_kernel(x_ref, mod_ref, g_ref, w_ref, hm_ref, gain_ref, cos_ref, sin_ref, *out_refs,
                    d, n_q, n_k, n_plain, ones_after_plain):
    q_ref = out_refs[0] if n_q else None
    k_ref = out_refs[1] if n_k else None
    p_ref = out_refs[-1]
    x = x_ref[...]
    ms = jnp.mean(x * x, axis=-1, keepdims=True)
    h = x * lax.rsqrt(ms + EPS) * g_ref[...]
    h = h * (1.0 + mod_ref[:, d:2 * d]) + mod_ref[:, 0:d]
    hb = h.astype(BF16)
    lane = lax.broadcasted_iota(jnp.int32, (1, LANES), 1)
    first_half = (lane // ROT_FREQS) % 2 == 0
    n_chunks = n_q + n_k + n_plain
    for c2 in range(0, n_chunks, 2):
        width = min(2, n_chunks - c2) * LANES
        y2 = jnp.dot(hb, w_ref[:, c2 * LANES:c2 * LANES + width], preferred_element_type=F32)
        for half in range(width // LANES):
            c = c2 + half
            y = y2[:, half * LANES:(half + 1) * LANES]
            if c < n_q + n_k:
                msq = jnp.dot(y * y, hm_ref[...], precision=HIGHEST, preferred_element_type=F32)
                is_q = c < n_q
                gain = gain_ref[0:1, :] if is_q else gain_ref[1:2, :]
                yn = y * lax.rsqrt(msq + EPS) * gain
                partner = jnp.where(first_half, pltpu.roll(yn, LANES - ROT_FREQS, 1), pltpu.roll(yn, ROT_FREQS, 1))
                out = yn * cos_ref[...] + partner * sin_ref[...]
                if is_q:
                    q_ref[:, c * LANES:(c + 1) * LANES] = (out * (HEAD_DIM ** -0.5)).astype(q_ref.dtype)
                else:
                    ck = c - n_q
                    k_ref[:, ck * LANES:(ck + 1) * LANES] = out.astype(k_ref.dtype)
            else:
                cp = c - n_q - n_k
                if ones_after_plain:
                    p_ref[:, 2 * cp * LANES:(2 * cp + 1) * LANES] = y.astype(p_ref.dtype)
                    p_ref[:, (2 * cp + 1) * LANES:(2 * cp + 2) * LANES] = jnp.ones(y.shape, p_ref.dtype)
                else:
                    p_ref[:, cp * LANES:(cp + 1) * LANES] = y.astype(p_ref.dtype)


def _proj_in(xs, mods, g, w, gains, cos_t, sin_t, geom, *, n_q, n_k, n_plain, ones_after_plain=False):
    n, d = xs.shape
    tm = TOKEN_TILE
    tpb, lt, batch = geom
    n_cols = (n_q + n_k + n_plain) * LANES
    assert w.shape == (d, n_cols)
    out_shape, out_specs = [], []
    for cnt in (n_q, n_k):
        if cnt:
            out_shape.append(jax.ShapeDtypeStruct((n, cnt * LANES), BF16))
            out_specs.append(pl.BlockSpec((tm, cnt * LANES), lambda i: (i, 0)))
    pw = n_plain * LANES * (2 if ones_after_plain else 1)
    out_shape.append(jax.ShapeDtypeStruct((n, pw), BF16))
    out_specs.append(pl.BlockSpec((tm, pw), lambda i: (i, 0)))
    kern = functools.partial(_proj_in_kernel, d=d, n_q=n_q, n_k=n_k, n_plain=n_plain,
                             ones_after_plain=ones_after_plain)
    return pl.pallas_call(
        kern,
        grid=(n // tm,),
        in_specs=[pl.BlockSpec((tm, d), lambda i: (i, 0)),
                  pl.BlockSpec((None, 1, mods.shape[-1]), lambda i: (_mod_row(i, tpb, lt, batch), 0, 0)),
                  pl.BlockSpec((1, d), lambda i: (0, 0)),
                  pl.BlockSpec((d, n_cols), lambda i: (0, 0)),
                  pl.BlockSpec((LANES, LANES), lambda i: (0, 0)),
                  pl.BlockSpec((2, LANES), lambda i: (0, 0)),
                  pl.BlockSpec((tm, LANES), lambda i: (i % tpb, 0)),
                  pl.BlockSpec((tm, LANES), lambda i: (i % tpb, 0))],
        out_specs=out_specs,
        out_shape=out_shape,
        compiler_params=_cparams("parallel"),
        name="proj_in",
    )(xs, mods, g.reshape(1, d), w, _head_mean_matrix(), gains, cos_t, sin_t)


def _diff_attn_kernel(lam_ref, q_ref, k_ref, v_ref, sg_ref, o_ref, acc_ref, m_ref,
                      *, lam_init, seq, ctx_len, kv_chunk, latent_tiles):
    qt = pl.program_id(2)
    tq = q_ref.shape[0]
    dv = o_ref.shape[1]
    q = q_ref[...]
    lane = lax.broadcasted_iota(jnp.int32, q.shape, 1)
    zero = jnp.zeros_like(q)
    q_maps = (jnp.where(lane < HEAD_DIM, q, zero), jnp.where(lane >= HEAD_DIM, q, zero))
    m_ref[...] = jnp.full(m_ref.shape, NEG_INF, F32)
    acc_ref[...] = jnp.zeros(acc_ref.shape, F32)

    def chunk(start, size):
        kk = k_ref[pl.ds(start, size), :]
        vv = v_ref[pl.ds(start, size), :]
        for m in range(2):
            s = lax.dot_general(q_maps[m], kk, (((1,), (1,)), ((), ())), preferred_element_type=F32)
            m_old = m_ref[m]
            m_new = jnp.maximum(m_old, jnp.max(s, axis=1, keepdims=True))
            p = jnp.exp(s - m_new)
            alpha = jnp.exp(m_old - m_new)
            acc_ref[m] = acc_ref[m] * alpha + jnp.dot(p.astype(BF16), vv, preferred_element_type=F32)
            m_ref[m] = m_new

    @pl.when(qt < latent_tiles)
    def _():
        def body(i, carry):
            chunk(pl.multiple_of(i * kv_chunk, kv_chunk), kv_chunk)
            return carry
        lax.fori_loop(0, seq // kv_chunk, body, 0)

    chunk(seq, ctx_len)

    lam_v = lam_ref[...]
    lam = (jnp.exp(jnp.sum(lam_v[0:1] * lam_v[1:2], axis=1, keepdims=True))
           - jnp.exp(jnp.sum(lam_v[2:3] * lam_v[3:4], axis=1, keepdims=True)) + lam_init)
    outs = []
    for m in range(2):
        acc = acc_ref[m]
        outs.append(acc[:, 0:dv] * (1.0 / acc[:, dv:dv + 1]))
    a = outs[0] - lam * outs[1]
    a = a * lax.rsqrt(jnp.mean(a * a, axis=-1, keepdims=True) + EPS) * sg_ref[...] * (1.0 - lam_init)
    o_ref[...] = a.astype(o_ref.dtype)


def _diff_attention(q, k, v1, lam_vecs, sub_g, geom, seq, ctx_len, lam_init):
    n, dq = q.shape
    tpb, lt, batch = geom
    t = tpb * TOKEN_TILE
    heads = dq // (2 * HEAD_DIM)
    dv = 2 * HEAD_DIM
    k3 = k.reshape(batch, t, dq)
    v3 = v1.reshape(batch, t, heads * 2 * dv)
    kern = functools.partial(_diff_attn_kernel, lam_init=lam_init, seq=seq, ctx_len=ctx_len,
                             kv_chunk=512, latent_tiles=lt)
    return pl.pallas_call(
        kern,
        grid=(batch, heads, tpb),
        in_specs=[pl.BlockSpec((4, HEAD_DIM), lambda b, h, i: (0, 0)),
                  pl.BlockSpec((TOKEN_TILE, 2 * HEAD_DIM), lambda b, h, i: (b * tpb + i, h)),
                  pl.BlockSpec((None, t, 2 * HEAD_DIM), lambda b, h, i: (b, 0, h)),
                  pl.BlockSpec((None, t, 2 * dv), lambda b, h, i: (b, 0, h)),
                  pl.BlockSpec((1, dv), lambda b, h, i: (0, 0))],
        out_specs=pl.BlockSpec((TOKEN_TILE, dv), lambda b, h, i: (b * tpb + i, h)),
        out_shape=jax.ShapeDtypeStruct((n, heads * dv), BF16),
        scratch_shapes=[pltpu.VMEM((2, TOKEN_TILE, 2 * dv), F32), pltpu.VMEM((2, TOKEN_TILE, 1), F32)],
        compiler_params=_cparams("parallel", "parallel", "arbitrary"),
        name="diff_attention",
    )(lam_vecs, q, k3, v3, sub_g.reshape(1, dv))


def _swa_kernel(sink_ref, q_ref, k_ref, v_ref, o_ref, *, seq, ctx_len, latent_tiles, group):
    i = pl.program_id(1)
    tq = q_ref.shape[0]
    span = tq + 2 * WINDOW
    start = pl.multiple_of(jnp.clip(i * tq - WINDOW, 0, seq - span), WINDOW)
    qpos = i * tq + lax.broadcasted_iota(jnp.int32, (tq, span), 0)
    kpos = start + lax.broadcasted_iota(jnp.int32, (tq, span), 1)
    valid = (jnp.abs(kpos - qpos) <= WINDOW) & (i < latent_tiles)
    bias = jnp.concatenate([jnp.where(valid, 0.0, NEG_INF).astype(F32), jnp.zeros((tq, ctx_len), F32)], axis=1)
    kw = k_ref[pl.ds(start, span), :]
    vw = v_ref[pl.ds(start, span), :]
    kc = k_ref[pl.ds(seq, ctx_len), :]
    vc = v_ref[pl.ds(seq, ctx_len), :]
    for h in range(SWA_KV_HEADS):
        hs = slice(h * LANES, (h + 1) * LANES)
        keys = jnp.concatenate([kw[:, hs], kc[:, hs]], axis=0)
        vals = jnp.concatenate([vw[:, hs], vc[:, hs]], axis=0)
        for g in range(group):
            hq = h * group + g
            q = q_ref[:, hq * LANES:(hq + 1) * LANES]
            s = lax.dot_general(q, keys, (((1,), (1,)), ((), ())), preferred_element_type=F32) + bias
            sink = sink_ref[hq]
            m = jnp.maximum(jnp.max(s, axis=1, keepdims=True), sink)
            p = jnp.exp(s - m)
            denom = jnp.sum(p, axis=1, keepdims=True) + jnp.exp(sink - m)
            o = jnp.dot(p.astype(BF16), vals, preferred_element_type=F32) * (1.0 / denom)
            o_ref[:, hq * LANES:(hq + 1) * LANES] = o.astype(o_ref.dtype)


def _swa_attention(q, k, v, sink, geom, seq, ctx_len):
    n, dq = q.shape
    tpb, lt, batch = geom
    t = tpb * TOKEN_TILE
    q_heads = dq // LANES
    k3 = k.reshape(batch, t, SWA_KV_HEADS * LANES)
    v3 = v.reshape(batch, t, SWA_KV_HEADS * LANES)
    kern = functools.partial(_swa_kernel, seq=seq, ctx_len=ctx_len, latent_tiles=lt,
                             group=q_heads // SWA_KV_HEADS)
    return pl.pallas_call(
        kern,
        grid=(batch, tpb),
        in_specs=[pl.BlockSpec(memory_space=pltpu.SMEM),
                  pl.BlockSpec((TOKEN_TILE, dq), lambda b, i: (b * tpb + i, 0)),
                  pl.BlockSpec((None, t, SWA_KV_HEADS * LANES), lambda b, i: (b, 0, 0)),
                  pl.BlockSpec((None, t, SWA_KV_HEADS * LANES), lambda b, i: (b, 0, 0))],
        out_specs=pl.BlockSpec((TOKEN_TILE, dq), lambda b, i: (b * tpb + i, 0)),
        out_shape=jax.ShapeDtypeStruct((n, dq), BF16),
        compiler_params=_cparams("parallel", "arbitrary"),
        name="swa_attention",
    )(sink, q, k3, v3)


def _seq_dft_kernel(ac_ref, as_ref, cc_ref, cs_ref, zc_ref, zs_ref, yc_ref, ys_ref, o_ref, *, latent_tiles):
    m = pl.program_id(2)

    def mix(a_cos, a_msin, z_cos, z_sin):
        acc = jnp.dot(a_cos[...], z_cos[...], preferred_element_type=F32)
        acc = acc + jnp.dot(a_msin[...], z_sin[...], preferred_element_type=F32)
        o_ref[...] = acc.astype(o_ref.dtype)

    @pl.when(m < latent_tiles)
    def _():
        mix(ac_ref, as_ref, zc_ref, zs_ref)

    @pl.when(m >= latent_tiles)
    def _():
        mix(cc_ref, cs_ref, yc_ref, ys_ref)


def _seq_dft(z3, lat_mats, ctx_mats, geom, d):
    tpb, lt, batch = geom
    seq, ctx_len = lat_mats[0].shape[0], ctx_mats[0].shape[0]
    assert ctx_len == TOKEN_TILE and seq % ctx_len == 0
    tm = TOKEN_TILE
    tn = d // 2
    n_col = d // tn
    lat_a = pl.BlockSpec((tm, seq), lambda b, j, m: (jnp.minimum(m, lt - 1), 0))
    ctx_a = pl.BlockSpec((ctx_len, ctx_len), lambda b, j, m: (0, 0))
    return pl.pallas_call(
        functools.partial(_seq_dft_kernel, latent_tiles=lt),
        grid=(batch, n_col, tpb),
        in_specs=[lat_a, lat_a, ctx_a, ctx_a,
                  pl.BlockSpec((None, seq, tn), lambda b, j, m: (b, 0, j)),
                  pl.BlockSpec((None, seq, tn), lambda b, j, m: (b, 0, n_col + j)),
                  pl.BlockSpec((None, ctx_len, tn), lambda b, j, m: (b, seq // ctx_len, j)),
                  pl.BlockSpec((None, ctx_len, tn), lambda b, j, m: (b, seq // ctx_len, n_col + j))],
        out_specs=pl.BlockSpec((tm, tn), lambda b, j, m: (b * tpb + m, j)),
        out_shape=jax.ShapeDtypeStruct((batch * tpb * tm, d), BF16),
        compiler_params=_cparams("parallel", "parallel", "arbitrary"),
        name="seq_dft",
    )(*lat_mats, *ctx_mats, z3, z3, z3, z3)


def _dft_mats(n, scale):
    idx = (jnp.arange(n, dtype=jnp.int32)[:, None] * jnp.arange(n, dtype=jnp.int32)[None, :]) % n
    ang = idx.astype(F32) * (2.0 * math.pi / n)
    return (jnp.cos(ang) * scale).astype(BF16), (-jnp.sin(ang) * scale).astype(BF16)


def _proj_out_kernel(o_ref, w_ref, x_ref, mod_ref, g_ref, xo_ref, h_ref, *, d):
    y = jnp.dot(o_ref[...], w_ref[...], preferred_element_type=F32)
    x = x_ref[...] + mod_ref[:, 2 * d:3 * d] * y
    xo_ref[...] = x
    ms = jnp.mean(x * x, axis=-1, keepdims=True)
    h = x * lax.rsqrt(ms + EPS) * g_ref[...]
    h_ref[...] = h * (1.0 + mod_ref[:, 4 * d:5 * d]) + mod_ref[:, 3 * d:4 * d]


def _proj_out(o, w, xs, mods, g2, geom):
    n, d = xs.shape
    ko = o.shape[1]
    tm = TOKEN_TILE
    tpb, lt, batch = geom
    return pl.pallas_call(
        functools.partial(_proj_out_kernel, d=d),
        grid=(n // tm,),
        in_specs=[pl.BlockSpec((tm, ko), lambda i: (i, 0)),
                  pl.BlockSpec((ko, d), lambda i: (0, 0)),
                  pl.BlockSpec((tm, d), lambda i: (i, 0)),
                  pl.BlockSpec((None, 1, mods.shape[-1]), lambda i: (_mod_row(i, tpb, lt, batch), 0, 0)),
                  pl.BlockSpec((1, d), lambda i: (0, 0))],
        out_specs=[pl.BlockSpec((tm, d), lambda i: (i, 0)), pl.BlockSpec((tm, d), lambda i: (i, 0))],
        out_shape=[jax.ShapeDtypeStruct((n, d), F32), jax.ShapeDtypeStruct((n, d), F32)],
        compiler_params=_cparams("parallel"),
        name="proj_out",
    )(o, w, xs, mods, g2.reshape(1, d))


ROUTER_ROWS = SUBLANES + N_EXPERTS


def _router_kernel(h_ref, wt_ref, b_ref, tri_ref, ints_ref, flt_ref, cnt_ref, carry_ref):
    step = pl.program_id(0)
    tr = h_ref.shape[0]

    @pl.when(step == 0)
    def _():
        carry_ref[...] = jnp.zeros(carry_ref.shape, F32)

    logits = lax.dot_general(wt_ref[...], h_ref[...], (((1,), (1,)), ((), ())),
                             precision=HIGHEST, preferred_element_type=F32) + b_ref[:, 0:1]
    row8 = lax.broadcasted_iota(jnp.int32, (SUBLANES, tr), 0)
    lg = jnp.where(row8 < N_GROUPS, logits[0:SUBLANES], NEG_INF)
    lg_max = jnp.max(lg, axis=0, keepdims=True)
    pg = 1.0 / jnp.sum(jnp.exp(lg - lg_max), axis=0, keepdims=True)
    grp = jnp.min(jnp.where(lg == lg_max, row8, SUBLANES), axis=0, keepdims=True)
    l2 = jnp.zeros((EXPERTS_PER_GROUP, tr), F32)
    for g in range(N_GROUPS):
        lo = SUBLANES + g * EXPERTS_PER_GROUP
        l2 = l2 + jnp.where(grp == g, logits[lo:lo + EXPERTS_PER_GROUP], 0.0)
    l2_max = jnp.max(l2, axis=0, keepdims=True)
    j0 = jnp.min(jnp.where(l2 == l2_max, row8, SUBLANES), axis=0, keepdims=True)
    rest = jnp.where(row8 == j0, NEG_INF, l2)
    r_max = jnp.max(rest, axis=0, keepdims=True)
    j1 = jnp.min(jnp.where(rest == r_max, row8, SUBLANES), axis=0, keepdims=True)
    e1 = jnp.exp(r_max - l2_max)
    inv = 1.0 / (1.0 + e1)
    w0 = pg * inv
    w1 = pg * e1 * inv
    ex0 = grp * EXPERTS_PER_GROUP + j0
    ex1 = grp * EXPERTS_PER_GROUP + j1

    rows = lax.broadcasted_iota(jnp.int32, (N_EXPERTS, tr), 0)
    oh0 = (rows == ex0).astype(F32)
    oh1 = (rows == ex1).astype(F32)
    both = oh0 + oh1
    before = jnp.dot(both.astype(BF16), tri_ref[...], preferred_element_type=F32) + carry_ref[:, 0:1]
    rank0 = jnp.sum(oh0 * before, axis=0, keepdims=True)
    rank1 = jnp.sum(oh1 * before, axis=0, keepdims=True)
    carry_ref[...] = carry_ref[...] + jnp.sum(both, axis=1, keepdims=True)
    cnt_ref[...] = carry_ref[...]

    zi = jnp.zeros((SUBLANES - 4, tr), jnp.int32)
    ints_ref[...] = jnp.concatenate([ex0, ex1, rank0.astype(jnp.int32), rank1.astype(jnp.int32), zi], axis=0)
    flt_ref[...] = jnp.concatenate([w0, w1, jnp.zeros((SUBLANES - 2, tr), F32)], axis=0)


def _router(h2, wt, bias, tri):
    n, d = h2.shape
    tr = ROUTER_TILE
    return pl.pallas_call(
        _router_kernel,
        grid=(n // tr,),
        in_specs=[pl.BlockSpec((tr, d), lambda i: (i, 0)),
                  pl.BlockSpec((ROUTER_ROWS, d), lambda i: (0, 0)),
                  pl.BlockSpec((ROUTER_ROWS, LANES), lambda i: (0, 0)),
                  pl.BlockSpec((tr, tr), lambda i: (0, 0))],
        out_specs=[pl.BlockSpec((SUBLANES, tr), lambda i: (0, i)),
                   pl.BlockSpec((SUBLANES, tr), lambda i: (0, i)),
                   pl.BlockSpec((N_EXPERTS, LANES), lambda i: (0, 0))],
        out_shape=[jax.ShapeDtypeStruct((SUBLANES, n), jnp.int32),
                   jax.ShapeDtypeStruct((SUBLANES, n), F32),
                   jax.ShapeDtypeStruct((N_EXPERTS, LANES), F32)],
        scratch_shapes=[pltpu.VMEM((N_EXPERTS, LANES), F32)],
        compiler_params=_cparams("arbitrary"),
        name="router",
    )(h2, wt, bias, tri)


def _dest_kernel(pstart_ref, ints_ref, o_ref):
    ints = ints_ref[...]
    ex = ints[0:2]
    base = jnp.zeros(ex.shape, jnp.int32)
    for e in range(N_EXPERTS):
        base = jnp.where(ex == e, pstart_ref[e], base)
    o_ref[...] = jnp.concatenate([base + ints[2:4], jnp.zeros((SUBLANES - 2, ints.shape[1]), jnp.int32)], axis=0)


def _dest_rows(pstart, ints):
    n = ints.shape[1]
    tn = 2048 if n % 2048 == 0 else ROUTER_TILE
    return pl.pallas_call(
        _dest_kernel,
        grid=(n // tn,),
        in_specs=[pl.BlockSpec(memory_space=pltpu.SMEM),
                  pl.BlockSpec((SUBLANES, tn), lambda i: (0, i))],
        out_specs=pl.BlockSpec((SUBLANES, tn), lambda i: (0, i)),
        out_shape=jax.ShapeDtypeStruct((SUBLANES, n), jnp.int32),
        compiler_params=_cparams("parallel"),
        name="dest_rows",
    )(pstart, ints)


DISPATCH_TILE = 512


def _dispatch_kernel(meta_ref, d0_ref, d1_ref, h_ref, zrow_ref, zblk_ref, xs_ref, sem_ref, zsem_ref,
                     *, n_blocks):
    step = pl.program_id(0)
    n_steps = pl.num_programs(0)
    td = d0_ref.shape[0]

    @pl.when(step == 0)
    def _():
        def zero_expert(e, carry):
            def zero_row(r, c):
                pltpu.make_async_copy(zrow_ref.at[pl.ds(0, 1)], xs_ref.at[pl.ds(meta_ref[e] + r, 1)], zsem_ref).start()
                return c
            lax.fori_loop(0, meta_ref[N_EXPERTS + e], zero_row, 0)

            def wait_row(r, c):
                pltpu.make_async_copy(zrow_ref.at[pl.ds(0, 1)], xs_ref.at[pl.ds(0, 1)], zsem_ref).wait()
                return c
            lax.fori_loop(0, meta_ref[N_EXPERTS + e], wait_row, 0)
            return carry
        lax.fori_loop(0, N_EXPERTS, zero_expert, 0)

        def zero_block(b, carry):
            cp = pltpu.make_async_copy(zblk_ref, xs_ref.at[pl.ds(b * MOE_TILE, MOE_TILE)], zsem_ref)
            cp.start()
            cp.wait()
            return carry
        lax.fori_loop(meta_ref[2 * N_EXPERTS], n_blocks, zero_block, 0)

    slot = step % 2
    base = step * td

    def issue(r, carry):
        src = h_ref.at[pl.ds(base + r, 1)]
        pltpu.make_async_copy(src, xs_ref.at[pl.ds(d0_ref[r], 1)], sem_ref.at[slot]).start()
        pltpu.make_async_copy(src, xs_ref.at[pl.ds(d1_ref[r], 1)], sem_ref.at[slot]).start()
        return carry
    lax.fori_loop(0, td, issue, 0)

    def wait_all(s):
        pltpu.make_async_copy(h_ref.at[pl.ds(0, 2 * td)], xs_ref.at[pl.ds(0, 2 * td)], sem_ref.at[s]).wait()

    @pl.when(step > 0)
    def _():
        wait_all(1 - slot)

    @pl.when(step == n_steps - 1)
    def _():
        wait_all(slot)


def _dispatch(meta, dest0, dest1, h2, n_rows):
    n, d = h2.shape
    td = DISPATCH_TILE
    n_blocks = n_rows // MOE_TILE
    smem = functools.partial(pl.BlockSpec, memory_space=pltpu.SMEM)
    return pl.pallas_call(
        functools.partial(_dispatch_kernel, n_blocks=n_blocks),
        grid=(n // td,),
        in_specs=[smem(),
                  smem((td,), lambda i: (i,)),
                  smem((td,), lambda i: (i,)),
                  pl.BlockSpec(memory_space=pl.ANY),
                  pl.BlockSpec(memory_space=pl.ANY),
                  pl.BlockSpec(memory_space=pl.ANY)],
        out_specs=pl.BlockSpec(memory_space=pl.ANY),
        out_shape=jax.ShapeDtypeStruct((n_rows, d), F32),
        scratch_shapes=[pltpu.SemaphoreType.DMA((2,)), pltpu.SemaphoreType.DMA(())],
        compiler_params=_cparams("arbitrary"),
        name="moe_dispatch",
    )(meta, dest0, dest1, h2, jnp.zeros((SUBLANES, d), F32), jnp.zeros((MOE_TILE, d), F32))


def _expert_kernel(blk_exp_ref, n_used_ref, x_ref, w13_ref, w2_ref, y_ref, *, d_expert):
    i = pl.program_id(0)

    @pl.when(i < n_used_ref[0])
    def _():
        gu = jnp.dot(x_ref[...].astype(BF16), w13_ref[...], preferred_element_type=F32)
        g = gu[:, 0:d_expert]
        u = gu[:, d_expert:2 * d_expert]
        a = g * (1.0 / (1.0 + jnp.exp(-g))) * u
        y_ref[...] = jnp.dot(a.astype(BF16), w2_ref[...], preferred_element_type=F32)

    @pl.when(i >= n_used_ref[0])
    def _():
        y_ref[...] = jnp.zeros(y_ref.shape, F32)


def _experts(blk_exp, n_used, x_sorted, w13, w2):
    n_rows, d = x_sorted.shape
    d_expert = w2.shape[1]
    n_blocks = n_rows // MOE_TILE
    grid_spec = pltpu.PrefetchScalarGridSpec(
        num_scalar_prefetch=2,
        grid=(n_blocks,),
        in_specs=[pl.BlockSpec((MOE_TILE, d), lambda i, be, nu: (i, 0)),
                  pl.BlockSpec((None, d, 2 * d_expert), lambda i, be, nu: (be[i], 0, 0)),
                  pl.BlockSpec((None, d_expert, d), lambda i, be, nu: (be[i], 0, 0))],
        out_specs=pl.BlockSpec((MOE_TILE, d), lambda i, be, nu: (i, 0)),
    )
    return pl.pallas_call(
        functools.partial(_expert_kernel, d_expert=d_expert),
        grid_spec=grid_spec,
        out_shape=jax.ShapeDtypeStruct((n_rows, d), F32),
        compiler_params=_cparams("arbitrary"),
        name="moe_experts",
    )(blk_exp, n_used, x_sorted, w13, w2)


def _combine_kernel(d0_ref, d1_ref, w_ref, x_ref, mod_ref, y_ref, o_ref, b0_ref, b1_ref, sem_ref, *, d):
    tc = x_ref.shape[0]

    def issue(r, carry):
        pltpu.make_async_copy(y_ref.at[pl.ds(d0_ref[r], 1)], b0_ref.at[pl.ds(r, 1)], sem_ref.at[0]).start()
        pltpu.make_async_copy(y_ref.at[pl.ds(d1_ref[r], 1)], b1_ref.at[pl.ds(r, 1)], sem_ref.at[1]).start()
        return carry
    lax.fori_loop(0, tc, issue, 0)
    pltpu.make_async_copy(y_ref.at[pl.ds(0, tc)], b0_ref, sem_ref.at[0]).wait()
    pltpu.make_async_copy(y_ref.at[pl.ds(0, tc)], b1_ref, sem_ref.at[1]).wait()
    out = w_ref[:, 0:1] * b0_ref[...] + w_ref[:, 1:2] * b1_ref[...]
    o_ref[...] = x_ref[...] + mod_ref[:, 5 * d:6 * d] * out


def _combine(dest0, dest1, w_cols, xs, mods, y_sorted, geom):
    n, d = xs.shape
    tc = TOKEN_TILE
    tpb, lt, batch = geom
    smem = functools.partial(pl.BlockSpec, memory_space=pltpu.SMEM)
    return pl.pallas_call(
        functools.partial(_combine_kernel, d=d),
        grid=(n // tc,),
        in_specs=[smem((tc,), lambda i: (i,)),
                  smem((tc,), lambda i: (i,)),
                  pl.BlockSpec((tc, 2), lambda i: (i, 0)),
                  pl.BlockSpec((tc, d), lambda i: (i, 0)),
                  pl.BlockSpec((None, 1, mods.shape[-1]), lambda i: (_mod_row(i, tpb, lt, batch), 0, 0)),
                  pl.BlockSpec(memory_space=pl.ANY)],
        out_specs=pl.BlockSpec((tc, d), lambda i: (i, 0)),
        out_shape=jax.ShapeDtypeStruct((n, d), F32),
        scratch_shapes=[pltpu.VMEM((tc, d), F32), pltpu.VMEM((tc, d), F32), pltpu.SemaphoreType.DMA((2,))],
        compiler_params=_cparams("arbitrary"),
        name="moe_combine",
    )(dest0, dest1, w_cols, xs, mods, y_sorted)


def _hier_moe(h2, xs, mods, geom, w_r1, b_r1, w_r2, b_r2, w13, w2):
    n, d = h2.shape
    wt = jnp.zeros((ROUTER_ROWS, d), F32)
    wt = wt.at[0:N_GROUPS].set(w_r1.T)
    wt = wt.at[SUBLANES:].set(jnp.transpose(w_r2, (0, 2, 1)).reshape(N_EXPERTS, d))
    bias = jnp.zeros((ROUTER_ROWS,), F32).at[0:N_GROUPS].set(b_r1).at[SUBLANES:].set(b_r2.reshape(-1))
    bias = jnp.broadcast_to(bias[:, None], (ROUTER_ROWS, LANES))
    tri = jnp.asarray(np.triu(np.ones((ROUTER_TILE, ROUTER_TILE), np.float32), 1), BF16)
    ints, flt, cnt = _router(h2, wt, bias, tri)

    counts = cnt[:, 0].astype(jnp.int32)
    padded = ((counts + MOE_TILE - 1) // MOE_TILE) * MOE_TILE
    pend = jnp.cumsum(padded)
    pstart = pend - padded
    n_blocks = (2 * n) // MOE_TILE + N_EXPERTS
    n_rows = n_blocks * MOE_TILE
    n_used = pend[-1] // MOE_TILE
    blk_exp = jnp.clip(jnp.searchsorted(pend, jnp.arange(n_blocks, dtype=jnp.int32) * MOE_TILE, side='right'),
                       0, N_EXPERTS - 1).astype(jnp.int32)
    meta = jnp.concatenate([pstart + counts, padded - counts, n_used[None]]).astype(jnp.int32)

    dest = _dest_rows(pstart.astype(jnp.int32), ints)
    dest0, dest1 = dest[0], dest[1]
    x_sorted = _dispatch(meta, dest0, dest1, h2, n_rows)
    y_sorted = _experts(blk_exp, n_used[None].astype(jnp.int32), x_sorted, w13.astype(BF16), w2.astype(BF16))
    w_cols = flt[0:2].T
    return _combine(dest0, dest1, w_cols, xs, mods, y_sorted, geom)


def _rope_tables(seq, ctx_len):
    rows = seq // GRID_W
    row = jnp.repeat(jnp.arange(rows, dtype=F32), GRID_W)
    col = jnp.tile(jnp.arange(GRID_W, dtype=F32), rows)
    inv = ROPE_THETA ** (-jnp.arange(ROT_FREQS, dtype=F32) / ROT_FREQS)
    ang_r, ang_c = row[:, None] * inv, col[:, None] * inv
    cos_h = jnp.concatenate([jnp.cos(ang_r)] * 2 + [jnp.cos(ang_c)] * 2, axis=1)
    sin_h = jnp.concatenate([-jnp.sin(ang_r), jnp.sin(ang_r), -jnp.sin(ang_c), jnp.sin(ang_c)], axis=1)
    cos_t = jnp.concatenate([jnp.tile(cos_h, (1, 2)), jnp.ones((ctx_len, LANES), F32)], axis=0)
    sin_t = jnp.concatenate([jnp.tile(sin_h, (1, 2)), jnp.zeros((ctx_len, LANES), F32)], axis=0)
    return cos_t, sin_t


def _pad_heads_cols(w, n_heads):
    d = w.shape[0]
    w3 = w.reshape(d, n_heads, HEAD_DIM)
    return jnp.concatenate([w3, jnp.zeros_like(w3)], axis=2).reshape(d, n_heads * LANES)


def kernel(x, c, ctx, c_ctx, ada_w, ada_b, norm1_g, norm2_g, a_wqkv, a_wo, a_q_norm, a_k_norm, a_lambda_q1, a_lambda_k1, a_lambda_q2, a_lambda_k2, a_subln_g, b_wqkv, b_wo, b_q_norm, b_k_norm, b_sink, f_wo, r_w1, r_b1, r_w2, r_b2, e_w13, e_w2):
    batch, seq, d = x.shape
    ctx_len = ctx.shape[1]
    depth = ada_w.shape[0]
    t = seq + ctx_len
    n = batch * t
    assert seq % TOKEN_TILE == 0 and ctx_len % TOKEN_TILE == 0 and n % ROUTER_TILE == 0
    tpb, lt = t // TOKEN_TILE, seq // TOKEN_TILE
    geom = (tpb, lt, batch)

    xs = jnp.concatenate([x, ctx], axis=1).reshape(n, d)
    mod_rows = ((batch + 1 + SUBLANES - 1) // SUBLANES) * SUBLANES
    cvec = jnp.zeros((mod_rows, d), F32).at[0:batch].set(c).at[batch].set(c_ctx)
    mods_all = _ada_table(cvec, ada_w, ada_b).reshape(depth, mod_rows, 1, 6 * d)
    cos_t, sin_t = _rope_tables(seq, ctx_len)

    for i in range(depth):
        kind, j = i % N_MIXERS, i // N_MIXERS
        mods = mods_all[i]
        if kind == 0:
            lam_init = 0.8 - 0.6 * math.exp(-0.3 * i)
            nh = d // LANES
            gains = jnp.stack([jnp.tile(a_q_norm[j], 2), jnp.tile(a_k_norm[j], 2)])
            q, k, v1 = _proj_in(xs, mods, norm1_g[i], a_wqkv[j].astype(BF16), gains, cos_t, sin_t, geom,
                                n_q=nh, n_k=nh, n_plain=nh, ones_after_plain=True)
            lam_vecs = jnp.stack([a_lambda_q1[j], a_lambda_k1[j], a_lambda_q2[j], a_lambda_k2[j]])
            o = _diff_attention(q, k, v1, lam_vecs, a_subln_g[j], geom, seq, ctx_len, lam_init)
            wo = a_wo[j].astype(BF16)
        elif kind == 1:
            q_heads = d // HEAD_DIM
            nq, nk = q_heads * HEAD_DIM, SWA_KV_HEADS * HEAD_DIM
            w = b_wqkv[j]
            w_pad = jnp.concatenate([_pad_heads_cols(w[:, :nq], q_heads),
                                     _pad_heads_cols(w[:, nq:nq + nk], SWA_KV_HEADS),
                                     _pad_heads_cols(w[:, nq + nk:], SWA_KV_HEADS)], axis=1).astype(BF16)
            zeros64 = jnp.zeros((HEAD_DIM,), F32)
            gains = jnp.stack([jnp.concatenate([b_q_norm[j], zeros64]), jnp.concatenate([b_k_norm[j], zeros64])])
            q, k, v = _proj_in(xs, mods, norm1_g[i], w_pad, gains, cos_t, sin_t, geom,
                               n_q=q_heads, n_k=SWA_KV_HEADS, n_plain=SWA_KV_HEADS)
            o = _swa_attention(q, k, v, b_sink[j], geom, seq, ctx_len)
            wo3 = b_wo[j].reshape(q_heads, HEAD_DIM, d)
            wo = jnp.concatenate([wo3, jnp.zeros_like(wo3)], axis=1).reshape(q_heads * LANES, d).astype(BF16)
        else:
            gd = d // FOURIER_GROUPS
            cd, msd = _dft_mats(gd, gd ** -0.5)
            eye = jnp.eye(FOURIER_GROUPS, dtype=BF16)
            w_cs = jnp.concatenate([jnp.kron(eye, cd), jnp.kron(eye, -msd)], axis=1)
            (z,) = _proj_in(xs, mods, norm1_g[i], w_cs, jnp.zeros((2, LANES), F32), cos_t, sin_t, geom,
                            n_q=0, n_k=0, n_plain=2 * d // LANES)
            o = _seq_dft(z.reshape(batch, t, 2 * d), _dft_mats(seq, seq ** -0.5),
                         _dft_mats(ctx_len, ctx_len ** -0.5), geom, d)
            wo = f_wo[j].astype(BF16)
        xs, h2 = _proj_out(o, wo, xs, mods, norm2_g[i], geom)
        xs = _hier_moe(h2, xs, mods, geom, r_w1[i], r_b1[i], r_w2[i], r_b2[i], e_w13[i], e_w2[i])

    return xs.reshape(batch, t, d)[:, :seq]
```

```python
import functools
import math

import numpy as np
import jax
import jax.numpy as jnp
from jax import lax
from jax.experimental import pallas as pl
from jax.experimental.pallas import tpu as pltpu

F32 = jnp.float32
BF16 = jnp.bfloat16
HIGHEST = lax.Precision.HIGHEST

GRID_W = 64
HEAD_DIM = 64
ROT_FREQS = HEAD_DIM // 4
ROPE_THETA = 10000.0
WINDOW = 128
N_MIXERS = 3
SWA_KV_HEADS = 4
FOURIER_GROUPS = 4
N_GROUPS = 4
EXPERTS_PER_GROUP = 8
N_EXPERTS = N_GROUPS * EXPERTS_PER_GROUP
EPS = 1e-6
NEG_INF = -1e30

LANES = 128
SUBLANES = 8
MXU_DIM = 256
TOKEN_TILE = 256
MOE_TILE = 256
ROUTER_TILE = 512
VMEM_LIMIT = 48 * 1024 * 1024


def _cparams(*sem):
    return pltpu.CompilerParams(dimension_semantics=sem, vmem_limit_bytes=VMEM_LIMIT)


def _mod_row(i, tiles_per_batch, latent_tiles, batch):
    return jnp.where(i % tiles_per_batch >= latent_tiles, batch, i // tiles_per_batch)


def _ada_kernel(c_ref, w_ref, b_ref, o_ref):
    c = c_ref[...]
    s = c * (1.0 / (1.0 + jnp.exp(-c)))
    o_ref[...] = jnp.dot(s, w_ref[...], precision=HIGHEST, preferred_element_type=F32) + b_ref[...]


def _ada_table(cvec, ada_w, ada_b):
    depth, d, n6 = ada_w.shape
    rows = cvec.shape[0]
    tn = n6 // 4
    return pl.pallas_call(
        _ada_kernel,
        grid=(depth, n6 // tn),
        in_specs=[pl.BlockSpec((rows, d), lambda l, j: (0, 0)),
                  pl.BlockSpec((None, d, tn), lambda l, j: (l, 0, j)),
                  pl.BlockSpec((None, 1, tn), lambda l, j: (l, 0, j))],
        out_specs=pl.BlockSpec((None, rows, tn), lambda l, j: (l, 0, j)),
        out_shape=jax.ShapeDtypeStruct((depth, rows, n6), F32),
        compiler_params=_cparams("parallel", "parallel"),
        name="ada_table",
    )(cvec, ada_w, ada_b.reshape(depth, 1, n6))


def _head_mean_matrix():
    r = np.arange(LANES)
    return jnp.asarray((r[:, None] // HEAD_DIM == r[None, :] // HEAD_DIM).astype(np.float32) / HEAD_DIM)


def _proj_in_kernel(x_ref, mod_ref, g_ref, w_ref, hm_ref, gain_ref, cos_ref, sin_ref, *out_refs,
                    d, n_q, n_k, n_plain, ones_after_plain, q_scale):
    q_ref = out_refs[0] if n_q else None
    k_ref = out_refs[1] if n_k else None
    p_ref = out_refs[-1]
    x = x_ref[...]
    ms = jnp.mean(x * x, axis=-1, keepdims=True)
    h = x * lax.rsqrt(ms + EPS) * g_ref[...]
    h = h * (1.0 + mod_ref[:, d:2 * d]) + mod_ref[:, 0:d]
    hb = h.astype(BF16)
    lane = lax.broadcasted_iota(jnp.int32, (1, LANES), 1)
    first_half = (lane // ROT_FREQS) % 2 == 0
    n_chunks = n_q + n_k + n_plain
    for c2 in range(0, n_chunks, 2):
        width = min(2, n_chunks - c2) * LANES
        y2 = jnp.dot(hb, w_ref[:, c2 * LANES:c2 * LANES + width], preferred_element_type=F32)
        for half in range(width // LANES):
            c = c2 + half
            y = y2[:, half * LANES:(half + 1) * LANES]
            if c < n_q + n_k:
                msq = jnp.dot(y * y, hm_ref[...], precision=HIGHEST, preferred_element_type=F32)
                is_q = c < n_q
                gain = gain_ref[0:1, :] if is_q else gain_ref[1:2, :]
                yn = y * lax.rsqrt(msq + EPS) * gain
                partner = jnp.where(first_half, pltpu.roll(yn, LANES - ROT_FREQS, 1), pltpu.roll(yn, ROT_FREQS, 1))
                out = yn * cos_ref[...] + partner * sin_ref[...]
                if is_q:
                    q_ref[:, c * LANES:(c + 1) * LANES] = (out * q_scale).astype(q_ref.dtype)
                else:
                    ck = c - n_q
                    k_ref[:, ck * LANES:(ck + 1) * LANES] = out.astype(k_ref.dtype)
            else:
                cp = c - n_q - n_k
                if ones_after_plain:
                    p_ref[:, 2 * cp * LANES:(2 * cp + 1) * LANES] = y.astype(p_ref.dtype)
                    p_ref[:, (2 * cp + 1) * LANES:(2 * cp + 2) * LANES] = jnp.ones(y.shape, p_ref.dtype)
                else:
                    p_ref[:, cp * LANES:(cp + 1) * LANES] = y.astype(p_ref.dtype)


def _proj_in(xs, mods, g, w, gains, cos_t, sin_t, geom, *, n_q, n_k, n_plain, ones_after_plain=False,
             q_scale=HEAD_DIM ** -0.5):
    n, d = xs.shape
    tm = TOKEN_TILE
    tpb, lt, batch = geom
    n_cols = (n_q + n_k + n_plain) * LANES
    assert w.shape == (d, n_cols)
    out_shape, out_specs = [], []
    for cnt in (n_q, n_k):
        if cnt:
            out_shape.append(jax.ShapeDtypeStruct((n, cnt * LANES), BF16))
            out_specs.append(pl.BlockSpec((tm, cnt * LANES), lambda i: (i, 0)))
    pw = n_plain * LANES * (2 if ones_after_plain else 1)
    out_shape.append(jax.ShapeDtypeStruct((n, pw), BF16))
    out_specs.append(pl.BlockSpec((tm, pw), lambda i: (i, 0)))
    kern = functools.partial(_proj_in_kernel, d=d, n_q=n_q, n_k=n_k, n_plain=n_plain,
                             ones_after_plain=ones_after_plain, q_scale=q_scale)
    return pl.pallas_call(
        kern,
        grid=(n // tm,),
        in_specs=[pl.BlockSpec((tm, d), lambda i: (i, 0)),
                  pl.BlockSpec((None, 1, mods.shape[-1]), lambda i: (_mod_row(i, tpb, lt, batch), 0, 0)),
                  pl.BlockSpec((1, d), lambda i: (0, 0)),
                  pl.BlockSpec((d, n_cols), lambda i: (0, 0)),
                  pl.BlockSpec((LANES, LANES), lambda i: (0, 0)),
                  pl.BlockSpec((2, LANES), lambda i: (0, 0)),
                  pl.BlockSpec((tm, LANES), lambda i: (i % tpb, 0)),
                  pl.BlockSpec((tm, LANES), lambda i: (i % tpb, 0))],
        out_specs=out_specs,
        out_shape=out_shape,
        compiler_params=_cparams("parallel"),
        name="proj_in",
    )(xs, mods, g.reshape(1, d), w, _head_mean_matrix(), gains, cos_t, sin_t)


def _diff_attn_kernel(lam_ref, q_ref, k_ref, v_ref, sg_ref, o_ref, s_ref, acc_ref, m_ref,
                      *, lam_init, seq, ctx_len, kv_chunk, latent_tiles):
    qt = pl.program_id(2)
    dv = o_ref.shape[1]
    q = q_ref[...]
    lane = lax.broadcasted_iota(jnp.int32, q.shape, 1)
    zero = jnp.zeros_like(q)
    q_maps = (jnp.where(lane < HEAD_DIM, q, zero), jnp.where(lane >= HEAD_DIM, q, zero))
    m_ref[...] = jnp.full(m_ref.shape, NEG_INF, F32)
    acc_ref[...] = jnp.zeros(acc_ref.shape, F32)
    latent_chunks = [(c * kv_chunk, kv_chunk) for c in range(seq // kv_chunk)]
    context_chunks = [(seq, ctx_len)]

    def score_sweep(chunks):
        for m in range(2):
            row_max = m_ref[m]
            for start, size in chunks:
                s = lax.dot_general(q_maps[m], k_ref[start:start + size, :], (((1,), (1,)), ((), ())),
                                    preferred_element_type=F32)
                s_ref[m, :, start:start + size] = s
                row_max = jnp.maximum(row_max, jnp.max(s, axis=1, keepdims=True))
            m_ref[m] = row_max

    def value_sweep(chunks):
        for m in range(2):
            row_max = m_ref[m]
            acc = acc_ref[m]
            for start, size in chunks:
                p = jnp.exp2(s_ref[m, :, start:start + size] - row_max)
                acc = acc + jnp.dot(p.astype(BF16), v_ref[start:start + size, :], preferred_element_type=F32)
            acc_ref[m] = acc

    @pl.when(qt < latent_tiles)
    def _():
        score_sweep(latent_chunks)

    score_sweep(context_chunks)

    @pl.when(qt < latent_tiles)
    def _():
        value_sweep(latent_chunks)

    value_sweep(context_chunks)

    lam_v = lam_ref[...]
    lam = (jnp.exp(jnp.sum(lam_v[0:1] * lam_v[1:2], axis=1, keepdims=True))
           - jnp.exp(jnp.sum(lam_v[2:3] * lam_v[3:4], axis=1, keepdims=True)) + lam_init)
    outs = []
    for m in range(2):
        acc = acc_ref[m]
        outs.append(acc[:, 0:dv] * (1.0 / acc[:, dv:dv + 1]))
    a = outs[0] - lam * outs[1]
    a = a * lax.rsqrt(jnp.mean(a * a, axis=-1, keepdims=True) + EPS) * sg_ref[...] * (1.0 - lam_init)
    o_ref[...] = a.astype(o_ref.dtype)


def _diff_attention(q, k, v1, lam_vecs, sub_g, geom, seq, ctx_len, lam_init):
    n, dq = q.shape
    tpb, lt, batch = geom
    t = tpb * TOKEN_TILE
    heads = dq // (2 * HEAD_DIM)
    dv = 2 * HEAD_DIM
    k3 = k.reshape(batch, t, dq)
    v3 = v1.reshape(batch, t, heads * 2 * dv)
    kern = functools.partial(_diff_attn_kernel, lam_init=lam_init, seq=seq, ctx_len=ctx_len,
                             kv_chunk=512, latent_tiles=lt)
    return pl.pallas_call(
        kern,
        grid=(batch, heads, tpb),
        in_specs=[pl.BlockSpec((4, HEAD_DIM), lambda b, h, i: (0, 0)),
                  pl.BlockSpec((TOKEN_TILE, 2 * HEAD_DIM), lambda b, h, i: (b * tpb + i, h)),
                  pl.BlockSpec((None, t, 2 * HEAD_DIM), lambda b, h, i: (b, 0, h)),
                  pl.BlockSpec((None, t, 2 * dv), lambda b, h, i: (b, 0, h)),
                  pl.BlockSpec((1, dv), lambda b, h, i: (0, 0))],
        out_specs=pl.BlockSpec((TOKEN_TILE, dv), lambda b, h, i: (b * tpb + i, h)),
        out_shape=jax.ShapeDtypeStruct((n, heads * dv), BF16),
        scratch_shapes=[pltpu.VMEM((2, TOKEN_TILE, t), F32), pltpu.VMEM((2, TOKEN_TILE, 2 * dv), F32),
                        pltpu.VMEM((2, TOKEN_TILE, 1), F32)],
        compiler_params=_cparams("parallel", "parallel", "arbitrary"),
        name="diff_attention",
    )(lam_vecs, q, k3, v3, sub_g.reshape(1, dv))


def _swa_kernel(sink_ref, q_ref, k_ref, v_ref, o_ref, *, seq, ctx_len, latent_tiles, group):
    i = pl.program_id(1)
    tq = q_ref.shape[0]
    span = tq + 2 * WINDOW
    start = pl.multiple_of(jnp.clip(i * tq - WINDOW, 0, seq - span), WINDOW)
    qpos = i * tq + lax.broadcasted_iota(jnp.int32, (tq, span), 0)
    kpos = start + lax.broadcasted_iota(jnp.int32, (tq, span), 1)
    valid = (jnp.abs(kpos - qpos) <= WINDOW) & (i < latent_tiles)
    bias = jnp.concatenate([jnp.where(valid, 0.0, NEG_INF).astype(F32), jnp.zeros((tq, ctx_len), F32)], axis=1)
    kw = k_ref[pl.ds(start, span), :]
    vw = v_ref[pl.ds(start, span), :]
    kc = k_ref[pl.ds(seq, ctx_len), :]
    vc = v_ref[pl.ds(seq, ctx_len), :]
    for h in range(SWA_KV_HEADS):
        hs = slice(h * LANES, (h + 1) * LANES)
        keys = jnp.concatenate([kw[:, hs], kc[:, hs]], axis=0)
        vals = jnp.concatenate([vw[:, hs], vc[:, hs]], axis=0)
        for g in range(group):
            hq = h * group + g
            q = q_ref[:, hq * LANES:(hq + 1) * LANES]
            s = lax.dot_general(q, keys, (((1,), (1,)), ((), ())), preferred_element_type=F32) + bias
            sink = sink_ref[hq]
            m = jnp.maximum(jnp.max(s, axis=1, keepdims=True), sink)
            p = jnp.exp(s - m)
            denom = jnp.sum(p, axis=1, keepdims=True) + jnp.exp(sink - m)
            o = jnp.dot(p.astype(BF16), vals, preferred_element_type=F32) * (1.0 / denom)
            o_ref[:, hq * LANES:(hq + 1) * LANES] = o.astype(o_ref.dtype)


def _swa_attention(q, k, v, sink, geom, seq, ctx_len):
    n, dq = q.shape
    tpb, lt, batch = geom
    t = tpb * TOKEN_TILE
    q_heads = dq // LANES
    k3 = k.reshape(batch, t, SWA_KV_HEADS * LANES)
    v3 = v.reshape(batch, t, SWA_KV_HEADS * LANES)
    kern = functools.partial(_swa_kernel, seq=seq, ctx_len=ctx_len, latent_tiles=lt,
                             group=q_heads // SWA_KV_HEADS)
    return pl.pallas_call(
        kern,
        grid=(batch, tpb),
        in_specs=[pl.BlockSpec(memory_space=pltpu.SMEM),
                  pl.BlockSpec((TOKEN_TILE, dq), lambda b, i: (b * tpb + i, 0)),
                  pl.BlockSpec((None, t, SWA_KV_HEADS * LANES), lambda b, i: (b, 0, 0)),
                  pl.BlockSpec((None, t, SWA_KV_HEADS * LANES), lambda b, i: (b, 0, 0))],
        out_specs=pl.BlockSpec((TOKEN_TILE, dq), lambda b, i: (b * tpb + i, 0)),
        out_shape=jax.ShapeDtypeStruct((n, dq), BF16),
        compiler_params=_cparams("parallel", "arbitrary"),
        name="swa_attention",
    )(sink, q, k3, v3)


def _seq_dft_kernel(ac_ref, as_ref, cc_ref, cs_ref, zc_ref, zs_ref, yc_ref, ys_ref, o_ref, *, latent_tiles):
    m = pl.program_id(2)

    def mix(a_cos, a_msin, z_cos, z_sin):
        acc = jnp.dot(a_cos[...], z_cos[...], preferred_element_type=F32)
        acc = acc + jnp.dot(a_msin[...], z_sin[...], preferred_element_type=F32)
        o_ref[...] = acc.astype(o_ref.dtype)

    @pl.when(m < latent_tiles)
    def _():
        mix(ac_ref, as_ref, zc_ref, zs_ref)

    @pl.when(m >= latent_tiles)
    def _():
        mix(cc_ref, cs_ref, yc_ref, ys_ref)


def _seq_dft(z3, lat_mats, ctx_mats, geom, d):
    tpb, lt, batch = geom
    seq, ctx_len = lat_mats[0].shape[0], ctx_mats[0].shape[0]
    assert ctx_len == TOKEN_TILE and seq % ctx_len == 0
    tm = TOKEN_TILE
    tn = d // 2
    n_col = d // tn
    lat_a = pl.BlockSpec((tm, seq), lambda b, j, m: (jnp.minimum(m, lt - 1), 0))
    ctx_a = pl.BlockSpec((ctx_len, ctx_len), lambda b, j, m: (0, 0))
    return pl.pallas_call(
        functools.partial(_seq_dft_kernel, latent_tiles=lt),
        grid=(batch, n_col, tpb),
        in_specs=[lat_a, lat_a, ctx_a, ctx_a,
                  pl.BlockSpec((None, seq, tn), lambda b, j, m: (b, 0, j)),
                  pl.BlockSpec((None, seq, tn), lambda b, j, m: (b, 0, n_col + j)),
                  pl.BlockSpec((None, ctx_len, tn), lambda b, j, m: (b, seq // ctx_len, j)),
                  pl.BlockSpec((None, ctx_len, tn), lambda b, j, m: (b, seq // ctx_len, n_col + j))],
        out_specs=pl.BlockSpec((tm, tn), lambda b, j, m: (b * tpb + m, j)),
        out_shape=jax.ShapeDtypeStruct((batch * tpb * tm, d), BF16),
        compiler_params=_cparams("parallel", "parallel", "arbitrary"),
        name="seq_dft",
    )(*lat_mats, *ctx_mats, z3, z3, z3, z3)


def _dft_mats(n, scale):
    idx = (jnp.arange(n, dtype=jnp.int32)[:, None] * jnp.arange(n, dtype=jnp.int32)[None, :]) % n
    ang = idx.astype(F32) * (2.0 * math.pi / n)
    return (jnp.cos(ang) * scale).astype(BF16), (-jnp.sin(ang) * scale).astype(BF16)


def _proj_out_kernel(o_ref, w_ref, x_ref, mod_ref, g_ref, xo_ref, h_ref, *, d):
    y = jnp.dot(o_ref[...], w_ref[...], preferred_element_type=F32)
    x = x_ref[...] + mod_ref[:, 2 * d:3 * d] * y
    xo_ref[...] = x
    ms = jnp.mean(x * x, axis=-1, keepdims=True)
    h = x * lax.rsqrt(ms + EPS) * g_ref[...]
    h_ref[...] = h * (1.0 + mod_ref[:, 4 * d:5 * d]) + mod_ref[:, 3 * d:4 * d]


def _proj_out(o, w, xs, mods, g2, geom):
    n, d = xs.shape
    ko = o.shape[1]
    tm = TOKEN_TILE
    tpb, lt, batch = geom
    return pl.pallas_call(
        functools.partial(_proj_out_kernel, d=d),
        grid=(n // tm,),
        in_specs=[pl.BlockSpec((tm, ko), lambda i: (i, 0)),
                  pl.BlockSpec((ko, d), lambda i: (0, 0)),
                  pl.BlockSpec((tm, d), lambda i: (i, 0)),
                  pl.BlockSpec((None, 1, mods.shape[-1]), lambda i: (_mod_row(i, tpb, lt, batch), 0, 0)),
                  pl.BlockSpec((1, d), lambda i: (0, 0))],
        out_specs=[pl.BlockSpec((tm, d), lambda i: (i, 0)), pl.BlockSpec((tm, d), lambda i: (i, 0))],
        out_shape=[jax.ShapeDtypeStruct((n, d), F32), jax.ShapeDtypeStruct((n, d), F32)],
        compiler_params=_cparams("parallel"),
        name="proj_out",
    )(o, w, xs, mods, g2.reshape(1, d))


ROUTER_ROWS = SUBLANES + N_EXPERTS


def _router_kernel(h_ref, wt_ref, b_ref, tri_ref, ints_ref, flt_ref, cnt_ref, carry_ref):
    step = pl.program_id(0)
    tr = h_ref.shape[0]

    @pl.when(step == 0)
    def _():
        carry_ref[...] = jnp.zeros(carry_ref.shape, F32)

    logits = lax.dot_general(wt_ref[...], h_ref[...], (((1,), (1,)), ((), ())),
                             precision=HIGHEST, preferred_element_type=F32) + b_ref[:, 0:1]
    row8 = lax.broadcasted_iota(jnp.int32, (SUBLANES, tr), 0)
    lg = jnp.where(row8 < N_GROUPS, logits[0:SUBLANES], NEG_INF)
    lg_max = jnp.max(lg, axis=0, keepdims=True)
    pg = 1.0 / jnp.sum(jnp.exp(lg - lg_max), axis=0, keepdims=True)
    grp = jnp.min(jnp.where(lg == lg_max, row8, SUBLANES), axis=0, keepdims=True)
    l2 = jnp.zeros((EXPERTS_PER_GROUP, tr), F32)
    for g in range(N_GROUPS):
        lo = SUBLANES + g * EXPERTS_PER_GROUP
        l2 = l2 + jnp.where(grp == g, logits[lo:lo + EXPERTS_PER_GROUP], 0.0)
    l2_max = jnp.max(l2, axis=0, keepdims=True)
    j0 = jnp.min(jnp.where(l2 == l2_max, row8, SUBLANES), axis=0, keepdims=True)
    rest = jnp.where(row8 == j0, NEG_INF, l2)
    r_max = jnp.max(rest, axis=0, keepdims=True)
    j1 = jnp.min(jnp.where(rest == r_max, row8, SUBLANES), axis=0, keepdims=True)
    e1 = jnp.exp(r_max - l2_max)
    inv = 1.0 / (1.0 + e1)
    w0 = pg * inv
    w1 = pg * e1 * inv
    ex0 = grp * EXPERTS_PER_GROUP + j0
    ex1 = grp * EXPERTS_PER_GROUP + j1

    rows = lax.broadcasted_iota(jnp.int32, (N_EXPERTS, tr), 0)
    oh0 = (rows == ex0).astype(F32)
    oh1 = (rows == ex1).astype(F32)
    both = oh0 + oh1
    before = jnp.dot(both.astype(BF16), tri_ref[...], preferred_element_type=F32) + carry_ref[:, 0:1]
    rank0 = jnp.sum(oh0 * before, axis=0, keepdims=True)
    rank1 = jnp.sum(oh1 * before, axis=0, keepdims=True)
    carry_ref[...] = carry_ref[...] + jnp.sum(both, axis=1, keepdims=True)
    cnt_ref[...] = carry_ref[...]

    zi = jnp.zeros((SUBLANES - 4, tr), jnp.int32)
    ints_ref[...] = jnp.concatenate([ex0, ex1, rank0.astype(jnp.int32), rank1.astype(jnp.int32), zi], axis=0)
    flt_ref[...] = jnp.concatenate([w0, w1, jnp.zeros((SUBLANES - 2, tr), F32)], axis=0)


def _router(h2, wt, bias, tri):
    n, d = h2.shape
    tr = ROUTER_TILE
    return pl.pallas_call(
        _router_kernel,
        grid=(n // tr,),
        in_specs=[pl.BlockSpec((tr, d), lambda i: (i, 0)),
                  pl.BlockSpec((ROUTER_ROWS, d), lambda i: (0, 0)),
                  pl.BlockSpec((ROUTER_ROWS, LANES), lambda i: (0, 0)),
                  pl.BlockSpec((tr, tr), lambda i: (0, 0))],
        out_specs=[pl.BlockSpec((SUBLANES, tr), lambda i: (0, i)),
                   pl.BlockSpec((SUBLANES, tr), lambda i: (0, i)),
                   pl.BlockSpec((N_EXPERTS, LANES), lambda i: (0, 0))],
        out_shape=[jax.ShapeDtypeStruct((SUBLANES, n), jnp.int32),
                   jax.ShapeDtypeStruct((SUBLANES, n), F32),
                   jax.ShapeDtypeStruct((N_EXPERTS, LANES), F32)],
        scratch_shapes=[pltpu.VMEM((N_EXPERTS, LANES), F32)],
        compiler_params=_cparams("arbitrary"),
        name="router",
    )(h2, wt, bias, tri)


def _dest_kernel(pstart_ref, ints_ref, o_ref):
    ints = ints_ref[...]
    ex = ints[0:2]
    base = jnp.zeros(ex.shape, jnp.int32)
    for e in range(N_EXPERTS):
        base = jnp.where(ex == e, pstart_ref[e], base)
    o_ref[...] = jnp.concatenate([base + ints[2:4], jnp.zeros((SUBLANES - 2, ints.shape[1]), jnp.int32)], axis=0)


def _dest_rows(pstart, ints):
    n = ints.shape[1]
    tn = 2048 if n % 2048 == 0 else ROUTER_TILE
    return pl.pallas_call(
        _dest_kernel,
        grid=(n // tn,),
        in_specs=[pl.BlockSpec(memory_space=pltpu.SMEM),
                  pl.BlockSpec((SUBLANES, tn), lambda i: (0, i))],
        out_specs=pl.BlockSpec((SUBLANES, tn), lambda i: (0, i)),
        out_shape=jax.ShapeDtypeStruct((SUBLANES, n), jnp.int32),
        compiler_params=_cparams("parallel"),
        name="dest_rows",
    )(pstart, ints)


DISPATCH_TILE = 512


def _dispatch_kernel(meta_ref, d0_ref, d1_ref, h_ref, xs_ref, zero_ref, sem_ref, zsem_ref, *, n_blocks):
    step = pl.program_id(0)
    td = h_ref.shape[0]

    @pl.when(step == 0)
    def _():
        zero_ref[...] = jnp.zeros(zero_ref.shape, zero_ref.dtype)

        def zero_expert(e, carry):
            def zero_row(r, c):
                pltpu.make_async_copy(zero_ref.at[pl.ds(0, 1)], xs_ref.at[pl.ds(meta_ref[e] + r, 1)], zsem_ref).start()
                return c
            lax.fori_loop(0, meta_ref[N_EXPERTS + e], zero_row, 0)

            def wait_row(r, c):
                pltpu.make_async_copy(zero_ref.at[pl.ds(0, 1)], xs_ref.at[pl.ds(0, 1)], zsem_ref).wait()
                return c
            lax.fori_loop(0, meta_ref[N_EXPERTS + e], wait_row, 0)
            return carry
        lax.fori_loop(0, N_EXPERTS, zero_expert, 0)

        def zero_block(b, carry):
            cp = pltpu.make_async_copy(zero_ref, xs_ref.at[pl.ds(b * MOE_TILE, MOE_TILE)], zsem_ref)
            cp.start()
            cp.wait()
            return carry
        lax.fori_loop(meta_ref[2 * N_EXPERTS], n_blocks, zero_block, 0)

    def issue(r, carry):
        src = h_ref.at[pl.ds(r, 1)]
        pltpu.make_async_copy(src, xs_ref.at[pl.ds(d0_ref[r], 1)], sem_ref.at[0]).start()
        pltpu.make_async_copy(src, xs_ref.at[pl.ds(d1_ref[r], 1)], sem_ref.at[1]).start()
        return carry
    lax.fori_loop(0, td, issue, 0)
    for k in range(2):
        pltpu.make_async_copy(h_ref, xs_ref.at[pl.ds(0, td)], sem_ref.at[k]).wait()


def _dispatch(meta, dest0, dest1, h2, n_rows):
    n, d = h2.shape
    td = DISPATCH_TILE
    n_blocks = n_rows // MOE_TILE
    smem = functools.partial(pl.BlockSpec, memory_space=pltpu.SMEM)
    return pl.pallas_call(
        functools.partial(_dispatch_kernel, n_blocks=n_blocks),
        grid=(n // td,),
        in_specs=[smem(),
                  smem((td,), lambda i: (i,)),
                  smem((td,), lambda i: (i,)),
                  pl.BlockSpec((td, d), lambda i: (i, 0))],
        out_specs=pl.BlockSpec(memory_space=pl.ANY),
        out_shape=jax.ShapeDtypeStruct((n_rows, d), F32),
        scratch_shapes=[pltpu.VMEM((MOE_TILE, d), F32), pltpu.SemaphoreType.DMA((2,)),
                        pltpu.SemaphoreType.DMA(())],
        compiler_params=_cparams("arbitrary"),
        name="moe_dispatch",
    )(meta, dest0, dest1, h2)


def _expert_kernel(blk_exp_ref, n_used_ref, x_ref, w13_ref, w2_ref, y_ref, *, d_expert):
    i = pl.program_id(0)

    @pl.when(i < n_used_ref[0])
    def _():
        gu = jnp.dot(x_ref[...].astype(BF16), w13_ref[...], preferred_element_type=F32)
        g = gu[:, 0:d_expert]
        u = gu[:, d_expert:2 * d_expert]
        a = g * (1.0 / (1.0 + jnp.exp(-g))) * u
        y_ref[...] = jnp.dot(a.astype(BF16), w2_ref[...], preferred_element_type=F32)

    @pl.when(i >= n_used_ref[0])
    def _():
        y_ref[...] = jnp.zeros(y_ref.shape, F32)


def _experts(blk_exp, n_used, x_sorted, w13, w2):
    n_rows, d = x_sorted.shape
    d_expert = w2.shape[1]
    n_blocks = n_rows // MOE_TILE
    grid_spec = pltpu.PrefetchScalarGridSpec(
        num_scalar_prefetch=2,
        grid=(n_blocks,),
        in_specs=[pl.BlockSpec((MOE_TILE, d), lambda i, be, nu: (i, 0)),
                  pl.BlockSpec((None, d, 2 * d_expert), lambda i, be, nu: (be[i], 0, 0)),
                  pl.BlockSpec((None, d_expert, d), lambda i, be, nu: (be[i], 0, 0))],
        out_specs=pl.BlockSpec((MOE_TILE, d), lambda i, be, nu: (i, 0)),
    )
    return pl.pallas_call(
        functools.partial(_expert_kernel, d_expert=d_expert),
        grid_spec=grid_spec,
        out_shape=jax.ShapeDtypeStruct((n_rows, d), F32),
        compiler_params=_cparams("arbitrary"),
        name="moe_experts",
    )(blk_exp, n_used, x_sorted, w13, w2)


def _combine_kernel(d0_ref, d1_ref, w_ref, x_ref, mod_ref, y_ref, o_ref, b0_ref, b1_ref, sem_ref, *, d):
    tc = x_ref.shape[0]

    def issue(r, carry):
        pltpu.make_async_copy(y_ref.at[pl.ds(d0_ref[r], 1)], b0_ref.at[pl.ds(r, 1)], sem_ref.at[0]).start()
        pltpu.make_async_copy(y_ref.at[pl.ds(d1_ref[r], 1)], b1_ref.at[pl.ds(r, 1)], sem_ref.at[1]).start()
        return carry
    lax.fori_loop(0, tc, issue, 0)
    pltpu.make_async_copy(y_ref.at[pl.ds(0, tc)], b0_ref, sem_ref.at[0]).wait()
    pltpu.make_async_copy(y_ref.at[pl.ds(0, tc)], b1_ref, sem_ref.at[1]).wait()
    out = w_ref[:, 0:1] * b0_ref[...] + w_ref[:, 1:2] * b1_ref[...]
    o_ref[...] = x_ref[...] + mod_ref[:, 5 * d:6 * d] * out


def _combine(dest0, dest1, w_cols, xs, mods, y_sorted, geom):
    n, d = xs.shape
    tc = TOKEN_TILE
    tpb, lt, batch = geom
    smem = functools.partial(pl.BlockSpec, memory_space=pltpu.SMEM)
    return pl.pallas_call(
        functools.partial(_combine_kernel, d=d),
        grid=(n // tc,),
        in_specs=[smem((tc,), lambda i: (i,)),
                  smem((tc,), lambda i: (i,)),
                  pl.BlockSpec((tc, 2), lambda i: (i, 0)),
                  pl.BlockSpec((tc, d), lambda i: (i, 0)),
                  pl.BlockSpec((None, 1, mods.shape[-1]), lambda i: (_mod_row(i, tpb, lt, batch), 0, 0)),
                  pl.BlockSpec(memory_space=pl.ANY)],
        out_specs=pl.BlockSpec((tc, d), lambda i: (i, 0)),
        out_shape=jax.ShapeDtypeStruct((n, d), F32),
        scratch_shapes=[pltpu.VMEM((tc, d), F32), pltpu.VMEM((tc, d), F32), pltpu.SemaphoreType.DMA((2,))],
        compiler_params=_cparams("arbitrary"),
        name="moe_combine",
    )(dest0, dest1, w_cols, xs, mods, y_sorted)


def _hier_moe(h2, xs, mods, geom, w_r1, b_r1, w_r2, b_r2, w13, w2):
    n, d = h2.shape
    wt = jnp.zeros((ROUTER_ROWS, d), F32)
    wt = wt.at[0:N_GROUPS].set(w_r1.T)
    wt = wt.at[SUBLANES:].set(jnp.transpose(w_r2, (0, 2, 1)).reshape(N_EXPERTS, d))
    bias = jnp.zeros((ROUTER_ROWS,), F32).at[0:N_GROUPS].set(b_r1).at[SUBLANES:].set(b_r2.reshape(-1))
    bias = jnp.broadcast_to(bias[:, None], (ROUTER_ROWS, LANES))
    tri = jnp.asarray(np.triu(np.ones((ROUTER_TILE, ROUTER_TILE), np.float32), 1), BF16)
    ints, flt, cnt = _router(h2, wt, bias, tri)

    counts = cnt[:, 0].astype(jnp.int32)
    padded = ((counts + MOE_TILE - 1) // MOE_TILE) * MOE_TILE
    pend = jnp.cumsum(padded)
    pstart = pend - padded
    n_blocks = (2 * n) // MOE_TILE + N_EXPERTS
    n_rows = n_blocks * MOE_TILE
    n_used = pend[-1] // MOE_TILE
    blk_exp = jnp.clip(jnp.searchsorted(pend, jnp.arange(n_blocks, dtype=jnp.int32) * MOE_TILE, side='right'),
                       0, N_EXPERTS - 1).astype(jnp.int32)
    meta = jnp.concatenate([pstart + counts, padded - counts, n_used[None]]).astype(jnp.int32)

    dest = _dest_rows(pstart.astype(jnp.int32), ints)
    dest0, dest1 = dest[0], dest[1]
    x_sorted = _dispatch(meta, dest0, dest1, h2, n_rows)
    y_sorted = _experts(blk_exp, n_used[None].astype(jnp.int32), x_sorted, w13.astype(BF16), w2.astype(BF16))
    w_cols = flt[0:2].T
    return _combine(dest0, dest1, w_cols, xs, mods, y_sorted, geom)


def _rope_tables(seq, ctx_len):
    rows = seq // GRID_W
    row = jnp.repeat(jnp.arange(rows, dtype=F32), GRID_W)
    col = jnp.tile(jnp.arange(GRID_W, dtype=F32), rows)
    inv = ROPE_THETA ** (-jnp.arange(ROT_FREQS, dtype=F32) / ROT_FREQS)
    ang_r, ang_c = row[:, None] * inv, col[:, None] * inv
    cos_h = jnp.concatenate([jnp.cos(ang_r)] * 2 + [jnp.cos(ang_c)] * 2, axis=1)
    sin_h = jnp.concatenate([-jnp.sin(ang_r), jnp.sin(ang_r), -jnp.sin(ang_c), jnp.sin(ang_c)], axis=1)
    cos_t = jnp.concatenate([jnp.tile(cos_h, (1, 2)), jnp.ones((ctx_len, LANES), F32)], axis=0)
    sin_t = jnp.concatenate([jnp.tile(sin_h, (1, 2)), jnp.zeros((ctx_len, LANES), F32)], axis=0)
    return cos_t, sin_t


def _pad_heads_cols(w, n_heads):
    d = w.shape[0]
    w3 = w.reshape(d, n_heads, HEAD_DIM)
    return jnp.concatenate([w3, jnp.zeros_like(w3)], axis=2).reshape(d, n_heads * LANES)


def kernel(x, c, ctx, c_ctx, ada_w, ada_b, norm1_g, norm2_g, a_wqkv, a_wo, a_q_norm, a_k_norm, a_lambda_q1, a_lambda_k1, a_lambda_q2, a_lambda_k2, a_subln_g, b_wqkv, b_wo, b_q_norm, b_k_norm, b_sink, f_wo, r_w1, r_b1, r_w2, r_b2, e_w13, e_w2):
    batch, seq, d = x.shape
    ctx_len = ctx.shape[1]
    depth = ada_w.shape[0]
    t = seq + ctx_len
    n = batch * t
    assert seq % TOKEN_TILE == 0 and ctx_len % TOKEN_TILE == 0 and n % ROUTER_TILE == 0
    tpb, lt = t // TOKEN_TILE, seq // TOKEN_TILE
    geom = (tpb, lt, batch)

    xs = jnp.concatenate([x, ctx], axis=1).reshape(n, d)
    mod_rows = ((batch + 1 + SUBLANES - 1) // SUBLANES) * SUBLANES
    cvec = jnp.zeros((mod_rows, d), F32).at[0:batch].set(c).at[batch].set(c_ctx)
    mods_all = _ada_table(cvec, ada_w, ada_b).reshape(depth, mod_rows, 1, 6 * d)
    cos_t, sin_t = _rope_tables(seq, ctx_len)

    for i in range(depth):
        kind, j = i % N_MIXERS, i // N_MIXERS
        mods = mods_all[i]
        if kind == 0:
            lam_init = 0.8 - 0.6 * math.exp(-0.3 * i)
            nh = d // LANES
            gains = jnp.stack([jnp.tile(a_q_norm[j], 2), jnp.tile(a_k_norm[j], 2)])
            q, k, v1 = _proj_in(xs, mods, norm1_g[i], a_wqkv[j].astype(BF16), gains, cos_t, sin_t, geom,
                                n_q=nh, n_k=nh, n_plain=nh, ones_after_plain=True,
                                q_scale=HEAD_DIM ** -0.5 * math.log2(math.e))
            lam_vecs = jnp.stack([a_lambda_q1[j], a_lambda_k1[j], a_lambda_q2[j], a_lambda_k2[j]])
            o = _diff_attention(q, k, v1, lam_vecs, a_subln_g[j], geom, seq, ctx_len, lam_init)
            wo = a_wo[j].astype(BF16)
        elif kind == 1:
            q_heads = d // HEAD_DIM
            nq, nk = q_heads * HEAD_DIM, SWA_KV_HEADS * HEAD_DIM
            w = b_wqkv[j]
            w_pad = jnp.concatenate([_pad_heads_cols(w[:, :nq], q_heads),
                                     _pad_heads_cols(w[:, nq:nq + nk], SWA_KV_HEADS),
                                     _pad_heads_cols(w[:, nq + nk:], SWA_KV_HEADS)], axis=1).astype(BF16)
            zeros64 = jnp.zeros((HEAD_DIM,), F32)
            gains = jnp.stack([jnp.concatenate([b_q_norm[j], zeros64]), jnp.concatenate([b_k_norm[j], zeros64])])
            q, k, v = _proj_in(xs, mods, norm1_g[i], w_pad, gains, cos_t, sin_t, geom,
                               n_q=q_heads, n_k=SWA_KV_HEADS, n_plain=SWA_KV_HEADS)
            o = _swa_attention(q, k, v, b_sink[j], geom, seq, ctx_len)
            wo3 = b_wo[j].reshape(q_heads, HEAD_DIM, d)
            wo = jnp.concatenate([wo3, jnp.zeros_like(wo3)], axis=1).reshape(q_heads * LANES, d).astype(BF16)
        else:
            gd = d // FOURIER_GROUPS
            cd, msd = _dft_mats(gd, gd ** -0.5)
            eye = jnp.eye(FOURIER_GROUPS, dtype=BF16)
            w_cs = jnp.concatenate([jnp.kron(eye, cd), jnp.kron(eye, -msd)], axis=1)
            (z,) = _proj_in(xs, mods, norm1_g[i], w_cs, jnp.zeros((2, LANES), F32), cos_t, sin_t, geom,
                            n_q=0, n_k=0, n_plain=2 * d // LANES)
            o = _seq_dft(z.reshape(batch, t, 2 * d), _dft_mats(seq, seq ** -0.5),
                         _dft_mats(ctx_len, ctx_len ** -0.5), geom, d)
            wo = f_wo[j].astype(BF16)
        xs, h2 = _proj_out(o, wo, xs, mods, norm2_g[i], geom)
        xs = _hier_moe(h2, xs, mods, geom, r_w1[i], r_b1[i], r_w2[i], r_b2[i], e_w13[i], e_w2[i])

    return xs.reshape(batch, t, d)[:, :seq]
```

```python
import functools
import math

import numpy as np
import jax
import jax.numpy as jnp
from jax import lax
from jax.experimental import pallas as pl
from jax.experimental.pallas import tpu as pltpu

F32 = jnp.float32
BF16 = jnp.bfloat16
HIGHEST = lax.Precision.HIGHEST

GRID_W = 64
HEAD_DIM = 64
ROT_FREQS = HEAD_DIM // 4
ROPE_THETA = 10000.0
WINDOW = 128
N_MIXERS = 3
SWA_KV_HEADS = 4
FOURIER_GROUPS = 4
N_GROUPS = 4
EXPERTS_PER_GROUP = 8
N_EXPERTS = N_GROUPS * EXPERTS_PER_GROUP
EPS = 1e-6
NEG_INF = -1e30

LANES = 128
SUBLANES = 8
MXU_DIM = 256
TOKEN_TILE = 256
MOE_TILE = 256
ROUTER_TILE = 512
VMEM_LIMIT = 48 * 1024 * 1024


def _cparams(*sem):
    return pltpu.CompilerParams(dimension_semantics=sem, vmem_limit_bytes=VMEM_LIMIT)


def _mod_row(i, tiles_per_batch, latent_tiles, batch):
    return jnp.where(i % tiles_per_batch >= latent_tiles, batch, i // tiles_per_batch)


def _ada_kernel(c_ref, w_ref, b_ref, o_ref):
    c = c_ref[...]
    s = c * (1.0 / (1.0 + jnp.exp(-c)))
    o_ref[...] = jnp.dot(s, w_ref[...], precision=HIGHEST, preferred_element_type=F32) + b_ref[...]


def _ada_table(cvec, ada_w, ada_b):
    depth, d, n6 = ada_w.shape
    rows = cvec.shape[0]
    tn = n6 // 4
    return pl.pallas_call(
        _ada_kernel,
        grid=(depth, n6 // tn),
        in_specs=[pl.BlockSpec((rows, d), lambda l, j: (0, 0)),
                  pl.BlockSpec((None, d, tn), lambda l, j: (l, 0, j)),
                  pl.BlockSpec((None, 1, tn), lambda l, j: (l, 0, j))],
        out_specs=pl.BlockSpec((None, rows, tn), lambda l, j: (l, 0, j)),
        out_shape=jax.ShapeDtypeStruct((depth, rows, n6), F32),
        compiler_params=_cparams("parallel", "parallel"),
        name="ada_table",
    )(cvec, ada_w, ada_b.reshape(depth, 1, n6))


def _head_mean_matrix():
    r = np.arange(LANES)
    return jnp.asarray((r[:, None] // HEAD_DIM == r[None, :] // HEAD_DIM).astype(np.float32) / HEAD_DIM, BF16)


def _proj_in_kernel(x_ref, mod_ref, g_ref, w_ref, hm_ref, gain_ref, cos_ref, sin_ref, *out_refs,
                    d, n_q, n_k, n_plain, ones_after_plain, q_scale):
    q_ref = out_refs[0] if n_q else None
    k_ref = out_refs[1] if n_k else None
    p_ref = out_refs[-1]
    x = x_ref[...]
    ms = jnp.mean(x * x, axis=-1, keepdims=True)
    h = x * lax.rsqrt(ms + EPS) * g_ref[...]
    h = h * (1.0 + mod_ref[:, d:2 * d]) + mod_ref[:, 0:d]
    hb = h.astype(BF16)
    lane = lax.broadcasted_iota(jnp.int32, (1, LANES), 1)
    first_half = (lane // ROT_FREQS) % 2 == 0
    n_chunks = n_q + n_k + n_plain
    for c2 in range(0, n_chunks, 2):
        width = min(2, n_chunks - c2) * LANES
        y2 = jnp.dot(hb, w_ref[:, c2 * LANES:c2 * LANES + width], preferred_element_type=F32)
        for half in range(width // LANES):
            c = c2 + half
            y = y2[:, half * LANES:(half + 1) * LANES]
            if c < n_q + n_k:
                msq = jnp.dot((y * y).astype(BF16), hm_ref[...], preferred_element_type=F32)
                is_q = c < n_q
                gain = gain_ref[0:1, :] if is_q else gain_ref[1:2, :]
                yn = y * lax.rsqrt(msq + EPS) * gain
                partner = jnp.where(first_half, pltpu.roll(yn, LANES - ROT_FREQS, 1), pltpu.roll(yn, ROT_FREQS, 1))
                out = yn * cos_ref[...] + partner * sin_ref[...]
                if is_q:
                    q_ref[:, c * LANES:(c + 1) * LANES] = (out * q_scale).astype(q_ref.dtype)
                else:
                    ck = c - n_q
                    k_ref[:, ck * LANES:(ck + 1) * LANES] = out.astype(k_ref.dtype)
            else:
                cp = c - n_q - n_k
                if ones_after_plain:
                    p_ref[:, 2 * cp * LANES:(2 * cp + 1) * LANES] = y.astype(p_ref.dtype)
                    p_ref[:, (2 * cp + 1) * LANES:(2 * cp + 2) * LANES] = jnp.ones(y.shape, p_ref.dtype)
                else:
                    p_ref[:, cp * LANES:(cp + 1) * LANES] = y.astype(p_ref.dtype)


def _proj_in(xs, mods, g, w, gains, cos_t, sin_t, geom, *, n_q, n_k, n_plain, ones_after_plain=False,
             q_scale=HEAD_DIM ** -0.5):
    n, d = xs.shape
    tm = TOKEN_TILE
    tpb, lt, batch = geom
    n_cols = (n_q + n_k + n_plain) * LANES
    assert w.shape == (d, n_cols)
    out_shape, out_specs = [], []
    for cnt in (n_q, n_k):
        if cnt:
            out_shape.append(jax.ShapeDtypeStruct((n, cnt * LANES), BF16))
            out_specs.append(pl.BlockSpec((tm, cnt * LANES), lambda i: (i, 0)))
    pw = n_plain * LANES * (2 if ones_after_plain else 1)
    out_shape.append(jax.ShapeDtypeStruct((n, pw), BF16))
    out_specs.append(pl.BlockSpec((tm, pw), lambda i: (i, 0)))
    kern = functools.partial(_proj_in_kernel, d=d, n_q=n_q, n_k=n_k, n_plain=n_plain,
                             ones_after_plain=ones_after_plain, q_scale=q_scale)
    return pl.pallas_call(
        kern,
        grid=(n // tm,),
        in_specs=[pl.BlockSpec((tm, d), lambda i: (i, 0)),
                  pl.BlockSpec((None, 1, mods.shape[-1]), lambda i: (_mod_row(i, tpb, lt, batch), 0, 0)),
                  pl.BlockSpec((1, d), lambda i: (0, 0)),
                  pl.BlockSpec((d, n_cols), lambda i: (0, 0)),
                  pl.BlockSpec((LANES, LANES), lambda i: (0, 0)),
                  pl.BlockSpec((2, LANES), lambda i: (0, 0)),
                  pl.BlockSpec((tm, LANES), lambda i: (i % tpb, 0)),
                  pl.BlockSpec((tm, LANES), lambda i: (i % tpb, 0))],
        out_specs=out_specs,
        out_shape=out_shape,
        compiler_params=_cparams("parallel"),
        name="proj_in",
    )(xs, mods, g.reshape(1, d), w, _head_mean_matrix(), gains, cos_t, sin_t)


def _diff_attn_kernel(lam_ref, q_ref, k_ref, v_ref, sg_ref, o_ref, s_ref, acc_ref, m_ref,
                      *, lam_init, seq, ctx_len, kv_chunk, latent_tiles):
    qt = pl.program_id(2)
    dv = o_ref.shape[1]
    q = q_ref[...]
    lane = lax.broadcasted_iota(jnp.int32, q.shape, 1)
    zero = jnp.zeros_like(q)
    q_maps = (jnp.where(lane < HEAD_DIM, q, zero), jnp.where(lane >= HEAD_DIM, q, zero))
    m_ref[...] = jnp.full(m_ref.shape, NEG_INF, F32)
    acc_ref[...] = jnp.zeros(acc_ref.shape, F32)
    latent_chunks = [(c * kv_chunk, kv_chunk) for c in range(seq // kv_chunk)]
    context_chunks = [(seq, ctx_len)]

    def score_sweep(chunks):
        for m in range(2):
            row_max = m_ref[m]
            for start, size in chunks:
                s = lax.dot_general(q_maps[m], k_ref[start:start + size, :], (((1,), (1,)), ((), ())),
                                    preferred_element_type=F32)
                s_ref[m, :, start:start + size] = s
                row_max = jnp.maximum(row_max, jnp.max(s, axis=1, keepdims=True))
            m_ref[m] = row_max

    def value_sweep(chunks):
        for m in range(2):
            row_max = m_ref[m]
            acc = acc_ref[m]
            for start, size in chunks:
                p = jnp.exp2(s_ref[m, :, start:start + size] - row_max)
                acc = acc + jnp.dot(p.astype(BF16), v_ref[start:start + size, :], preferred_element_type=F32)
            acc_ref[m] = acc

    @pl.when(qt < latent_tiles)
    def _():
        score_sweep(latent_chunks)

    score_sweep(context_chunks)

    @pl.when(qt < latent_tiles)
    def _():
        value_sweep(latent_chunks)

    value_sweep(context_chunks)

    lam_v = lam_ref[...]
    lam = (jnp.exp(jnp.sum(lam_v[0:1] * lam_v[1:2], axis=1, keepdims=True))
           - jnp.exp(jnp.sum(lam_v[2:3] * lam_v[3:4], axis=1, keepdims=True)) + lam_init)
    outs = []
    for m in range(2):
        acc = acc_ref[m]
        outs.append(acc[:, 0:dv] * (1.0 / acc[:, dv:dv + 1]))
    a = outs[0] - lam * outs[1]
    a = a * lax.rsqrt(jnp.mean(a * a, axis=-1, keepdims=True) + EPS) * sg_ref[...] * (1.0 - lam_init)
    o_ref[...] = a.astype(o_ref.dtype)


def _diff_attention(q, k, v1, lam_vecs, sub_g, geom, seq, ctx_len, lam_init):
    n, dq = q.shape
    tpb, lt, batch = geom
    t = tpb * TOKEN_TILE
    heads = dq // (2 * HEAD_DIM)
    dv = 2 * HEAD_DIM
    k3 = k.reshape(batch, t, dq)
    v3 = v1.reshape(batch, t, heads * 2 * dv)
    kern = functools.partial(_diff_attn_kernel, lam_init=lam_init, seq=seq, ctx_len=ctx_len,
                             kv_chunk=512, latent_tiles=lt)
    return pl.pallas_call(
        kern,
        grid=(batch, heads, tpb),
        in_specs=[pl.BlockSpec((4, HEAD_DIM), lambda b, h, i: (0, 0)),
                  pl.BlockSpec((TOKEN_TILE, 2 * HEAD_DIM), lambda b, h, i: (b * tpb + i, h)),
                  pl.BlockSpec((None, t, 2 * HEAD_DIM), lambda b, h, i: (b, 0, h)),
                  pl.BlockSpec((None, t, 2 * dv), lambda b, h, i: (b, 0, h)),
                  pl.BlockSpec((1, dv), lambda b, h, i: (0, 0))],
        out_specs=pl.BlockSpec((TOKEN_TILE, dv), lambda b, h, i: (b * tpb + i, h)),
        out_shape=jax.ShapeDtypeStruct((n, heads * dv), BF16),
        scratch_shapes=[pltpu.VMEM((2, TOKEN_TILE, t), F32), pltpu.VMEM((2, TOKEN_TILE, 2 * dv), F32),
                        pltpu.VMEM((2, TOKEN_TILE, 1), F32)],
        compiler_params=_cparams("parallel", "parallel", "arbitrary"),
        name="diff_attention",
    )(lam_vecs, q, k3, v3, sub_g.reshape(1, dv))


def _swa_kernel(sink_ref, q_ref, k_ref, v_ref, o_ref, *, seq, ctx_len, latent_tiles, group):
    i = pl.program_id(1)
    tq = q_ref.shape[0]
    span = tq + 2 * WINDOW
    start = pl.multiple_of(jnp.clip(i * tq - WINDOW, 0, seq - span), WINDOW)
    qpos = i * tq + lax.broadcasted_iota(jnp.int32, (tq, span), 0)
    kpos = start + lax.broadcasted_iota(jnp.int32, (tq, span), 1)
    valid = (jnp.abs(kpos - qpos) <= WINDOW) & (i < latent_tiles)
    bias = jnp.concatenate([jnp.where(valid, 0.0, NEG_INF).astype(F32), jnp.zeros((tq, ctx_len), F32)], axis=1)
    kw = k_ref[pl.ds(start, span), :]
    vw = v_ref[pl.ds(start, span), :]
    kc = k_ref[pl.ds(seq, ctx_len), :]
    vc = v_ref[pl.ds(seq, ctx_len), :]
    for h in range(SWA_KV_HEADS):
        hs = slice(h * LANES, (h + 1) * LANES)
        keys = jnp.concatenate([kw[:, hs], kc[:, hs]], axis=0)
        vals = jnp.concatenate([vw[:, hs], vc[:, hs]], axis=0)
        for g in range(group):
            hq = h * group + g
            q = q_ref[:, hq * LANES:(hq + 1) * LANES]
            s = lax.dot_general(q, keys, (((1,), (1,)), ((), ())), preferred_element_type=F32) + bias
            sink = sink_ref[hq]
            m = jnp.maximum(jnp.max(s, axis=1, keepdims=True), sink)
            p = jnp.exp(s - m)
            denom = jnp.sum(p, axis=1, keepdims=True) + jnp.exp(sink - m)
            o = jnp.dot(p.astype(BF16), vals, preferred_element_type=F32) * (1.0 / denom)
            o_ref[:, hq * LANES:(hq + 1) * LANES] = o.astype(o_ref.dtype)


def _swa_attention(q, k, v, sink, geom, seq, ctx_len):
    n, dq = q.shape
    tpb, lt, batch = geom
    t = tpb * TOKEN_TILE
    q_heads = dq // LANES
    k3 = k.reshape(batch, t, SWA_KV_HEADS * LANES)
    v3 = v.reshape(batch, t, SWA_KV_HEADS * LANES)
    kern = functools.partial(_swa_kernel, seq=seq, ctx_len=ctx_len, latent_tiles=lt,
                             group=q_heads // SWA_KV_HEADS)
    return pl.pallas_call(
        kern,
        grid=(batch, tpb),
        in_specs=[pl.BlockSpec(memory_space=pltpu.SMEM),
                  pl.BlockSpec((TOKEN_TILE, dq), lambda b, i: (b * tpb + i, 0)),
                  pl.BlockSpec((None, t, SWA_KV_HEADS * LANES), lambda b, i: (b, 0, 0)),
                  pl.BlockSpec((None, t, SWA_KV_HEADS * LANES), lambda b, i: (b, 0, 0))],
        out_specs=pl.BlockSpec((TOKEN_TILE, dq), lambda b, i: (b * tpb + i, 0)),
        out_shape=jax.ShapeDtypeStruct((n, dq), BF16),
        compiler_params=_cparams("parallel", "arbitrary"),
        name="swa_attention",
    )(sink, q, k3, v3)


def _seq_dft_kernel(ac_ref, as_ref, cc_ref, cs_ref, zc_ref, zs_ref, yc_ref, ys_ref, o_ref, *, latent_tiles):
    m = pl.program_id(2)

    def mix(a_cos, a_msin, z_cos, z_sin):
        acc = jnp.dot(a_cos[...], z_cos[...], preferred_element_type=F32)
        acc = acc + jnp.dot(a_msin[...], z_sin[...], preferred_element_type=F32)
        o_ref[...] = acc.astype(o_ref.dtype)

    @pl.when(m < latent_tiles)
    def _():
        mix(ac_ref, as_ref, zc_ref, zs_ref)

    @pl.when(m >= latent_tiles)
    def _():
        mix(cc_ref, cs_ref, yc_ref, ys_ref)


def _seq_dft(z3, lat_mats, ctx_mats, geom, d):
    tpb, lt, batch = geom
    seq, ctx_len = lat_mats[0].shape[0], ctx_mats[0].shape[0]
    assert ctx_len == TOKEN_TILE and seq % ctx_len == 0
    tm = TOKEN_TILE
    tn = d // 2
    n_col = d // tn
    lat_a = pl.BlockSpec((tm, seq), lambda b, j, m: (jnp.minimum(m, lt - 1), 0))
    ctx_a = pl.BlockSpec((ctx_len, ctx_len), lambda b, j, m: (0, 0))
    return pl.pallas_call(
        functools.partial(_seq_dft_kernel, latent_tiles=lt),
        grid=(batch, n_col, tpb),
        in_specs=[lat_a, lat_a, ctx_a, ctx_a,
                  pl.BlockSpec((None, seq, tn), lambda b, j, m: (b, 0, j)),
                  pl.BlockSpec((None, seq, tn), lambda b, j, m: (b, 0, n_col + j)),
                  pl.BlockSpec((None, ctx_len, tn), lambda b, j, m: (b, seq // ctx_len, j)),
                  pl.BlockSpec((None, ctx_len, tn), lambda b, j, m: (b, seq // ctx_len, n_col + j))],
        out_specs=pl.BlockSpec((tm, tn), lambda b, j, m: (b * tpb + m, j)),
        out_shape=jax.ShapeDtypeStruct((batch * tpb * tm, d), BF16),
        compiler_params=_cparams("parallel", "parallel", "arbitrary"),
        name="seq_dft",
    )(*lat_mats, *ctx_mats, z3, z3, z3, z3)


def _dft_mats(n, scale):
    idx = (jnp.arange(n, dtype=jnp.int32)[:, None] * jnp.arange(n, dtype=jnp.int32)[None, :]) % n
    ang = idx.astype(F32) * (2.0 * math.pi / n)
    return (jnp.cos(ang) * scale).astype(BF16), (-jnp.sin(ang) * scale).astype(BF16)


def _proj_out_kernel(o_ref, w_ref, x_ref, mod_ref, g_ref, xo_ref, h_ref, *, d):
    y = jnp.dot(o_ref[...], w_ref[...], preferred_element_type=F32)
    x = x_ref[...] + mod_ref[:, 2 * d:3 * d] * y
    xo_ref[...] = x
    ms = jnp.mean(x * x, axis=-1, keepdims=True)
    h = x * lax.rsqrt(ms + EPS) * g_ref[...]
    h_ref[...] = h * (1.0 + mod_ref[:, 4 * d:5 * d]) + mod_ref[:, 3 * d:4 * d]


def _proj_out(o, w, xs, mods, g2, geom):
    n, d = xs.shape
    ko = o.shape[1]
    tm = TOKEN_TILE
    tpb, lt, batch = geom
    return pl.pallas_call(
        functools.partial(_proj_out_kernel, d=d),
        grid=(n // tm,),
        in_specs=[pl.BlockSpec((tm, ko), lambda i: (i, 0)),
                  pl.BlockSpec((ko, d), lambda i: (0, 0)),
                  pl.BlockSpec((tm, d), lambda i: (i, 0)),
                  pl.BlockSpec((None, 1, mods.shape[-1]), lambda i: (_mod_row(i, tpb, lt, batch), 0, 0)),
                  pl.BlockSpec((1, d), lambda i: (0, 0))],
        out_specs=[pl.BlockSpec((tm, d), lambda i: (i, 0)), pl.BlockSpec((tm, d), lambda i: (i, 0))],
        out_shape=[jax.ShapeDtypeStruct((n, d), F32), jax.ShapeDtypeStruct((n, d), F32)],
        compiler_params=_cparams("parallel"),
        name="proj_out",
    )(o, w, xs, mods, g2.reshape(1, d))


ROUTER_ROWS = SUBLANES + N_EXPERTS


def _router_kernel(h_ref, wt_ref, b_ref, tri_ref, ints_ref, flt_ref, cnt_ref, carry_ref):
    step = pl.program_id(0)
    tr = h_ref.shape[0]

    @pl.when(step == 0)
    def _():
        carry_ref[...] = jnp.zeros(carry_ref.shape, F32)

    logits = lax.dot_general(wt_ref[...], h_ref[...], (((1,), (1,)), ((), ())),
                             precision=HIGHEST, preferred_element_type=F32) + b_ref[:, 0:1]
    row8 = lax.broadcasted_iota(jnp.int32, (SUBLANES, tr), 0)
    lg = jnp.where(row8 < N_GROUPS, logits[0:SUBLANES], NEG_INF)
    lg_max = jnp.max(lg, axis=0, keepdims=True)
    pg = 1.0 / jnp.sum(jnp.exp(lg - lg_max), axis=0, keepdims=True)
    grp = jnp.min(jnp.where(lg == lg_max, row8, SUBLANES), axis=0, keepdims=True)
    l2 = jnp.zeros((EXPERTS_PER_GROUP, tr), F32)
    for g in range(N_GROUPS):
        lo = SUBLANES + g * EXPERTS_PER_GROUP
        l2 = l2 + jnp.where(grp == g, logits[lo:lo + EXPERTS_PER_GROUP], 0.0)
    l2_max = jnp.max(l2, axis=0, keepdims=True)
    j0 = jnp.min(jnp.where(l2 == l2_max, row8, SUBLANES), axis=0, keepdims=True)
    rest = jnp.where(row8 == j0, NEG_INF, l2)
    r_max = jnp.max(rest, axis=0, keepdims=True)
    j1 = jnp.min(jnp.where(rest == r_max, row8, SUBLANES), axis=0, keepdims=True)
    e1 = jnp.exp(r_max - l2_max)
    inv = 1.0 / (1.0 + e1)
    w0 = pg * inv
    w1 = pg * e1 * inv
    ex0 = grp * EXPERTS_PER_GROUP + j0
    ex1 = grp * EXPERTS_PER_GROUP + j1

    rows = lax.broadcasted_iota(jnp.int32, (N_EXPERTS, tr), 0)
    oh0 = (rows == ex0).astype(F32)
    oh1 = (rows == ex1).astype(F32)
    both = oh0 + oh1
    before = jnp.dot(both.astype(BF16), tri_ref[...], preferred_element_type=F32) + carry_ref[:, 0:1]
    rank0 = jnp.sum(oh0 * before, axis=0, keepdims=True)
    rank1 = jnp.sum(oh1 * before, axis=0, keepdims=True)
    carry_ref[...] = carry_ref[...] + jnp.sum(both, axis=1, keepdims=True)
    cnt_ref[...] = carry_ref[...]

    zi = jnp.zeros((SUBLANES - 4, tr), jnp.int32)
    ints_ref[...] = jnp.concatenate([ex0, ex1, rank0.astype(jnp.int32), rank1.astype(jnp.int32), zi], axis=0)
    flt_ref[...] = jnp.concatenate([w0, w1, jnp.zeros((SUBLANES - 2, tr), F32)], axis=0)


def _router(h2, wt, bias, tri):
    n, d = h2.shape
    tr = ROUTER_TILE
    return pl.pallas_call(
        _router_kernel,
        grid=(n // tr,),
        in_specs=[pl.BlockSpec((tr, d), lambda i: (i, 0)),
                  pl.BlockSpec((ROUTER_ROWS, d), lambda i: (0, 0)),
                  pl.BlockSpec((ROUTER_ROWS, LANES), lambda i: (0, 0)),
                  pl.BlockSpec((tr, tr), lambda i: (0, 0))],
        out_specs=[pl.BlockSpec((SUBLANES, tr), lambda i: (0, i)),
                   pl.BlockSpec((SUBLANES, tr), lambda i: (0, i)),
                   pl.BlockSpec((N_EXPERTS, LANES), lambda i: (0, 0))],
        out_shape=[jax.ShapeDtypeStruct((SUBLANES, n), jnp.int32),
                   jax.ShapeDtypeStruct((SUBLANES, n), F32),
                   jax.ShapeDtypeStruct((N_EXPERTS, LANES), F32)],
        scratch_shapes=[pltpu.VMEM((N_EXPERTS, LANES), F32)],
        compiler_params=_cparams("arbitrary"),
        name="router",
    )(h2, wt, bias, tri)


def _dest_kernel(pstart_ref, ints_ref, o_ref):
    ints = ints_ref[...]
    ex = ints[0:2]
    base = jnp.zeros(ex.shape, jnp.int32)
    for e in range(N_EXPERTS):
        base = jnp.where(ex == e, pstart_ref[e], base)
    o_ref[...] = jnp.concatenate([base + ints[2:4], jnp.zeros((SUBLANES - 2, ints.shape[1]), jnp.int32)], axis=0)


def _dest_rows(pstart, ints):
    n = ints.shape[1]
    tn = 2048 if n % 2048 == 0 else ROUTER_TILE
    return pl.pallas_call(
        _dest_kernel,
        grid=(n // tn,),
        in_specs=[pl.BlockSpec(memory_space=pltpu.SMEM),
                  pl.BlockSpec((SUBLANES, tn), lambda i: (0, i))],
        out_specs=pl.BlockSpec((SUBLANES, tn), lambda i: (0, i)),
        out_shape=jax.ShapeDtypeStruct((SUBLANES, n), jnp.int32),
        compiler_params=_cparams("parallel"),
        name="dest_rows",
    )(pstart, ints)


DISPATCH_TILE = 512


def _dispatch_kernel(meta_ref, d0_ref, d1_ref, h_ref, xs_ref, zero_ref, sem_ref, zsem_ref, *, n_blocks):
    step = pl.program_id(0)
    td = h_ref.shape[0]

    @pl.when(step == 0)
    def _():
        zero_ref[...] = jnp.zeros(zero_ref.shape, zero_ref.dtype)

        def zero_expert(e, carry):
            def zero_row(r, c):
                pltpu.make_async_copy(zero_ref.at[pl.ds(0, 1)], xs_ref.at[pl.ds(meta_ref[e] + r, 1)], zsem_ref).start()
                return c
            lax.fori_loop(0, meta_ref[N_EXPERTS + e], zero_row, 0)

            def wait_row(r, c):
                pltpu.make_async_copy(zero_ref.at[pl.ds(0, 1)], xs_ref.at[pl.ds(0, 1)], zsem_ref).wait()
                return c
            lax.fori_loop(0, meta_ref[N_EXPERTS + e], wait_row, 0)
            return carry
        lax.fori_loop(0, N_EXPERTS, zero_expert, 0)

        def zero_block(b, carry):
            cp = pltpu.make_async_copy(zero_ref, xs_ref.at[pl.ds(b * MOE_TILE, MOE_TILE)], zsem_ref)
            cp.start()
            cp.wait()
            return carry
        lax.fori_loop(meta_ref[2 * N_EXPERTS], n_blocks, zero_block, 0)

    def issue(r, carry):
        src = h_ref.at[pl.ds(r, 1)]
        pltpu.make_async_copy(src, xs_ref.at[pl.ds(d0_ref[r], 1)], sem_ref.at[0]).start()
        pltpu.make_async_copy(src, xs_ref.at[pl.ds(d1_ref[r], 1)], sem_ref.at[1]).start()
        return carry
    lax.fori_loop(0, td, issue, 0, unroll=8)
    for k in range(2):
        pltpu.make_async_copy(h_ref, xs_ref.at[pl.ds(0, td)], sem_ref.at[k]).wait()


def _dispatch(meta, dest0, dest1, h2, n_rows):
    n, d = h2.shape
    td = DISPATCH_TILE
    n_blocks = n_rows // MOE_TILE
    smem = functools.partial(pl.BlockSpec, memory_space=pltpu.SMEM)
    return pl.pallas_call(
        functools.partial(_dispatch_kernel, n_blocks=n_blocks),
        grid=(n // td,),
        in_specs=[smem(),
                  smem((td,), lambda i: (i,)),
                  smem((td,), lambda i: (i,)),
                  pl.BlockSpec((td, d), lambda i: (i, 0))],
        out_specs=pl.BlockSpec(memory_space=pl.ANY),
        out_shape=jax.ShapeDtypeStruct((n_rows, d), F32),
        scratch_shapes=[pltpu.VMEM((MOE_TILE, d), F32), pltpu.SemaphoreType.DMA((2,)),
                        pltpu.SemaphoreType.DMA(())],
        compiler_params=_cparams("arbitrary"),
        name="moe_dispatch",
    )(meta, dest0, dest1, h2)


def _expert_kernel(blk_exp_ref, n_used_ref, x_ref, w13_ref, w2_ref, y_ref, *, d_expert):
    i = pl.program_id(0)

    @pl.when(i < n_used_ref[0])
    def _():
        gu = jnp.dot(x_ref[...].astype(BF16), w13_ref[...], preferred_element_type=F32)
        g = gu[:, 0:d_expert]
        u = gu[:, d_expert:2 * d_expert]
        a = g * (1.0 / (1.0 + jnp.exp(-g))) * u
        y_ref[...] = jnp.dot(a.astype(BF16), w2_ref[...], preferred_element_type=F32)

    @pl.when(i >= n_used_ref[0])
    def _():
        y_ref[...] = jnp.zeros(y_ref.shape, F32)


def _experts(blk_exp, n_used, x_sorted, w13, w2):
    n_rows, d = x_sorted.shape
    d_expert = w2.shape[1]
    n_blocks = n_rows // MOE_TILE
    grid_spec = pltpu.PrefetchScalarGridSpec(
        num_scalar_prefetch=2,
        grid=(n_blocks,),
        in_specs=[pl.BlockSpec((MOE_TILE, d), lambda i, be, nu: (i, 0)),
                  pl.BlockSpec((None, d, 2 * d_expert), lambda i, be, nu: (be[i], 0, 0)),
                  pl.BlockSpec((None, d_expert, d), lambda i, be, nu: (be[i], 0, 0))],
        out_specs=pl.BlockSpec((MOE_TILE, d), lambda i, be, nu: (i, 0)),
    )
    return pl.pallas_call(
        functools.partial(_expert_kernel, d_expert=d_expert),
        grid_spec=grid_spec,
        out_shape=jax.ShapeDtypeStruct((n_rows, d), F32),
        compiler_params=_cparams("arbitrary"),
        name="moe_experts",
    )(blk_exp, n_used, x_sorted, w13, w2)


def _combine_kernel(d0_ref, d1_ref, w_ref, x_ref, mod_ref, y_ref, o_ref, b0_ref, b1_ref, sem_ref, *, d):
    tc = x_ref.shape[0]

    def issue(r, carry):
        pltpu.make_async_copy(y_ref.at[pl.ds(d0_ref[r], 1)], b0_ref.at[pl.ds(r, 1)], sem_ref.at[0]).start()
        pltpu.make_async_copy(y_ref.at[pl.ds(d1_ref[r], 1)], b1_ref.at[pl.ds(r, 1)], sem_ref.at[1]).start()
        return carry
    lax.fori_loop(0, tc, issue, 0, unroll=8)
    pltpu.make_async_copy(y_ref.at[pl.ds(0, tc)], b0_ref, sem_ref.at[0]).wait()
    pltpu.make_async_copy(y_ref.at[pl.ds(0, tc)], b1_ref, sem_ref.at[1]).wait()
    out = w_ref[:, 0:1] * b0_ref[...] + w_ref[:, 1:2] * b1_ref[...]
    o_ref[...] = x_ref[...] + mod_ref[:, 5 * d:6 * d] * out


def _combine(dest0, dest1, w_cols, xs, mods, y_sorted, geom):
    n, d = xs.shape
    tc = TOKEN_TILE
    tpb, lt, batch = geom
    smem = functools.partial(pl.BlockSpec, memory_space=pltpu.SMEM)
    return pl.pallas_call(
        functools.partial(_combine_kernel, d=d),
        grid=(n // tc,),
        in_specs=[smem((tc,), lambda i: (i,)),
                  smem((tc,), lambda i: (i,)),
                  pl.BlockSpec((tc, 2), lambda i: (i, 0)),
                  pl.BlockSpec((tc, d), lambda i: (i, 0)),
                  pl.BlockSpec((None, 1, mods.shape[-1]), lambda i: (_mod_row(i, tpb, lt, batch), 0, 0)),
                  pl.BlockSpec(memory_space=pl.ANY)],
        out_specs=pl.BlockSpec((tc, d), lambda i: (i, 0)),
        out_shape=jax.ShapeDtypeStruct((n, d), F32),
        scratch_shapes=[pltpu.VMEM((tc, d), F32), pltpu.VMEM((tc, d), F32), pltpu.SemaphoreType.DMA((2,))],
        compiler_params=_cparams("arbitrary"),
        name="moe_combine",
    )(dest0, dest1, w_cols, xs, mods, y_sorted)


def _hier_moe(h2, xs, mods, geom, w_r1, b_r1, w_r2, b_r2, w13, w2):
    n, d = h2.shape
    wt = jnp.zeros((ROUTER_ROWS, d), F32)
    wt = wt.at[0:N_GROUPS].set(w_r1.T)
    wt = wt.at[SUBLANES:].set(jnp.transpose(w_r2, (0, 2, 1)).reshape(N_EXPERTS, d))
    bias = jnp.zeros((ROUTER_ROWS,), F32).at[0:N_GROUPS].set(b_r1).at[SUBLANES:].set(b_r2.reshape(-1))
    bias = jnp.broadcast_to(bias[:, None], (ROUTER_ROWS, LANES))
    tri = jnp.asarray(np.triu(np.ones((ROUTER_TILE, ROUTER_TILE), np.float32), 1), BF16)
    ints, flt, cnt = _router(h2, wt, bias, tri)

    counts = cnt[:, 0].astype(jnp.int32)
    padded = ((counts + MOE_TILE - 1) // MOE_TILE) * MOE_TILE
    pend = jnp.cumsum(padded)
    pstart = pend - padded
    n_blocks = (2 * n) // MOE_TILE + N_EXPERTS
    n_rows = n_blocks * MOE_TILE
    n_used = pend[-1] // MOE_TILE
    blk_row0 = jnp.arange(n_blocks, dtype=jnp.int32) * MOE_TILE
    blk_exp = jnp.minimum(jnp.sum(pend[None, :] <= blk_row0[:, None], axis=1), N_EXPERTS - 1).astype(jnp.int32)
    meta = jnp.concatenate([pstart + counts, padded - counts, n_used[None]]).astype(jnp.int32)

    dest = _dest_rows(pstart.astype(jnp.int32), ints)
    dest0, dest1 = dest[0], dest[1]
    x_sorted = _dispatch(meta, dest0, dest1, h2, n_rows)
    y_sorted = _experts(blk_exp, n_used[None].astype(jnp.int32), x_sorted, w13.astype(BF16), w2.astype(BF16))
    w_cols = flt[0:2].T
    return _combine(dest0, dest1, w_cols, xs, mods, y_sorted, geom)


def _rope_tables(seq, ctx_len):
    rows = seq // GRID_W
    row = jnp.repeat(jnp.arange(rows, dtype=F32), GRID_W)
    col = jnp.tile(jnp.arange(GRID_W, dtype=F32), rows)
    inv = ROPE_THETA ** (-jnp.arange(ROT_FREQS, dtype=F32) / ROT_FREQS)
    ang_r, ang_c = row[:, None] * inv, col[:, None] * inv
    cos_h = jnp.concatenate([jnp.cos(ang_r)] * 2 + [jnp.cos(ang_c)] * 2, axis=1)
    sin_h = jnp.concatenate([-jnp.sin(ang_r), jnp.sin(ang_r), -jnp.sin(ang_c), jnp.sin(ang_c)], axis=1)
    cos_t = jnp.concatenate([jnp.tile(cos_h, (1, 2)), jnp.ones((ctx_len, LANES), F32)], axis=0)
    sin_t = jnp.concatenate([jnp.tile(sin_h, (1, 2)), jnp.zeros((ctx_len, LANES), F32)], axis=0)
    return cos_t, sin_t


def _pad_heads_cols(w, n_heads):
    d = w.shape[0]
    w3 = w.reshape(d, n_heads, HEAD_DIM)
    return jnp.concatenate([w3, jnp.zeros_like(w3)], axis=2).reshape(d, n_heads * LANES)


def kernel(x, c, ctx, c_ctx, ada_w, ada_b, norm1_g, norm2_g, a_wqkv, a_wo, a_q_norm, a_k_norm, a_lambda_q1, a_lambda_k1, a_lambda_q2, a_lambda_k2, a_subln_g, b_wqkv, b_wo, b_q_norm, b_k_norm, b_sink, f_wo, r_w1, r_b1, r_w2, r_b2, e_w13, e_w2):
    batch, seq, d = x.shape
    ctx_len = ctx.shape[1]
    depth = ada_w.shape[0]
    t = seq + ctx_len
    n = batch * t
    assert seq % TOKEN_TILE == 0 and ctx_len % TOKEN_TILE == 0 and n % ROUTER_TILE == 0
    tpb, lt = t // TOKEN_TILE, seq // TOKEN_TILE
    geom = (tpb, lt, batch)

    xs = jnp.concatenate([x, ctx], axis=1).reshape(n, d)
    mod_rows = ((batch + 1 + SUBLANES - 1) // SUBLANES) * SUBLANES
    cvec = jnp.zeros((mod_rows, d), F32).at[0:batch].set(c).at[batch].set(c_ctx)
    mods_all = _ada_table(cvec, ada_w, ada_b).reshape(depth, mod_rows, 1, 6 * d)
    cos_t, sin_t = _rope_tables(seq, ctx_len)

    for i in range(depth):
        kind, j = i % N_MIXERS, i // N_MIXERS
        mods = mods_all[i]
        if kind == 0:
            lam_init = 0.8 - 0.6 * math.exp(-0.3 * i)
            nh = d // LANES
            gains = jnp.stack([jnp.tile(a_q_norm[j], 2), jnp.tile(a_k_norm[j], 2)])
            q, k, v1 = _proj_in(xs, mods, norm1_g[i], a_wqkv[j].astype(BF16), gains, cos_t, sin_t, geom,
                                n_q=nh, n_k=nh, n_plain=nh, ones_after_plain=True,
                                q_scale=HEAD_DIM ** -0.5 * math.log2(math.e))
            lam_vecs = jnp.stack([a_lambda_q1[j], a_lambda_k1[j], a_lambda_q2[j], a_lambda_k2[j]])
            o = _diff_attention(q, k, v1, lam_vecs, a_subln_g[j], geom, seq, ctx_len, lam_init)
            wo = a_wo[j].astype(BF16)
        elif kind == 1:
            q_heads = d // HEAD_DIM
            nq, nk = q_heads * HEAD_DIM, SWA_KV_HEADS * HEAD_DIM
            w = b_wqkv[j]
            w_pad = jnp.concatenate([_pad_heads_cols(w[:, :nq], q_heads),
                                     _pad_heads_cols(w[:, nq:nq + nk], SWA_KV_HEADS),
                                     _pad_heads_cols(w[:, nq + nk:], SWA_KV_HEADS)], axis=1).astype(BF16)
            zeros64 = jnp.zeros((HEAD_DIM,), F32)
            gains = jnp.stack([jnp.concatenate([b_q_norm[j], zeros64]), jnp.concatenate([b_k_norm[j], zeros64])])
            q, k, v = _proj_in(xs, mods, norm1_g[i], w_pad, gains, cos_t, sin_t, geom,
                               n_q=q_heads, n_k=SWA_KV_HEADS, n_plain=SWA_KV_HEADS)
            o = _swa_attention(q, k, v, b_sink[j], geom, seq, ctx_len)
            wo3 = b_wo[j].reshape(q_heads, HEAD_DIM, d)
            wo = jnp.concatenate([wo3, jnp.zeros_like(wo3)], axis=1).reshape(q_heads * LANES, d).astype(BF16)
        else:
            gd = d // FOURIER_GROUPS
            cd, msd = _dft_mats(gd, gd ** -0.5)
            eye = jnp.eye(FOURIER_GROUPS, dtype=BF16)
            w_cs = jnp.concatenate([jnp.kron(eye, cd), jnp.kron(eye, -msd)], axis=1)
            (z,) = _proj_in(xs, mods, norm1_g[i], w_cs, jnp.zeros((2, LANES), F32), cos_t, sin_t, geom,
                            n_q=0, n_k=0, n_plain=2 * d // LANES)
            o = _seq_dft(z.reshape(batch, t, 2 * d), _dft_mats(seq, seq ** -0.5),
                         _dft_mats(ctx_len, ctx_len ** -0.5), geom, d)
            wo = f_wo[j].astype(BF16)
        xs, h2 = _proj_out(o, wo, xs, mods, norm2_g[i], geom)
        xs = _hier_moe(h2, xs, mods, geom, r_w1[i], r_b1[i], r_w2[i], r_b2[i], e_w13[i], e_w2[i])

    return xs.reshape(batch, t, d)[:, :seq]
```

```python
import functools
import math
from typing import NamedTuple

import numpy as np
import jax
import jax.numpy as jnp
from jax import lax
from jax.experimental import pallas as pl
from jax.experimental.pallas import tpu as pltpu

F32 = jnp.float32
BF16 = jnp.bfloat16
HIGHEST = lax.Precision.HIGHEST

GRID_W = 64
HEAD_DIM = 64
ROT_FREQS = HEAD_DIM // 4
ROPE_THETA = 10000.0
WINDOW = 128
N_MIXERS = 3
SWA_KV_HEADS = 4
FOURIER_GROUPS = 4
N_GROUPS = 4
EXPERTS_PER_GROUP = 8
N_EXPERTS = N_GROUPS * EXPERTS_PER_GROUP
EPS = 1e-6
NEG_INF = -1e30

LANES = 128
SUBLANES = 8
MXU_DIM = 256
TOKEN_TILE = 256
MOE_TILE = 256
ROUTER_TILE = 512
VMEM_LIMIT = 48 * 1024 * 1024


def _cparams(*sem):
    return pltpu.CompilerParams(dimension_semantics=sem, vmem_limit_bytes=VMEM_LIMIT)


class Geom(NamedTuple):
    batch: int
    seq: int
    ctx_len: int

    @property
    def lat_tiles(self):
        return self.seq // TOKEN_TILE

    @property
    def n_lat_tiles(self):
        return self.batch * self.lat_tiles

    @property
    def n_tiles(self):
        return self.n_lat_tiles + self.batch * (self.ctx_len // TOKEN_TILE)

    def mod_row(self, i):
        return jnp.where(i < self.n_lat_tiles, i // self.lat_tiles, self.batch)

    def rope_row(self, i):
        return jnp.where(i < self.n_lat_tiles, i % self.lat_tiles, self.lat_tiles)

    def ctx_tile(self, b):
        return self.n_lat_tiles + b


def _ada_kernel(c_ref, w_ref, b_ref, o_ref):
    c = c_ref[...]
    s = c * (1.0 / (1.0 + jnp.exp(-c)))
    o_ref[...] = jnp.dot(s, w_ref[...], precision=HIGHEST, preferred_element_type=F32) + b_ref[...]


def _ada_table(cvec, ada_w, ada_b):
    depth, d, n6 = ada_w.shape
    rows = cvec.shape[0]
    tn = n6 // 4
    return pl.pallas_call(
        _ada_kernel,
        grid=(depth, n6 // tn),
        in_specs=[pl.BlockSpec((rows, d), lambda l, j: (0, 0)),
                  pl.BlockSpec((None, d, tn), lambda l, j: (l, 0, j)),
                  pl.BlockSpec((None, 1, tn), lambda l, j: (l, 0, j))],
        out_specs=pl.BlockSpec((None, rows, tn), lambda l, j: (l, 0, j)),
        out_shape=jax.ShapeDtypeStruct((depth, rows, n6), F32),
        compiler_params=_cparams("parallel", "parallel"),
        name="ada_table",
    )(cvec, ada_w, ada_b.reshape(depth, 1, n6))


def _head_mean_matrix():
    r = np.arange(LANES)
    return jnp.asarray((r[:, None] // HEAD_DIM == r[None, :] // HEAD_DIM).astype(np.float32) / HEAD_DIM, BF16)


def _proj_in_kernel(x_ref, mod_ref, g_ref, w_ref, hm_ref, gain_ref, cos_ref, sin_ref, *out_refs,
                    d, n_q, n_k, n_plain, ones_after_plain, q_scale):
    q_ref = out_refs[0] if n_q else None
    k_ref = out_refs[1] if n_k else None
    p_ref = out_refs[-1]
    x = x_ref[...]
    ms = jnp.mean(x * x, axis=-1, keepdims=True)
    h = x * lax.rsqrt(ms + EPS) * g_ref[...]
    h = h * (1.0 + mod_ref[:, d:2 * d]) + mod_ref[:, 0:d]
    hb = h.astype(BF16)
    lane = lax.broadcasted_iota(jnp.int32, (1, LANES), 1)
    first_half = (lane // ROT_FREQS) % 2 == 0
    n_chunks = n_q + n_k + n_plain
    for c2 in range(0, n_chunks, 2):
        width = min(2, n_chunks - c2) * LANES
        y2 = jnp.dot(hb, w_ref[:, c2 * LANES:c2 * LANES + width], preferred_element_type=F32)
        for half in range(width // LANES):
            c = c2 + half
            y = y2[:, half * LANES:(half + 1) * LANES]
            if c < n_q + n_k:
                msq = jnp.dot((y * y).astype(BF16), hm_ref[...], preferred_element_type=F32)
                is_q = c < n_q
                gain = gain_ref[0:1, :] if is_q else gain_ref[1:2, :]
                yn = y * lax.rsqrt(msq + EPS) * gain
                partner = jnp.where(first_half, pltpu.roll(yn, LANES - ROT_FREQS, 1), pltpu.roll(yn, ROT_FREQS, 1))
                out = yn * cos_ref[...] + partner * sin_ref[...]
                if is_q:
                    q_ref[:, c * LANES:(c + 1) * LANES] = (out * q_scale).astype(q_ref.dtype)
                else:
                    ck = c - n_q
                    k_ref[:, ck * LANES:(ck + 1) * LANES] = out.astype(k_ref.dtype)
            else:
                cp = c - n_q - n_k
                if ones_after_plain:
                    p_ref[:, 2 * cp * LANES:(2 * cp + 1) * LANES] = y.astype(p_ref.dtype)
                    p_ref[:, (2 * cp + 1) * LANES:(2 * cp + 2) * LANES] = jnp.ones(y.shape, p_ref.dtype)
                else:
                    p_ref[:, cp * LANES:(cp + 1) * LANES] = y.astype(p_ref.dtype)


def _proj_in(xs, mods, g, w, gains, cos_t, sin_t, geom, *, n_q, n_k, n_plain, ones_after_plain=False,
             q_scale=HEAD_DIM ** -0.5):
    n, d = xs.shape
    tm = TOKEN_TILE
    n_cols = (n_q + n_k + n_plain) * LANES
    assert w.shape == (d, n_cols)
    out_shape, out_specs = [], []
    for cnt in (n_q, n_k):
        if cnt:
            out_shape.append(jax.ShapeDtypeStruct((n, cnt * LANES), BF16))
            out_specs.append(pl.BlockSpec((tm, cnt * LANES), lambda i: (i, 0)))
    pw = n_plain * LANES * (2 if ones_after_plain else 1)
    out_shape.append(jax.ShapeDtypeStruct((n, pw), BF16))
    out_specs.append(pl.BlockSpec((tm, pw), lambda i: (i, 0)))
    kern = functools.partial(_proj_in_kernel, d=d, n_q=n_q, n_k=n_k, n_plain=n_plain,
                             ones_after_plain=ones_after_plain, q_scale=q_scale)
    return pl.pallas_call(
        kern,
        grid=(n // tm,),
        in_specs=[pl.BlockSpec((tm, d), lambda i: (i, 0)),
                  pl.BlockSpec((None, 1, mods.shape[-1]), lambda i: (geom.mod_row(i), 0, 0)),
                  pl.BlockSpec((1, d), lambda i: (0, 0)),
                  pl.BlockSpec((d, n_cols), lambda i: (0, 0)),
                  pl.BlockSpec((LANES, LANES), lambda i: (0, 0)),
                  pl.BlockSpec((2, LANES), lambda i: (0, 0)),
                  pl.BlockSpec((tm, LANES), lambda i: (geom.rope_row(i), 0)),
                  pl.BlockSpec((tm, LANES), lambda i: (geom.rope_row(i), 0))],
        out_specs=out_specs,
        out_shape=out_shape,
        compiler_params=_cparams("parallel"),
        name="proj_in",
    )(xs, mods, g.reshape(1, d), w, _head_mean_matrix(), gains, cos_t, sin_t)


DIFF_KV_CHUNK = 512
DIFF_Q_TILE = 256


def _diff_attn_kernel(*refs, lam_init, n_kv, chunks):
    lam_ref, q_ref = refs[0], refs[1]
    k_refs = refs[2:2 + n_kv]
    v_refs = refs[2 + n_kv:2 + 2 * n_kv]
    sg_ref, o_ref, s_ref, m_ref = refs[2 + 2 * n_kv:]
    dv = o_ref.shape[1]
    in_grid = pl.program_id(0) < pl.num_programs(0)

    @pl.when(in_grid)
    def _():
        q = q_ref[...]
        lane = lax.broadcasted_iota(jnp.int32, q.shape, 1)
        zero = jnp.zeros_like(q)
        q_maps = (jnp.where(lane < HEAD_DIM, q, zero), jnp.where(lane >= HEAD_DIM, q, zero))
        for m in range(2):
            mx = None
            for slab, start, size, col in chunks:
                s = lax.dot_general(q_maps[m], k_refs[slab][start:start + size, :], (((1,), (1,)), ((), ())),
                                    preferred_element_type=F32)
                s_ref[m, :, col:col + size] = s
                cm = jnp.max(s, axis=1, keepdims=True)
                mx = cm if mx is None else jnp.maximum(mx, cm)
            m_ref[m] = mx

    @pl.when(in_grid)
    def _():
        outs = []
        for m in range(2):
            row_max = m_ref[m]
            acc = None
            for slab, start, size, col in chunks:
                p = jnp.exp2(s_ref[m, :, col:col + size] - row_max)
                pv = jnp.dot(p.astype(BF16), v_refs[slab][start:start + size, :], preferred_element_type=F32)
                acc = pv if acc is None else acc + pv
            outs.append(acc[:, 0:dv] * (1.0 / acc[:, dv:dv + 1]))
        lam_v = lam_ref[...]
        lam = (jnp.exp(jnp.sum(lam_v[0:1] * lam_v[1:2], axis=1, keepdims=True))
               - jnp.exp(jnp.sum(lam_v[2:3] * lam_v[3:4], axis=1, keepdims=True)) + lam_init)
        a = outs[0] - lam * outs[1]
        a = a * lax.rsqrt(jnp.mean(a * a, axis=-1, keepdims=True) + EPS) * sg_ref[...] * (1.0 - lam_init)
        o_ref[...] = a.astype(o_ref.dtype)


def _diff_attention(q, k, v1, lam_vecs, sub_g, geom, lam_init):
    dq = q.shape[1]
    batch, seq, ctx_len = geom
    heads = dq // (2 * HEAD_DIM)
    dv = 2 * HEAD_DIM
    tq = DIFF_Q_TILE
    q_tiles = seq // tq
    ctx_blk = seq // ctx_len * batch
    lam_spec = pl.BlockSpec((4, HEAD_DIM), lambda *_: (0, 0))
    sg_spec = pl.BlockSpec((1, dv), lambda *_: (0, 0))
    chunks = [(0, c * DIFF_KV_CHUNK, DIFF_KV_CHUNK, c * DIFF_KV_CHUNK) for c in range(seq // DIFF_KV_CHUNK)]
    chunks.append((1, 0, ctx_len, seq))
    o_lat = pl.pallas_call(
        functools.partial(_diff_attn_kernel, lam_init=lam_init, n_kv=2, chunks=chunks),
        grid=(batch, heads, q_tiles),
        in_specs=[lam_spec,
                  pl.BlockSpec((tq, 2 * HEAD_DIM), lambda b, h, i: (b * q_tiles + i, h)),
                  pl.BlockSpec((seq, 2 * HEAD_DIM), lambda b, h, i: (b, h)),
                  pl.BlockSpec((ctx_len, 2 * HEAD_DIM), lambda b, h, i: (ctx_blk + b, h)),
                  pl.BlockSpec((seq, 2 * dv), lambda b, h, i: (b, h)),
                  pl.BlockSpec((ctx_len, 2 * dv), lambda b, h, i: (ctx_blk + b, h)),
                  sg_spec],
        out_specs=pl.BlockSpec((tq, dv), lambda b, h, i: (b * q_tiles + i, h)),
        out_shape=jax.ShapeDtypeStruct((batch * seq, heads * dv), BF16),
        scratch_shapes=[pltpu.VMEM((2, tq, seq + ctx_len), F32), pltpu.VMEM((2, tq, 1), F32)],
        compiler_params=_cparams("parallel", "parallel", "arbitrary"),
        name="diff_attention",
    )(lam_vecs, q, k, k, v1, v1, sub_g.reshape(1, dv))
    o_ctx = pl.pallas_call(
        functools.partial(_diff_attn_kernel, lam_init=lam_init, n_kv=1, chunks=[(0, 0, ctx_len, 0)]),
        grid=(batch, heads),
        in_specs=[lam_spec,
                  pl.BlockSpec((ctx_len, 2 * HEAD_DIM), lambda b, h: (ctx_blk + b, h)),
                  pl.BlockSpec((ctx_len, 2 * HEAD_DIM), lambda b, h: (ctx_blk + b, h)),
                  pl.BlockSpec((ctx_len, 2 * dv), lambda b, h: (ctx_blk + b, h)),
                  sg_spec],
        out_specs=pl.BlockSpec((ctx_len, dv), lambda b, h: (b, h)),
        out_shape=jax.ShapeDtypeStruct((batch * ctx_len, heads * dv), BF16),
        scratch_shapes=[pltpu.VMEM((2, ctx_len, ctx_len), F32), pltpu.VMEM((2, ctx_len, 1), F32)],
        compiler_params=_cparams("parallel", "parallel"),
        name="diff_attention_ctx",
    )(lam_vecs, q, k, v1, sub_g.reshape(1, dv))
    return o_lat, o_ctx


def _swa_kernel(sink_ref, q_ref, kl_ref, kc_ref, vl_ref, vc_ref, o_ref, *, latent_tiles, group):
    i = pl.program_id(1)
    tq = q_ref.shape[0]
    seq, ctx_len = kl_ref.shape[0], kc_ref.shape[0]
    span = tq + 2 * WINDOW
    start = pl.multiple_of(jnp.clip(i * tq - WINDOW, 0, seq - span), WINDOW)
    qpos = i * tq + lax.broadcasted_iota(jnp.int32, (tq, span), 0)
    kpos = start + lax.broadcasted_iota(jnp.int32, (tq, span), 1)
    valid = (jnp.abs(kpos - qpos) <= WINDOW) & (i < latent_tiles)
    bias = jnp.concatenate([jnp.where(valid, 0.0, NEG_INF).astype(F32), jnp.zeros((tq, ctx_len), F32)], axis=1)
    kw = kl_ref[pl.ds(start, span), :]
    vw = vl_ref[pl.ds(start, span), :]
    kc = kc_ref[...]
    vc = vc_ref[...]
    for h in range(SWA_KV_HEADS):
        hs = slice(h * LANES, (h + 1) * LANES)
        keys = jnp.concatenate([kw[:, hs], kc[:, hs]], axis=0)
        vals = jnp.concatenate([vw[:, hs], vc[:, hs]], axis=0)
        for g in range(group):
            hq = h * group + g
            q = q_ref[:, hq * LANES:(hq + 1) * LANES]
            s = lax.dot_general(q, keys, (((1,), (1,)), ((), ())), preferred_element_type=F32) + bias
            sink = sink_ref[hq]
            m = jnp.maximum(jnp.max(s, axis=1, keepdims=True), sink)
            p = jnp.exp(s - m)
            denom = jnp.sum(p, axis=1, keepdims=True) + jnp.exp(sink - m)
            o = jnp.dot(p.astype(BF16), vals, preferred_element_type=F32) * (1.0 / denom)
            o_ref[:, hq * LANES:(hq + 1) * LANES] = o.astype(o_ref.dtype)


def _swa_attention(q, k, v, sink, geom):
    n, dq = q.shape
    batch, seq, ctx_len = geom
    assert ctx_len == TOKEN_TILE
    lt = geom.lat_tiles
    q_heads = dq // LANES
    kvw = SWA_KV_HEADS * LANES
    ctx_blk = seq // ctx_len * batch
    kern = functools.partial(_swa_kernel, latent_tiles=lt, group=q_heads // SWA_KV_HEADS)

    def tile(b, i):
        return jnp.where(i < lt, b * lt + i, geom.ctx_tile(b))

    lat_kv = pl.BlockSpec((seq, kvw), lambda b, i: (b, 0))
    ctx_kv = pl.BlockSpec((ctx_len, kvw), lambda b, i: (ctx_blk + b, 0))
    return pl.pallas_call(
        kern,
        grid=(batch, lt + 1),
        in_specs=[pl.BlockSpec(memory_space=pltpu.SMEM),
                  pl.BlockSpec((TOKEN_TILE, dq), lambda b, i: (tile(b, i), 0)),
                  lat_kv, ctx_kv, lat_kv, ctx_kv],
        out_specs=pl.BlockSpec((TOKEN_TILE, dq), lambda b, i: (tile(b, i), 0)),
        out_shape=jax.ShapeDtypeStruct((n, dq), BF16),
        compiler_params=_cparams("parallel", "arbitrary"),
        name="swa_attention",
    )(sink, q, k, k, v, v)


def _seq_dft_kernel(ac_ref, as_ref, cc_ref, cs_ref, zc_ref, zs_ref, yc_ref, ys_ref, o_ref, *, latent_tiles):
    m = pl.program_id(2)

    def mix(a_cos, a_msin, z_cos, z_sin):
        acc = jnp.dot(a_cos[...], z_cos[...], preferred_element_type=F32)
        acc = acc + jnp.dot(a_msin[...], z_sin[...], preferred_element_type=F32)
        o_ref[...] = acc.astype(o_ref.dtype)

    @pl.when(m < latent_tiles)
    def _():
        mix(ac_ref, as_ref, zc_ref, zs_ref)

    @pl.when(m >= latent_tiles)
    def _():
        mix(cc_ref, cs_ref, yc_ref, ys_ref)


def _seq_dft(z, lat_mats, ctx_mats, geom, d):
    batch, seq, ctx_len = geom
    assert ctx_len == TOKEN_TILE and seq % ctx_len == 0
    lt = geom.lat_tiles
    tm = TOKEN_TILE
    tn = d // 2
    n_col = d // tn
    ctx_blk = seq // ctx_len * batch
    lat_a = pl.BlockSpec((tm, seq), lambda b, j, m: (jnp.minimum(m, lt - 1), 0))
    ctx_a = pl.BlockSpec((ctx_len, ctx_len), lambda b, j, m: (0, 0))
    return pl.pallas_call(
        functools.partial(_seq_dft_kernel, latent_tiles=lt),
        grid=(batch, n_col, lt + 1),
        in_specs=[lat_a, lat_a, ctx_a, ctx_a,
                  pl.BlockSpec((seq, tn), lambda b, j, m: (b, j)),
                  pl.BlockSpec((seq, tn), lambda b, j, m: (b, n_col + j)),
                  pl.BlockSpec((ctx_len, tn), lambda b, j, m: (ctx_blk + b, j)),
                  pl.BlockSpec((ctx_len, tn), lambda b, j, m: (ctx_blk + b, n_col + j))],
        out_specs=pl.BlockSpec((tm, tn), lambda b, j, m: (jnp.where(m < lt, b * lt + m, geom.ctx_tile(b)), j)),
        out_shape=jax.ShapeDtypeStruct((z.shape[0], d), BF16),
        compiler_params=_cparams("parallel", "parallel", "arbitrary"),
        name="seq_dft",
    )(*lat_mats, *ctx_mats, z, z, z, z)


def _dft_mats(n, scale):
    idx = (jnp.arange(n, dtype=jnp.int32)[:, None] * jnp.arange(n, dtype=jnp.int32)[None, :]) % n
    ang = idx.astype(F32) * (2.0 * math.pi / n)
    return (jnp.cos(ang) * scale).astype(BF16), (-jnp.sin(ang) * scale).astype(BF16)


def _proj_out_kernel(ol_ref, oc_ref, w_ref, x_ref, mod_ref, g_ref, xo_ref, h_ref, *, d, n_lat_tiles):
    def finish(o_ref):
        y = jnp.dot(o_ref[...], w_ref[...], preferred_element_type=F32)
        x = x_ref[...] + mod_ref[:, 2 * d:3 * d] * y
        xo_ref[...] = x
        ms = jnp.mean(x * x, axis=-1, keepdims=True)
        h = x * lax.rsqrt(ms + EPS) * g_ref[...]
        h_ref[...] = h * (1.0 + mod_ref[:, 4 * d:5 * d]) + mod_ref[:, 3 * d:4 * d]

    @pl.when(pl.program_id(0) < n_lat_tiles)
    def _():
        finish(ol_ref)

    @pl.when(pl.program_id(0) >= n_lat_tiles)
    def _():
        finish(oc_ref)


def _proj_out(o_lat, o_ctx, w, xs, mods, g2, geom):
    n, d = xs.shape
    ko = o_lat.shape[1]
    tm = TOKEN_TILE
    nl = geom.n_lat_tiles
    ctx_off = nl if o_ctx.shape[0] == n else 0
    return pl.pallas_call(
        functools.partial(_proj_out_kernel, d=d, n_lat_tiles=nl),
        grid=(n // tm,),
        in_specs=[pl.BlockSpec((tm, ko), lambda i: (jnp.minimum(i, nl - 1), 0)),
                  pl.BlockSpec((tm, ko), lambda i: (jnp.maximum(i, nl) - nl + ctx_off, 0)),
                  pl.BlockSpec((ko, d), lambda i: (0, 0)),
                  pl.BlockSpec((tm, d), lambda i: (i, 0)),
                  pl.BlockSpec((None, 1, mods.shape[-1]), lambda i: (geom.mod_row(i), 0, 0)),
                  pl.BlockSpec((1, d), lambda i: (0, 0))],
        out_specs=[pl.BlockSpec((tm, d), lambda i: (i, 0)), pl.BlockSpec((tm, d), lambda i: (i, 0))],
        out_shape=[jax.ShapeDtypeStruct((n, d), F32), jax.ShapeDtypeStruct((n, d), F32)],
        compiler_params=_cparams("parallel"),
        name="proj_out",
    )(o_lat, o_ctx, w, xs, mods, g2.reshape(1, d))


ROUTER_ROWS = SUBLANES + N_EXPERTS


def _router_kernel(h_ref, wt_ref, b_ref, tri_ref, ints_ref, flt_ref, cnt_ref, carry_ref):
    step = pl.program_id(0)
    tr = h_ref.shape[0]

    @pl.when(step == 0)
    def _():
        carry_ref[...] = jnp.zeros(carry_ref.shape, F32)

    logits = lax.dot_general(wt_ref[...], h_ref[...], (((1,), (1,)), ((), ())),
                             precision=HIGHEST, preferred_element_type=F32) + b_ref[:, 0:1]
    row8 = lax.broadcasted_iota(jnp.int32, (SUBLANES, tr), 0)
    lg = jnp.where(row8 < N_GROUPS, logits[0:SUBLANES], NEG_INF)
    lg_max = jnp.max(lg, axis=0, keepdims=True)
    pg = 1.0 / jnp.sum(jnp.exp(lg - lg_max), axis=0, keepdims=True)
    grp = jnp.min(jnp.where(lg == lg_max, row8, SUBLANES), axis=0, keepdims=True)
    l2 = jnp.zeros((EXPERTS_PER_GROUP, tr), F32)
    for g in range(N_GROUPS):
        lo = SUBLANES + g * EXPERTS_PER_GROUP
        l2 = l2 + jnp.where(grp == g, logits[lo:lo + EXPERTS_PER_GROUP], 0.0)
    l2_max = jnp.max(l2, axis=0, keepdims=True)
    j0 = jnp.min(jnp.where(l2 == l2_max, row8, SUBLANES), axis=0, keepdims=True)
    rest = jnp.where(row8 == j0, NEG_INF, l2)
    r_max = jnp.max(rest, axis=0, keepdims=True)
    j1 = jnp.min(jnp.where(rest == r_max, row8, SUBLANES), axis=0, keepdims=True)
    e1 = jnp.exp(r_max - l2_max)
    inv = 1.0 / (1.0 + e1)
    w0 = pg * inv
    w1 = pg * e1 * inv
    ex0 = grp * EXPERTS_PER_GROUP + j0
    ex1 = grp * EXPERTS_PER_GROUP + j1

    rows = lax.broadcasted_iota(jnp.int32, (N_EXPERTS, tr), 0)
    oh0 = (rows == ex0).astype(F32)
    oh1 = (rows == ex1).astype(F32)
    both = oh0 + oh1
    before = jnp.dot(both.astype(BF16), tri_ref[...], preferred_element_type=F32) + carry_ref[:, 0:1]
    rank0 = jnp.sum(oh0 * before, axis=0, keepdims=True)
    rank1 = jnp.sum(oh1 * before, axis=0, keepdims=True)
    carry_ref[...] = carry_ref[...] + jnp.sum(both, axis=1, keepdims=True)
    cnt_ref[...] = carry_ref[...]

    zi = jnp.zeros((SUBLANES - 4, tr), jnp.int32)
    ints_ref[...] = jnp.concatenate([ex0, ex1, rank0.astype(jnp.int32), rank1.astype(jnp.int32), zi], axis=0)
    flt_ref[...] = jnp.concatenate([w0, w1, jnp.zeros((SUBLANES - 2, tr), F32)], axis=0)


def _router(h2, wt, bias, tri):
    n, d = h2.shape
    tr = ROUTER_TILE
    return pl.pallas_call(
        _router_kernel,
        grid=(n // tr,),
        in_specs=[pl.BlockSpec((tr, d), lambda i: (i, 0)),
                  pl.BlockSpec((ROUTER_ROWS, d), lambda i: (0, 0)),
                  pl.BlockSpec((ROUTER_ROWS, LANES), lambda i: (0, 0)),
                  pl.BlockSpec((tr, tr), lambda i: (0, 0))],
        out_specs=[pl.BlockSpec((SUBLANES, tr), lambda i: (0, i)),
                   pl.BlockSpec((SUBLANES, tr), lambda i: (0, i)),
                   pl.BlockSpec((N_EXPERTS, LANES), lambda i: (0, 0))],
        out_shape=[jax.ShapeDtypeStruct((SUBLANES, n), jnp.int32),
                   jax.ShapeDtypeStruct((SUBLANES, n), F32),
                   jax.ShapeDtypeStruct((N_EXPERTS, LANES), F32)],
        scratch_shapes=[pltpu.VMEM((N_EXPERTS, LANES), F32)],
        compiler_params=_cparams("arbitrary"),
        name="router",
    )(h2, wt, bias, tri)


def _dest_kernel(pstart_ref, ints_ref, o_ref):
    ints = ints_ref[...]
    ex = ints[0:2]
    base = jnp.zeros(ex.shape, jnp.int32)
    for e in range(N_EXPERTS):
        base = jnp.where(ex == e, pstart_ref[e], base)
    o_ref[...] = jnp.concatenate([base + ints[2:4], jnp.zeros((SUBLANES - 2, ints.shape[1]), jnp.int32)], axis=0)


def _dest_rows(pstart, ints):
    n = ints.shape[1]
    tn = 2048 if n % 2048 == 0 else ROUTER_TILE
    return pl.pallas_call(
        _dest_kernel,
        grid=(n // tn,),
        in_specs=[pl.BlockSpec(memory_space=pltpu.SMEM),
                  pl.BlockSpec((SUBLANES, tn), lambda i: (0, i))],
        out_specs=pl.BlockSpec((SUBLANES, tn), lambda i: (0, i)),
        out_shape=jax.ShapeDtypeStruct((SUBLANES, n), jnp.int32),
        compiler_params=_cparams("parallel"),
        name="dest_rows",
    )(pstart, ints)


DISPATCH_TILE = 512


def _dispatch_kernel(meta_ref, d0_ref, d1_ref, h_ref, xs_ref, zero_ref, sem_ref, zsem_ref, *, n_blocks):
    step = pl.program_id(0)
    td = h_ref.shape[0]

    @pl.when(step == 0)
    def _():
        zero_ref[...] = jnp.zeros(zero_ref.shape, zero_ref.dtype)

        def zero_expert(e, carry):
            def zero_row(r, c):
                pltpu.make_async_copy(zero_ref.at[pl.ds(0, 1)], xs_ref.at[pl.ds(meta_ref[e] + r, 1)], zsem_ref).start()
                return c
            lax.fori_loop(0, meta_ref[N_EXPERTS + e], zero_row, 0)

            def wait_row(r, c):
                pltpu.make_async_copy(zero_ref.at[pl.ds(0, 1)], xs_ref.at[pl.ds(0, 1)], zsem_ref).wait()
                return c
            lax.fori_loop(0, meta_ref[N_EXPERTS + e], wait_row, 0)
            return carry
        lax.fori_loop(0, N_EXPERTS, zero_expert, 0)

        def zero_block(b, carry):
            cp = pltpu.make_async_copy(zero_ref, xs_ref.at[pl.ds(b * MOE_TILE, MOE_TILE)], zsem_ref)
            cp.start()
            cp.wait()
            return carry
        lax.fori_loop(meta_ref[2 * N_EXPERTS], n_blocks, zero_block, 0)

    def issue(r, carry):
        src = h_ref.at[pl.ds(r, 1)]
        pltpu.make_async_copy(src, xs_ref.at[pl.ds(d0_ref[r], 1)], sem_ref.at[0]).start()
        pltpu.make_async_copy(src, xs_ref.at[pl.ds(d1_ref[r], 1)], sem_ref.at[1]).start()
        return carry
    lax.fori_loop(0, td, issue, 0, unroll=8)
    for k in range(2):
        pltpu.make_async_copy(h_ref, xs_ref.at[pl.ds(0, td)], sem_ref.at[k]).wait()


def _dispatch(meta, dest0, dest1, h2, n_rows):
    n, d = h2.shape
    td = DISPATCH_TILE
    n_blocks = n_rows // MOE_TILE
    smem = functools.partial(pl.BlockSpec, memory_space=pltpu.SMEM)
    return pl.pallas_call(
        functools.partial(_dispatch_kernel, n_blocks=n_blocks),
        grid=(n // td,),
        in_specs=[smem(),
                  smem((td,), lambda i: (i,)),
                  smem((td,), lambda i: (i,)),
                  pl.BlockSpec((td, d), lambda i: (i, 0))],
        out_specs=pl.BlockSpec(memory_space=pl.ANY),
        out_shape=jax.ShapeDtypeStruct((n_rows, d), F32),
        scratch_shapes=[pltpu.VMEM((MOE_TILE, d), F32), pltpu.SemaphoreType.DMA((2,)),
                        pltpu.SemaphoreType.DMA(())],
        compiler_params=_cparams("arbitrary"),
        name="moe_dispatch",
    )(meta, dest0, dest1, h2)


def _expert_kernel(blk_exp_ref, n_used_ref, x_ref, w13_ref, w2_ref, y_ref, *, d_expert):
    i = pl.program_id(0)

    @pl.when(i < n_used_ref[0])
    def _():
        gu = jnp.dot(x_ref[...].astype(BF16), w13_ref[...], preferred_element_type=F32)
        g = gu[:, 0:d_expert]
        u = gu[:, d_expert:2 * d_expert]
        a = g * (1.0 / (1.0 + jnp.exp(-g))) * u
        y_ref[...] = jnp.dot(a.astype(BF16), w2_ref[...], preferred_element_type=F32)

    @pl.when(i >= n_used_ref[0])
    def _():
        y_ref[...] = jnp.zeros(y_ref.shape, F32)


def _experts(blk_exp, n_used, x_sorted, w13, w2):
    n_rows, d = x_sorted.shape
    d_expert = w2.shape[1]
    n_blocks = n_rows // MOE_TILE
    grid_spec = pltpu.PrefetchScalarGridSpec(
        num_scalar_prefetch=2,
        grid=(n_blocks,),
        in_specs=[pl.BlockSpec((MOE_TILE, d), lambda i, be, nu: (i, 0)),
                  pl.BlockSpec((None, d, 2 * d_expert), lambda i, be, nu: (be[i], 0, 0)),
                  pl.BlockSpec((None, d_expert, d), lambda i, be, nu: (be[i], 0, 0))],
        out_specs=pl.BlockSpec((MOE_TILE, d), lambda i, be, nu: (i, 0)),
    )
    return pl.pallas_call(
        functools.partial(_expert_kernel, d_expert=d_expert),
        grid_spec=grid_spec,
        out_shape=jax.ShapeDtypeStruct((n_rows, d), F32),
        compiler_params=_cparams("arbitrary"),
        name="moe_experts",
    )(blk_exp, n_used, x_sorted, w13, w2)


def _combine_kernel(d0_ref, d1_ref, w_ref, x_ref, mod_ref, y_ref, o_ref, b0_ref, b1_ref, sem_ref, *, d):
    tc = x_ref.shape[0]

    def issue(r, carry):
        pltpu.make_async_copy(y_ref.at[pl.ds(d0_ref[r], 1)], b0_ref.at[pl.ds(r, 1)], sem_ref.at[0]).start()
        pltpu.make_async_copy(y_ref.at[pl.ds(d1_ref[r], 1)], b1_ref.at[pl.ds(r, 1)], sem_ref.at[1]).start()
        return carry
    lax.fori_loop(0, tc, issue, 0, unroll=8)
    pltpu.make_async_copy(y_ref.at[pl.ds(0, tc)], b0_ref, sem_ref.at[0]).wait()
    pltpu.make_async_copy(y_ref.at[pl.ds(0, tc)], b1_ref, sem_ref.at[1]).wait()
    out = w_ref[:, 0:1] * b0_ref[...] + w_ref[:, 1:2] * b1_ref[...]
    o_ref[...] = x_ref[...] + mod_ref[:, 5 * d:6 * d] * out


def _combine(dest0, dest1, w_cols, xs, mods, y_sorted, geom):
    n, d = xs.shape
    tc = TOKEN_TILE
    smem = functools.partial(pl.BlockSpec, memory_space=pltpu.SMEM)
    return pl.pallas_call(
        functools.partial(_combine_kernel, d=d),
        grid=(n // tc,),
        in_specs=[smem((tc,), lambda i: (i,)),
                  smem((tc,), lambda i: (i,)),
                  pl.BlockSpec((tc, 2), lambda i: (i, 0)),
                  pl.BlockSpec((tc, d), lambda i: (i, 0)),
                  pl.BlockSpec((None, 1, mods.shape[-1]), lambda i: (geom.mod_row(i), 0, 0)),
                  pl.BlockSpec(memory_space=pl.ANY)],
        out_specs=pl.BlockSpec((tc, d), lambda i: (i, 0)),
        out_shape=jax.ShapeDtypeStruct((n, d), F32),
        scratch_shapes=[pltpu.VMEM((tc, d), F32), pltpu.VMEM((tc, d), F32), pltpu.SemaphoreType.DMA((2,))],
        compiler_params=_cparams("arbitrary"),
        name="moe_combine",
    )(dest0, dest1, w_cols, xs, mods, y_sorted)


def _hier_moe(h2, xs, mods, geom, w_r1, b_r1, w_r2, b_r2, w13, w2):
    n, d = h2.shape
    wt = jnp.zeros((ROUTER_ROWS, d), F32)
    wt = wt.at[0:N_GROUPS].set(w_r1.T)
    wt = wt.at[SUBLANES:].set(jnp.transpose(w_r2, (0, 2, 1)).reshape(N_EXPERTS, d))
    bias = jnp.zeros((ROUTER_ROWS,), F32).at[0:N_GROUPS].set(b_r1).at[SUBLANES:].set(b_r2.reshape(-1))
    bias = jnp.broadcast_to(bias[:, None], (ROUTER_ROWS, LANES))
    tri = jnp.asarray(np.triu(np.ones((ROUTER_TILE, ROUTER_TILE), np.float32), 1), BF16)
    ints, flt, cnt = _router(h2, wt, bias, tri)

    counts = cnt[:, 0].astype(jnp.int32)
    padded = ((counts + MOE_TILE - 1) // MOE_TILE) * MOE_TILE
    pend = jnp.cumsum(padded)
    pstart = pend - padded
    n_blocks = (2 * n) // MOE_TILE + N_EXPERTS
    n_rows = n_blocks * MOE_TILE
    n_used = pend[-1] // MOE_TILE
    blk_row0 = jnp.arange(n_blocks, dtype=jnp.int32) * MOE_TILE
    blk_exp = jnp.minimum(jnp.sum(pend[None, :] <= blk_row0[:, None], axis=1), N_EXPERTS - 1).astype(jnp.int32)
    meta = jnp.concatenate([pstart + counts, padded - counts, n_used[None]]).astype(jnp.int32)

    dest = _dest_rows(pstart.astype(jnp.int32), ints)
    dest0, dest1 = dest[0], dest[1]
    x_sorted = _dispatch(meta, dest0, dest1, h2, n_rows)
    y_sorted = _experts(blk_exp, n_used[None].astype(jnp.int32), x_sorted, w13.astype(BF16), w2.astype(BF16))
    w_cols = flt[0:2].T
    return _combine(dest0, dest1, w_cols, xs, mods, y_sorted, geom)


def _rope_tables(seq, ctx_len):
    rows = seq // GRID_W
    row = jnp.repeat(jnp.arange(rows, dtype=F32), GRID_W)
    col = jnp.tile(jnp.arange(GRID_W, dtype=F32), rows)
    inv = ROPE_THETA ** (-jnp.arange(ROT_FREQS, dtype=F32) / ROT_FREQS)
    ang_r, ang_c = row[:, None] * inv, col[:, None] * inv
    cos_h = jnp.concatenate([jnp.cos(ang_r)] * 2 + [jnp.cos(ang_c)] * 2, axis=1)
    sin_h = jnp.concatenate([-jnp.sin(ang_r), jnp.sin(ang_r), -jnp.sin(ang_c), jnp.sin(ang_c)], axis=1)
    cos_t = jnp.concatenate([jnp.tile(cos_h, (1, 2)), jnp.ones((ctx_len, LANES), F32)], axis=0)
    sin_t = jnp.concatenate([jnp.tile(sin_h, (1, 2)), jnp.zeros((ctx_len, LANES), F32)], axis=0)
    return cos_t, sin_t


def _pad_heads_cols(w, n_heads):
    d = w.shape[0]
    w3 = w.reshape(d, n_heads, HEAD_DIM)
    return jnp.concatenate([w3, jnp.zeros_like(w3)], axis=2).reshape(d, n_heads * LANES)


def kernel(x, c, ctx, c_ctx, ada_w, ada_b, norm1_g, norm2_g, a_wqkv, a_wo, a_q_norm, a_k_norm, a_lambda_q1, a_lambda_k1, a_lambda_q2, a_lambda_k2, a_subln_g, b_wqkv, b_wo, b_q_norm, b_k_norm, b_sink, f_wo, r_w1, r_b1, r_w2, r_b2, e_w13, e_w2):
    batch, seq, d = x.shape
    ctx_len = ctx.shape[1]
    depth = ada_w.shape[0]
    n = batch * (seq + ctx_len)
    assert seq % TOKEN_TILE == 0 and ctx_len % TOKEN_TILE == 0 and n % ROUTER_TILE == 0
    geom = Geom(batch, seq, ctx_len)

    xs = jnp.concatenate([x.reshape(batch * seq, d), ctx.reshape(batch * ctx_len, d)], axis=0)
    mod_rows = ((batch + 1 + SUBLANES - 1) // SUBLANES) * SUBLANES
    cvec = jnp.zeros((mod_rows, d), F32).at[0:batch].set(c).at[batch].set(c_ctx)
    mods_all = _ada_table(cvec, ada_w, ada_b).reshape(depth, mod_rows, 1, 6 * d)
    cos_t, sin_t = _rope_tables(seq, ctx_len)

    for i in range(depth):
        kind, j = i % N_MIXERS, i // N_MIXERS
        mods = mods_all[i]
        if kind == 0:
            lam_init = 0.8 - 0.6 * math.exp(-0.3 * i)
            nh = d // LANES
            gains = jnp.stack([jnp.tile(a_q_norm[j], 2), jnp.tile(a_k_norm[j], 2)])
            q, k, v1 = _proj_in(xs, mods, norm1_g[i], a_wqkv[j].astype(BF16), gains, cos_t, sin_t, geom,
                                n_q=nh, n_k=nh, n_plain=nh, ones_after_plain=True,
                                q_scale=HEAD_DIM ** -0.5 * math.log2(math.e))
            lam_vecs = jnp.stack([a_lambda_q1[j], a_lambda_k1[j], a_lambda_q2[j], a_lambda_k2[j]])
            o_lat, o_ctx = _diff_attention(q, k, v1, lam_vecs, a_subln_g[j], geom, lam_init)
            wo = a_wo[j].astype(BF16)
        elif kind == 1:
            q_heads = d // HEAD_DIM
            nq, nk = q_heads * HEAD_DIM, SWA_KV_HEADS * HEAD_DIM
            w = b_wqkv[j]
            w_pad = jnp.concatenate([_pad_heads_cols(w[:, :nq], q_heads),
                                     _pad_heads_cols(w[:, nq:nq + nk], SWA_KV_HEADS),
                                     _pad_heads_cols(w[:, nq + nk:], SWA_KV_HEADS)], axis=1).astype(BF16)
            zeros64 = jnp.zeros((HEAD_DIM,), F32)
            gains = jnp.stack([jnp.concatenate([b_q_norm[j], zeros64]), jnp.concatenate([b_k_norm[j], zeros64])])
            q, k, v = _proj_in(xs, mods, norm1_g[i], w_pad, gains, cos_t, sin_t, geom,
                               n_q=q_heads, n_k=SWA_KV_HEADS, n_plain=SWA_KV_HEADS)
            o_lat = o_ctx = _swa_attention(q, k, v, b_sink[j], geom)
            wo3 = b_wo[j].reshape(q_heads, HEAD_DIM, d)
            wo = jnp.concatenate([wo3, jnp.zeros_like(wo3)], axis=1).reshape(q_heads * LANES, d).astype(BF16)
        else:
            gd = d // FOURIER_GROUPS
            cd, msd = _dft_mats(gd, gd ** -0.5)
            eye = jnp.eye(FOURIER_GROUPS, dtype=BF16)
            w_cs = jnp.concatenate([jnp.kron(eye, cd), jnp.kron(eye, -msd)], axis=1)
            (z,) = _proj_in(xs, mods, norm1_g[i], w_cs, jnp.zeros((2, LANES), F32), cos_t, sin_t, geom,
                            n_q=0, n_k=0, n_plain=2 * d // LANES)
            o_lat = o_ctx = _seq_dft(z, _dft_mats(seq, seq ** -0.5), _dft_mats(ctx_len, ctx_len ** -0.5), geom, d)
            wo = f_wo[j].astype(BF16)
        xs, h2 = _proj_out(o_lat, o_ctx, wo, xs, mods, norm2_g[i], geom)
        xs = _hier_moe(h2, xs, mods, geom, r_w1[i], r_b1[i], r_w2[i], r_b2[i], e_w13[i], e_w2[i])

    return xs[:batch * seq].reshape(batch, seq, d)
```

```python
import functools
import math
from typing import NamedTuple

import numpy as np
import jax
import jax.numpy as jnp
from jax import lax
from jax.experimental import pallas as pl
from jax.experimental.pallas import tpu as pltpu

F32 = jnp.float32
BF16 = jnp.bfloat16
HIGHEST = lax.Precision.HIGHEST

GRID_W = 64
HEAD_DIM = 64
ROT_FREQS = HEAD_DIM // 4
ROPE_THETA = 10000.0
WINDOW = 128
N_MIXERS = 3
SWA_KV_HEADS = 4
FOURIER_GROUPS = 4
N_GROUPS = 4
EXPERTS_PER_GROUP = 8
N_EXPERTS = N_GROUPS * EXPERTS_PER_GROUP
EPS = 1e-6
NEG_INF = -1e30

LANES = 128
SUBLANES = 8
MXU_DIM = 256
TOKEN_TILE = 256
MOE_HALF = 256
MOE_TILE = 2 * MOE_HALF
ROUTER_TILE = 512
VMEM_LIMIT = 48 * 1024 * 1024


def _cparams(*sem):
    return pltpu.CompilerParams(dimension_semantics=sem, vmem_limit_bytes=VMEM_LIMIT)


class Geom(NamedTuple):
    batch: int
    seq: int
    ctx_len: int

    @property
    def lat_tiles(self):
        return self.seq // TOKEN_TILE

    @property
    def n_lat_tiles(self):
        return self.batch * self.lat_tiles

    @property
    def n_tiles(self):
        return self.n_lat_tiles + self.batch * (self.ctx_len // TOKEN_TILE)

    def mod_row(self, i):
        return jnp.where(i < self.n_lat_tiles, i // self.lat_tiles, self.batch)

    def rope_row(self, i):
        return jnp.where(i < self.n_lat_tiles, i % self.lat_tiles, self.lat_tiles)

    def ctx_tile(self, b):
        return self.n_lat_tiles + b


def _ada_kernel(c_ref, w_ref, b_ref, o_ref):
    c = c_ref[...]
    s = c * (1.0 / (1.0 + jnp.exp(-c)))
    o_ref[...] = jnp.dot(s, w_ref[...], precision=HIGHEST, preferred_element_type=F32) + b_ref[...]


def _ada_table(cvec, ada_w, ada_b):
    depth, d, n6 = ada_w.shape
    rows = cvec.shape[0]
    tn = n6 // 4
    return pl.pallas_call(
        _ada_kernel,
        grid=(depth, n6 // tn),
        in_specs=[pl.BlockSpec((rows, d), lambda l, j: (0, 0)),
                  pl.BlockSpec((None, d, tn), lambda l, j: (l, 0, j)),
                  pl.BlockSpec((None, 1, tn), lambda l, j: (l, 0, j))],
        out_specs=pl.BlockSpec((None, rows, tn), lambda l, j: (l, 0, j)),
        out_shape=jax.ShapeDtypeStruct((depth, rows, n6), F32),
        compiler_params=_cparams("parallel", "parallel"),
        name="ada_table",
    )(cvec, ada_w, ada_b.reshape(depth, 1, n6))


def _head_mean_matrix():
    r = np.arange(LANES)
    return jnp.asarray((r[:, None] // HEAD_DIM == r[None, :] // HEAD_DIM).astype(np.float32) / HEAD_DIM, BF16)


def _proj_in_kernel(x_ref, mod_ref, g_ref, w_ref, hm_ref, gain_ref, cos_ref, sin_ref, *out_refs,
                    d, n_q, n_k, n_plain, ones_after_plain, q_scale):
    q_ref = out_refs[0] if n_q else None
    k_ref = out_refs[1] if n_k else None
    p_ref = out_refs[-1]
    x = x_ref[...]
    ms = jnp.mean(x * x, axis=-1, keepdims=True)
    h = x * lax.rsqrt(ms + EPS) * g_ref[...]
    h = h * (1.0 + mod_ref[:, d:2 * d]) + mod_ref[:, 0:d]
    hb = h.astype(BF16)
    lane = lax.broadcasted_iota(jnp.int32, (1, LANES), 1)
    first_half = (lane // ROT_FREQS) % 2 == 0
    n_chunks = n_q + n_k + n_plain
    for c2 in range(0, n_chunks, 2):
        width = min(2, n_chunks - c2) * LANES
        y2 = jnp.dot(hb, w_ref[:, c2 * LANES:c2 * LANES + width], preferred_element_type=F32)
        for half in range(width // LANES):
            c = c2 + half
            y = y2[:, half * LANES:(half + 1) * LANES]
            if c < n_q + n_k:
                msq = jnp.dot((y * y).astype(BF16), hm_ref[...], preferred_element_type=F32)
                is_q = c < n_q
                gain = gain_ref[0:1, :] if is_q else gain_ref[1:2, :]
                yn = y * lax.rsqrt(msq + EPS) * gain
                partner = jnp.where(first_half, pltpu.roll(yn, LANES - ROT_FREQS, 1), pltpu.roll(yn, ROT_FREQS, 1))
                out = yn * cos_ref[...] + partner * sin_ref[...]
                if is_q:
                    q_ref[:, c * LANES:(c + 1) * LANES] = (out * q_scale).astype(q_ref.dtype)
                else:
                    ck = c - n_q
                    k_ref[:, ck * LANES:(ck + 1) * LANES] = out.astype(k_ref.dtype)
            else:
                cp = c - n_q - n_k
                if ones_after_plain:
                    p_ref[:, 2 * cp * LANES:(2 * cp + 1) * LANES] = y.astype(p_ref.dtype)
                    p_ref[:, (2 * cp + 1) * LANES:(2 * cp + 2) * LANES] = jnp.ones(y.shape, p_ref.dtype)
                else:
                    p_ref[:, cp * LANES:(cp + 1) * LANES] = y.astype(p_ref.dtype)


def _proj_in(xs, mods, g, w, gains, cos_t, sin_t, geom, *, n_q, n_k, n_plain, ones_after_plain=False,
             q_scale=HEAD_DIM ** -0.5):
    n, d = xs.shape
    tm = TOKEN_TILE
    n_cols = (n_q + n_k + n_plain) * LANES
    assert w.shape == (d, n_cols)
    out_shape, out_specs = [], []
    for cnt in (n_q, n_k):
        if cnt:
            out_shape.append(jax.ShapeDtypeStruct((n, cnt * LANES), BF16))
            out_specs.append(pl.BlockSpec((tm, cnt * LANES), lambda i: (i, 0)))
    pw = n_plain * LANES * (2 if ones_after_plain else 1)
    out_shape.append(jax.ShapeDtypeStruct((n, pw), BF16))
    out_specs.append(pl.BlockSpec((tm, pw), lambda i: (i, 0)))
    kern = functools.partial(_proj_in_kernel, d=d, n_q=n_q, n_k=n_k, n_plain=n_plain,
                             ones_after_plain=ones_after_plain, q_scale=q_scale)
    return pl.pallas_call(
        kern,
        grid=(n // tm,),
        in_specs=[pl.BlockSpec((tm, d), lambda i: (i, 0)),
                  pl.BlockSpec((None, 1, mods.shape[-1]), lambda i: (geom.mod_row(i), 0, 0)),
                  pl.BlockSpec((1, d), lambda i: (0, 0)),
                  pl.BlockSpec((d, n_cols), lambda i: (0, 0)),
                  pl.BlockSpec((LANES, LANES), lambda i: (0, 0)),
                  pl.BlockSpec((2, LANES), lambda i: (0, 0)),
                  pl.BlockSpec((tm, LANES), lambda i: (geom.rope_row(i), 0)),
                  pl.BlockSpec((tm, LANES), lambda i: (geom.rope_row(i), 0))],
        out_specs=out_specs,
        out_shape=out_shape,
        compiler_params=_cparams("parallel"),
        name="proj_in",
    )(xs, mods, g.reshape(1, d), w, _head_mean_matrix(), gains, cos_t, sin_t)


DIFF_KV_CHUNK = 512
DIFF_Q_TILE = 256


def _diff_attn_kernel(*refs, lam_init, n_kv, chunks):
    lam_ref, q_ref = refs[0], refs[1]
    k_refs = refs[2:2 + n_kv]
    v_refs = refs[2 + n_kv:2 + 2 * n_kv]
    sg_ref, o_ref, s_ref, m_ref = refs[2 + 2 * n_kv:]
    dv = o_ref.shape[1]
    in_grid = pl.program_id(0) < pl.num_programs(0)

    @pl.when(in_grid)
    def _():
        q = q_ref[...]
        lane = lax.broadcasted_iota(jnp.int32, q.shape, 1)
        zero = jnp.zeros_like(q)
        q_maps = (jnp.where(lane < HEAD_DIM, q, zero), jnp.where(lane >= HEAD_DIM, q, zero))
        for m in range(2):
            mx = None
            for slab, start, size, col in chunks:
                s = lax.dot_general(q_maps[m], k_refs[slab][start:start + size, :], (((1,), (1,)), ((), ())),
                                    preferred_element_type=F32)
                s_ref[m, :, col:col + size] = s
                cm = jnp.max(s, axis=1, keepdims=True)
                mx = cm if mx is None else jnp.maximum(mx, cm)
            m_ref[m] = mx

    @pl.when(in_grid)
    def _():
        outs = []
        for m in range(2):
            row_max = m_ref[m]
            acc = None
            for slab, start, size, col in chunks:
                p = jnp.exp2(s_ref[m, :, col:col + size] - row_max)
                pv = jnp.dot(p.astype(BF16), v_refs[slab][start:start + size, :], preferred_element_type=F32)
                acc = pv if acc is None else acc + pv
            outs.append(acc[:, 0:dv] * (1.0 / acc[:, dv:dv + 1]))
        lam_v = lam_ref[...]
        lam = (jnp.exp(jnp.sum(lam_v[0:1] * lam_v[1:2], axis=1, keepdims=True))
               - jnp.exp(jnp.sum(lam_v[2:3] * lam_v[3:4], axis=1, keepdims=True)) + lam_init)
        a = outs[0] - lam * outs[1]
        a = a * lax.rsqrt(jnp.mean(a * a, axis=-1, keepdims=True) + EPS) * sg_ref[...] * (1.0 - lam_init)
        o_ref[...] = a.astype(o_ref.dtype)


def _diff_attention(q, k, v1, lam_vecs, sub_g, geom, lam_init):
    dq = q.shape[1]
    batch, seq, ctx_len = geom
    heads = dq // (2 * HEAD_DIM)
    dv = 2 * HEAD_DIM
    tq = DIFF_Q_TILE
    q_tiles = seq // tq
    ctx_blk = seq // ctx_len * batch
    lam_spec = pl.BlockSpec((4, HEAD_DIM), lambda *_: (0, 0))
    sg_spec = pl.BlockSpec((1, dv), lambda *_: (0, 0))
    chunks = [(0, c * DIFF_KV_CHUNK, DIFF_KV_CHUNK, c * DIFF_KV_CHUNK) for c in range(seq // DIFF_KV_CHUNK)]
    chunks.append((1, 0, ctx_len, seq))
    o_lat = pl.pallas_call(
        functools.partial(_diff_attn_kernel, lam_init=lam_init, n_kv=2, chunks=chunks),
        grid=(batch, heads, q_tiles),
        in_specs=[lam_spec,
                  pl.BlockSpec((tq, 2 * HEAD_DIM), lambda b, h, i: (b * q_tiles + i, h)),
                  pl.BlockSpec((seq, 2 * HEAD_DIM), lambda b, h, i: (b, h)),
                  pl.BlockSpec((ctx_len, 2 * HEAD_DIM), lambda b, h, i: (ctx_blk + b, h)),
                  pl.BlockSpec((seq, 2 * dv), lambda b, h, i: (b, h)),
                  pl.BlockSpec((ctx_len, 2 * dv), lambda b, h, i: (ctx_blk + b, h)),
                  sg_spec],
        out_specs=pl.BlockSpec((tq, dv), lambda b, h, i: (b * q_tiles + i, h)),
        out_shape=jax.ShapeDtypeStruct((batch * seq, heads * dv), BF16),
        scratch_shapes=[pltpu.VMEM((2, tq, seq + ctx_len), F32), pltpu.VMEM((2, tq, 1), F32)],
        compiler_params=_cparams("parallel", "parallel", "arbitrary"),
        name="diff_attention",
    )(lam_vecs, q, k, k, v1, v1, sub_g.reshape(1, dv))
    o_ctx = pl.pallas_call(
        functools.partial(_diff_attn_kernel, lam_init=lam_init, n_kv=1, chunks=[(0, 0, ctx_len, 0)]),
        grid=(batch, heads),
        in_specs=[lam_spec,
                  pl.BlockSpec((ctx_len, 2 * HEAD_DIM), lambda b, h: (ctx_blk + b, h)),
                  pl.BlockSpec((ctx_len, 2 * HEAD_DIM), lambda b, h: (ctx_blk + b, h)),
                  pl.BlockSpec((ctx_len, 2 * dv), lambda b, h: (ctx_blk + b, h)),
                  sg_spec],
        out_specs=pl.BlockSpec((ctx_len, dv), lambda b, h: (b, h)),
        out_shape=jax.ShapeDtypeStruct((batch * ctx_len, heads * dv), BF16),
        scratch_shapes=[pltpu.VMEM((2, ctx_len, ctx_len), F32), pltpu.VMEM((2, ctx_len, 1), F32)],
        compiler_params=_cparams("parallel", "parallel"),
        name="diff_attention_ctx",
    )(lam_vecs, q, k, v1, sub_g.reshape(1, dv))
    return o_lat, o_ctx


def _swa_kernel(sink_ref, q_ref, kl_ref, kc_ref, vl_ref, vc_ref, o_ref, *, latent_tiles, group):
    i = pl.program_id(1)
    tq = q_ref.shape[0]
    seq, ctx_len = kl_ref.shape[0], kc_ref.shape[0]
    span = tq + 2 * WINDOW
    start = pl.multiple_of(jnp.clip(i * tq - WINDOW, 0, seq - span), WINDOW)
    qpos = i * tq + lax.broadcasted_iota(jnp.int32, (tq, span), 0)
    kpos = start + lax.broadcasted_iota(jnp.int32, (tq, span), 1)
    valid = (jnp.abs(kpos - qpos) <= WINDOW) & (i < latent_tiles)
    bias = jnp.concatenate([jnp.where(valid, 0.0, NEG_INF).astype(F32), jnp.zeros((tq, ctx_len), F32)], axis=1)
    kw = kl_ref[pl.ds(start, span), :]
    vw = vl_ref[pl.ds(start, span), :]
    kc = kc_ref[...]
    vc = vc_ref[...]
    for h in range(SWA_KV_HEADS):
        hs = slice(h * LANES, (h + 1) * LANES)
        keys = jnp.concatenate([kw[:, hs], kc[:, hs]], axis=0)
        vals = jnp.concatenate([vw[:, hs], vc[:, hs]], axis=0)
        for g in range(group):
            hq = h * group + g
            q = q_ref[:, hq * LANES:(hq + 1) * LANES]
            s = lax.dot_general(q, keys, (((1,), (1,)), ((), ())), preferred_element_type=F32) + bias
            sink = sink_ref[hq]
            m = jnp.maximum(jnp.max(s, axis=1, keepdims=True), sink)
            p = jnp.exp(s - m)
            denom = jnp.sum(p, axis=1, keepdims=True) + jnp.exp(sink - m)
            o = jnp.dot(p.astype(BF16), vals, preferred_element_type=F32) * (1.0 / denom)
            o_ref[:, hq * LANES:(hq + 1) * LANES] = o.astype(o_ref.dtype)


def _swa_attention(q, k, v, sink, geom):
    n, dq = q.shape
    batch, seq, ctx_len = geom
    assert ctx_len == TOKEN_TILE
    lt = geom.lat_tiles
    q_heads = dq // LANES
    kvw = SWA_KV_HEADS * LANES
    ctx_blk = seq // ctx_len * batch
    kern = functools.partial(_swa_kernel, latent_tiles=lt, group=q_heads // SWA_KV_HEADS)

    def tile(b, i):
        return jnp.where(i < lt, b * lt + i, geom.ctx_tile(b))

    lat_kv = pl.BlockSpec((seq, kvw), lambda b, i: (b, 0))
    ctx_kv = pl.BlockSpec((ctx_len, kvw), lambda b, i: (ctx_blk + b, 0))
    return pl.pallas_call(
        kern,
        grid=(batch, lt + 1),
        in_specs=[pl.BlockSpec(memory_space=pltpu.SMEM),
                  pl.BlockSpec((TOKEN_TILE, dq), lambda b, i: (tile(b, i), 0)),
                  lat_kv, ctx_kv, lat_kv, ctx_kv],
        out_specs=pl.BlockSpec((TOKEN_TILE, dq), lambda b, i: (tile(b, i), 0)),
        out_shape=jax.ShapeDtypeStruct((n, dq), BF16),
        compiler_params=_cparams("parallel", "arbitrary"),
        name="swa_attention",
    )(sink, q, k, k, v, v)


def _seq_dft_kernel(ac_ref, as_ref, cc_ref, cs_ref, zc_ref, zs_ref, yc_ref, ys_ref, o_ref, *, latent_tiles):
    m = pl.program_id(2)

    def mix(a_cos, a_msin, z_cos, z_sin):
        acc = jnp.dot(a_cos[...], z_cos[...], preferred_element_type=F32)
        acc = acc + jnp.dot(a_msin[...], z_sin[...], preferred_element_type=F32)
        o_ref[...] = acc.astype(o_ref.dtype)

    @pl.when(m < latent_tiles)
    def _():
        mix(ac_ref, as_ref, zc_ref, zs_ref)

    @pl.when(m >= latent_tiles)
    def _():
        mix(cc_ref, cs_ref, yc_ref, ys_ref)


def _seq_dft(z, lat_mats, ctx_mats, geom, d):
    batch, seq, ctx_len = geom
    assert ctx_len == TOKEN_TILE and seq % ctx_len == 0
    lt = geom.lat_tiles
    tm = TOKEN_TILE
    tn = d // 2
    n_col = d // tn
    ctx_blk = seq // ctx_len * batch
    lat_a = pl.BlockSpec((tm, seq), lambda b, j, m: (jnp.minimum(m, lt - 1), 0))
    ctx_a = pl.BlockSpec((ctx_len, ctx_len), lambda b, j, m: (0, 0))
    return pl.pallas_call(
        functools.partial(_seq_dft_kernel, latent_tiles=lt),
        grid=(batch, n_col, lt + 1),
        in_specs=[lat_a, lat_a, ctx_a, ctx_a,
                  pl.BlockSpec((seq, tn), lambda b, j, m: (b, j)),
                  pl.BlockSpec((seq, tn), lambda b, j, m: (b, n_col + j)),
                  pl.BlockSpec((ctx_len, tn), lambda b, j, m: (ctx_blk + b, j)),
                  pl.BlockSpec((ctx_len, tn), lambda b, j, m: (ctx_blk + b, n_col + j))],
        out_specs=pl.BlockSpec((tm, tn), lambda b, j, m: (jnp.where(m < lt, b * lt + m, geom.ctx_tile(b)), j)),
        out_shape=jax.ShapeDtypeStruct((z.shape[0], d), BF16),
        compiler_params=_cparams("parallel", "parallel", "arbitrary"),
        name="seq_dft",
    )(*lat_mats, *ctx_mats, z, z, z, z)


def _dft_mats(n, scale):
    idx = (jnp.arange(n, dtype=jnp.int32)[:, None] * jnp.arange(n, dtype=jnp.int32)[None, :]) % n
    ang = idx.astype(F32) * (2.0 * math.pi / n)
    return (jnp.cos(ang) * scale).astype(BF16), (-jnp.sin(ang) * scale).astype(BF16)


def _load_token_tiles(ref):
    rows = ref.shape[0] // SUBLANES
    return jnp.concatenate([ref[pl.ds(s, rows, stride=SUBLANES), :] for s in range(SUBLANES)], axis=1)


def _store_token_tiles(ref, val):
    rows = ref.shape[0] // SUBLANES
    for s in range(SUBLANES):
        ref[pl.ds(s, rows, stride=SUBLANES), :] = val[:, s * LANES:(s + 1) * LANES]


def _proj_out_kernel(ol_ref, oc_ref, w_ref, x_ref, mod_ref, g_ref, xo_ref, h_ref, *, d, n_lat_tiles):
    def finish(o_ref):
        y = jnp.dot(o_ref[...], w_ref[...], preferred_element_type=F32)
        x = x_ref[...] + mod_ref[:, 2 * d:3 * d] * y
        xo_ref[...] = x
        ms = jnp.mean(x * x, axis=-1, keepdims=True)
        h = x * lax.rsqrt(ms + EPS) * g_ref[...]
        _store_token_tiles(h_ref, h * (1.0 + mod_ref[:, 4 * d:5 * d]) + mod_ref[:, 3 * d:4 * d])

    @pl.when(pl.program_id(0) < n_lat_tiles)
    def _():
        finish(ol_ref)

    @pl.when(pl.program_id(0) >= n_lat_tiles)
    def _():
        finish(oc_ref)


def _proj_out(o_lat, o_ctx, w, xs, mods, g2, geom):
    n, d = xs.shape
    ko = o_lat.shape[1]
    tm = TOKEN_TILE
    nl = geom.n_lat_tiles
    ctx_off = nl if o_ctx.shape[0] == n else 0
    return pl.pallas_call(
        functools.partial(_proj_out_kernel, d=d, n_lat_tiles=nl),
        grid=(n // tm,),
        in_specs=[pl.BlockSpec((tm, ko), lambda i: (jnp.minimum(i, nl - 1), 0)),
                  pl.BlockSpec((tm, ko), lambda i: (jnp.maximum(i, nl) - nl + ctx_off, 0)),
                  pl.BlockSpec((ko, d), lambda i: (0, 0)),
                  pl.BlockSpec((tm, d), lambda i: (i, 0)),
                  pl.BlockSpec((None, 1, mods.shape[-1]), lambda i: (geom.mod_row(i), 0, 0)),
                  pl.BlockSpec((1, d), lambda i: (0, 0))],
        out_specs=[pl.BlockSpec((tm, d), lambda i: (i, 0)),
                   pl.BlockSpec((tm * SUBLANES, LANES), lambda i: (i, 0))],
        out_shape=[jax.ShapeDtypeStruct((n, d), F32), jax.ShapeDtypeStruct((n * SUBLANES, LANES), F32)],
        compiler_params=_cparams("parallel"),
        name="proj_out",
    )(o_lat, o_ctx, w, xs, mods, g2.reshape(1, d))


ROUTER_ROWS = SUBLANES + N_EXPERTS


def _router_kernel(h_ref, wt_ref, b_ref, tri_ref, ints_ref, flt_ref, cnt_ref, carry_ref):
    step = pl.program_id(0)
    tr = h_ref.shape[0] // SUBLANES

    @pl.when(step == 0)
    def _():
        carry_ref[...] = jnp.zeros(carry_ref.shape, F32)

    logits = lax.dot_general(wt_ref[...], _load_token_tiles(h_ref), (((1,), (1,)), ((), ())),
                             precision=HIGHEST, preferred_element_type=F32) + b_ref[:, 0:1]
    row8 = lax.broadcasted_iota(jnp.int32, (SUBLANES, tr), 0)
    lg = jnp.where(row8 < N_GROUPS, logits[0:SUBLANES], NEG_INF)
    lg_max = jnp.max(lg, axis=0, keepdims=True)
    pg = 1.0 / jnp.sum(jnp.exp(lg - lg_max), axis=0, keepdims=True)
    grp = jnp.min(jnp.where(lg == lg_max, row8, SUBLANES), axis=0, keepdims=True)
    l2 = jnp.zeros((EXPERTS_PER_GROUP, tr), F32)
    for g in range(N_GROUPS):
        lo = SUBLANES + g * EXPERTS_PER_GROUP
        l2 = l2 + jnp.where(grp == g, logits[lo:lo + EXPERTS_PER_GROUP], 0.0)
    l2_max = jnp.max(l2, axis=0, keepdims=True)
    j0 = jnp.min(jnp.where(l2 == l2_max, row8, SUBLANES), axis=0, keepdims=True)
    rest = jnp.where(row8 == j0, NEG_INF, l2)
    r_max = jnp.max(rest, axis=0, keepdims=True)
    j1 = jnp.min(jnp.where(rest == r_max, row8, SUBLANES), axis=0, keepdims=True)
    e1 = jnp.exp(r_max - l2_max)
    inv = 1.0 / (1.0 + e1)
    w0 = pg * inv
    w1 = pg * e1 * inv
    ex0 = grp * EXPERTS_PER_GROUP + j0
    ex1 = grp * EXPERTS_PER_GROUP + j1

    rows = lax.broadcasted_iota(jnp.int32, (N_EXPERTS, tr), 0)
    oh0 = (rows == ex0).astype(F32)
    oh1 = (rows == ex1).astype(F32)
    both = oh0 + oh1
    before = jnp.dot(both.astype(BF16), tri_ref[...], preferred_element_type=F32) + carry_ref[:, 0:1]
    rank0 = jnp.sum(oh0 * before, axis=0, keepdims=True)
    rank1 = jnp.sum(oh1 * before, axis=0, keepdims=True)
    carry_ref[...] = carry_ref[...] + jnp.sum(both, axis=1, keepdims=True)
    cnt_ref[...] = carry_ref[...]

    zi = jnp.zeros((SUBLANES - 4, tr), jnp.int32)
    ints_ref[...] = jnp.concatenate([ex0, ex1, rank0.astype(jnp.int32), rank1.astype(jnp.int32), zi], axis=0)
    flt_ref[...] = jnp.concatenate([w0, w1, jnp.zeros((SUBLANES - 2, tr), F32)], axis=0)


def _router(h3, wt, bias, tri):
    n = h3.shape[0] // SUBLANES
    d = wt.shape[1]
    tr = ROUTER_TILE
    return pl.pallas_call(
        _router_kernel,
        grid=(n // tr,),
        in_specs=[pl.BlockSpec((tr * SUBLANES, LANES), lambda i: (i, 0)),
                  pl.BlockSpec((ROUTER_ROWS, d), lambda i: (0, 0)),
                  pl.BlockSpec((ROUTER_ROWS, LANES), lambda i: (0, 0)),
                  pl.BlockSpec((tr, tr), lambda i: (0, 0))],
        out_specs=[pl.BlockSpec((SUBLANES, tr), lambda i: (0, i)),
                   pl.BlockSpec((SUBLANES, tr), lambda i: (0, i)),
                   pl.BlockSpec((N_EXPERTS, LANES), lambda i: (0, 0))],
        out_shape=[jax.ShapeDtypeStruct((SUBLANES, n), jnp.int32),
                   jax.ShapeDtypeStruct((SUBLANES, n), F32),
                   jax.ShapeDtypeStruct((N_EXPERTS, LANES), F32)],
        scratch_shapes=[pltpu.VMEM((N_EXPERTS, LANES), F32)],
        compiler_params=_cparams("arbitrary"),
        name="router",
    )(h3, wt, bias, tri)


def _dest_kernel(pstart_ref, ints_ref, o_ref):
    ints = ints_ref[...]
    ex = ints[0:2]
    base = jnp.zeros(ex.shape, jnp.int32)
    for e in range(N_EXPERTS):
        base = jnp.where(ex == e, pstart_ref[e], base)
    o_ref[...] = jnp.concatenate([base + ints[2:4], jnp.zeros((SUBLANES - 2, ints.shape[1]), jnp.int32)], axis=0)


def _dest_rows(pstart, ints):
    n = ints.shape[1]
    tn = 2048 if n % 2048 == 0 else ROUTER_TILE
    return pl.pallas_call(
        _dest_kernel,
        grid=(n // tn,),
        in_specs=[pl.BlockSpec(memory_space=pltpu.SMEM),
                  pl.BlockSpec((SUBLANES, tn), lambda i: (0, i))],
        out_specs=pl.BlockSpec((SUBLANES, tn), lambda i: (0, i)),
        out_shape=jax.ShapeDtypeStruct((SUBLANES, n), jnp.int32),
        compiler_params=_cparams("parallel"),
        name="dest_rows",
    )(pstart, ints)


def _row_maps_kernel(pad_ref, d0_ref, d1_ref, tok_ref, out_ref, *, n_tokens):
    step = pl.program_id(0)
    td = d0_ref.shape[0]
    n_spans = pad_ref.shape[0] // 2

    @pl.when(step == 0)
    def _():
        def span(e, carry):
            def fill(r, c):
                row = pad_ref[e] + r
                pad = row & (MOE_HALF - 1)
                tok_ref[row] = pad
                out_ref[row] = 2 * n_tokens + pad
                return c
            lax.fori_loop(0, pad_ref[n_spans + e], fill, 0)
            return carry
        lax.fori_loop(0, n_spans, span, 0)

    base = step * td

    def place(r, carry):
        n = base + r
        a0 = d0_ref[r]
        a1 = d1_ref[r]
        tok_ref[a0] = n
        out_ref[a0] = n
        tok_ref[a1] = n
        out_ref[a1] = n_tokens + n
        return carry
    lax.fori_loop(0, td, place, 0, unroll=8)


def _row_maps(pad_spans, dest0, dest1, n_rows):
    n = dest0.shape[0]
    td = 2048 if n % 2048 == 0 else ROUTER_TILE
    smem = functools.partial(pl.BlockSpec, memory_space=pltpu.SMEM)
    return pl.pallas_call(
        functools.partial(_row_maps_kernel, n_tokens=n),
        grid=(n // td,),
        in_specs=[smem(), smem((td,), lambda i: (i,)), smem((td,), lambda i: (i,))],
        out_specs=[smem(), smem()],
        out_shape=[jax.ShapeDtypeStruct((n_rows,), jnp.int32), jax.ShapeDtypeStruct((n_rows,), jnp.int32)],
        compiler_params=_cparams("arbitrary"),
        name="moe_row_maps",
    )(pad_spans, dest0, dest1)


def _expert_kernel(blk_exp_ref, n_used_ref, tok_first, tok_half1, tok_next, out_prev, out_half0, out_last,
                   h_ref, w13_ref, w2_ref, y_ref, xbuf0, xbuf1, ybuf0, ybuf1, w13b, w2b, gsem, ssem,
                   *, d_expert, dump_row):
    j = pl.program_id(0)
    n_used = n_used_ref[0]
    xbuf = (xbuf0, xbuf1)
    ybuf = (ybuf0, ybuf1)

    def tile_of(buf_ref, r):
        return buf_ref.at[pl.ds(pl.multiple_of(r * SUBLANES, SUBLANES), SUBLANES)]

    def gather_row(idx_ref, buf, r):
        pltpu.make_async_copy(h_ref.at[idx_ref[r]], tile_of(xbuf[buf], r), gsem.at[buf]).start()

    def scatter_row(idx_ref, buf, r):
        pltpu.make_async_copy(tile_of(ybuf[buf], r), y_ref.at[idx_ref[r]], ssem.at[buf]).start()

    def wait_gather(buf):
        pltpu.make_async_copy(h_ref.at[pl.ds(0, MOE_HALF)], y_ref.at[pl.ds(0, MOE_HALF)], gsem.at[buf]).wait()

    def wait_scatter(buf):
        pltpu.make_async_copy(h_ref.at[pl.ds(0, MOE_HALF)], y_ref.at[pl.ds(0, MOE_HALF)], ssem.at[buf]).wait()

    def half_block(buf, gather_idx, scatter_idx):
        for r in range(MOE_HALF):
            gather_row(gather_idx, 1 - buf, r)
            scatter_row(scatter_idx, 1 - buf, r)
        x = _load_token_tiles(xbuf[buf]).astype(BF16)
        gu = jnp.dot(x, w13b[...], preferred_element_type=F32)
        g = gu[:, 0:d_expert]
        u = gu[:, d_expert:2 * d_expert]
        a = g * (1.0 / (1.0 + jnp.exp(-g))) * u
        _store_token_tiles(ybuf[buf], jnp.dot(a.astype(BF16), w2b[...], preferred_element_type=F32))

    @pl.when(j < n_used)
    def _():
        @pl.when(j == 0)
        def _():
            ybuf0[...] = jnp.zeros(ybuf0.shape, F32)
            ybuf1[...] = jnp.zeros(ybuf1.shape, F32)

            def first_rows(r, carry):
                pltpu.make_async_copy(tile_of(ybuf0, r), y_ref.at[dump_row + r], ssem.at[0]).start()
                gather_row(tok_first, 0, r)
                return carry
            lax.fori_loop(0, MOE_HALF, first_rows, 0)
            wait_scatter(0)

        @pl.when((j == 0) | (blk_exp_ref[j] != blk_exp_ref[jnp.maximum(j - 1, 0)]))
        def _():
            w13b[...] = w13_ref[...].astype(BF16)
            w2b[...] = w2_ref[...].astype(BF16)

        wait_gather(0)

        @pl.when(j > 0)
        def _():
            wait_scatter(0)

        half_block(0, tok_half1, out_prev)
        wait_gather(1)
        wait_scatter(1)
        half_block(1, tok_next, out_half0)

        @pl.when(j == n_used - 1)
        def _():
            wait_gather(0)
            wait_scatter(0)

            def last_scatter(r, carry):
                scatter_row(out_last, 1, r)
                return carry
            lax.fori_loop(0, MOE_HALF, last_scatter, 0)
            wait_scatter(1)


def _experts(blk_exp, n_used, row_tok, row_out, h3, w13, w2):
    n = h3.shape[0] // SUBLANES
    d, d_expert = w13.shape[1], w2.shape[1]
    n_steps = row_tok.shape[0] // MOE_TILE
    smem = functools.partial(pl.BlockSpec, memory_space=pltpu.SMEM)

    def half_spec(half_of):
        def index(j, be, nu):
            last = 2 * nu[0] - 1
            return (jnp.clip(half_of(jnp.minimum(j, nu[0] - 1), last), 0, last),)
        return smem((MOE_HALF,), index)

    def expert(j, be, nu):
        return (be[jnp.minimum(j, nu[0] - 1)], 0, 0)

    token_tile = (MOE_HALF * SUBLANES, LANES)
    grid_spec = pltpu.PrefetchScalarGridSpec(
        num_scalar_prefetch=2,
        grid=(n_steps,),
        in_specs=[half_spec(lambda j, last: 0),
                  half_spec(lambda j, last: 2 * j + 1),
                  half_spec(lambda j, last: 2 * j + 2),
                  half_spec(lambda j, last: 2 * j - 1),
                  half_spec(lambda j, last: 2 * j),
                  half_spec(lambda j, last: last),
                  pl.BlockSpec(memory_space=pl.ANY),
                  pl.BlockSpec((None, d, 2 * d_expert), expert),
                  pl.BlockSpec((None, d_expert, d), expert)],
        out_specs=pl.BlockSpec(memory_space=pl.ANY),
        scratch_shapes=[pltpu.VMEM(token_tile, F32), pltpu.VMEM(token_tile, F32),
                        pltpu.VMEM(token_tile, F32), pltpu.VMEM(token_tile, F32),
                        pltpu.VMEM((d, 2 * d_expert), BF16), pltpu.VMEM((d_expert, d), BF16),
                        pltpu.SemaphoreType.DMA((2,)), pltpu.SemaphoreType.DMA((2,))],
    )
    y = pl.pallas_call(
        functools.partial(_expert_kernel, d_expert=d_expert, dump_row=2 * n),
        grid_spec=grid_spec,
        out_shape=jax.ShapeDtypeStruct((2 * n + MOE_HALF, SUBLANES, LANES), F32),
        compiler_params=_cparams("arbitrary"),
        name="moe_experts",
    )(blk_exp, n_used, row_tok, row_tok, row_tok, row_out, row_out, row_out,
      h3.reshape(n, SUBLANES, LANES), w13, w2)
    return y.reshape((2 * n + MOE_HALF) * SUBLANES, LANES)


def _combine_kernel(w_ref, x_ref, mod_ref, y0_ref, y1_ref, o_ref, *, d):
    out = w_ref[:, 0:1] * _load_token_tiles(y0_ref) + w_ref[:, 1:2] * _load_token_tiles(y1_ref)
    o_ref[...] = x_ref[...] + mod_ref[:, 5 * d:6 * d] * out


def _combine(w_cols, xs, mods, y_tok, geom):
    n, d = xs.shape
    tc = TOKEN_TILE
    return pl.pallas_call(
        functools.partial(_combine_kernel, d=d),
        grid=(n // tc,),
        in_specs=[pl.BlockSpec((tc, 2), lambda i: (i, 0)),
                  pl.BlockSpec((tc, d), lambda i: (i, 0)),
                  pl.BlockSpec((None, 1, mods.shape[-1]), lambda i: (geom.mod_row(i), 0, 0)),
                  pl.BlockSpec((tc * SUBLANES, LANES), lambda i: (i, 0)),
                  pl.BlockSpec((tc * SUBLANES, LANES), lambda i: (n // tc + i, 0))],
        out_specs=pl.BlockSpec((tc, d), lambda i: (i, 0)),
        out_shape=jax.ShapeDtypeStruct((n, d), F32),
        compiler_params=_cparams("parallel"),
        name="moe_combine",
    )(w_cols, xs, mods, y_tok, y_tok)


def _hier_moe(h3, xs, mods, geom, w_r1, b_r1, w_r2, b_r2, w13, w2):
    n, d = xs.shape
    wt = jnp.zeros((ROUTER_ROWS, d), F32)
    wt = wt.at[0:N_GROUPS].set(w_r1.T)
    wt = wt.at[SUBLANES:].set(jnp.transpose(w_r2, (0, 2, 1)).reshape(N_EXPERTS, d))
    bias = jnp.zeros((ROUTER_ROWS,), F32).at[0:N_GROUPS].set(b_r1).at[SUBLANES:].set(b_r2.reshape(-1))
    bias = jnp.broadcast_to(bias[:, None], (ROUTER_ROWS, LANES))
    tri = jnp.asarray(np.triu(np.ones((ROUTER_TILE, ROUTER_TILE), np.float32), 1), BF16)
    ints, flt, cnt = _router(h3, wt, bias, tri)

    counts = cnt[:, 0].astype(jnp.int32)
    padded = ((counts + MOE_TILE - 1) // MOE_TILE) * MOE_TILE
    pend = jnp.cumsum(padded)
    pstart = pend - padded
    n_blocks = (2 * n) // MOE_TILE + N_EXPERTS
    n_rows = n_blocks * MOE_TILE
    blk_row0 = jnp.arange(n_blocks, dtype=jnp.int32) * MOE_TILE
    blk_exp = jnp.minimum(jnp.sum(pend[None, :] <= blk_row0[:, None], axis=1), N_EXPERTS - 1).astype(jnp.int32)

    n_used = (pend[-1] // MOE_TILE).astype(jnp.int32)
    pad_spans = jnp.concatenate([pstart + counts, pend[-1:], padded - counts, n_rows - pend[-1:]]).astype(jnp.int32)

    dest = _dest_rows(pstart.astype(jnp.int32), ints)
    row_tok, row_out = _row_maps(pad_spans, dest[0], dest[1], n_rows)
    y_tok = _experts(blk_exp, n_used[None], row_tok, row_out, h3, w13, w2)
    return _combine(flt[0:2].T, xs, mods, y_tok, geom)


def _rope_tables(seq, ctx_len):
    rows = seq // GRID_W
    row = jnp.repeat(jnp.arange(rows, dtype=F32), GRID_W)
    col = jnp.tile(jnp.arange(GRID_W, dtype=F32), rows)
    inv = ROPE_THETA ** (-jnp.arange(ROT_FREQS, dtype=F32) / ROT_FREQS)
    ang_r, ang_c = row[:, None] * inv, col[:, None] * inv
    cos_h = jnp.concatenate([jnp.cos(ang_r)] * 2 + [jnp.cos(ang_c)] * 2, axis=1)
    sin_h = jnp.concatenate([-jnp.sin(ang_r), jnp.sin(ang_r), -jnp.sin(ang_c), jnp.sin(ang_c)], axis=1)
    cos_t = jnp.concatenate([jnp.tile(cos_h, (1, 2)), jnp.ones((ctx_len, LANES), F32)], axis=0)
    sin_t = jnp.concatenate([jnp.tile(sin_h, (1, 2)), jnp.zeros((ctx_len, LANES), F32)], axis=0)
    return cos_t, sin_t


def _pad_heads_cols(w, n_heads):
    d = w.shape[0]
    w3 = w.reshape(d, n_heads, HEAD_DIM)
    return jnp.concatenate([w3, jnp.zeros_like(w3)], axis=2).reshape(d, n_heads * LANES)


def kernel(x, c, ctx, c_ctx, ada_w, ada_b, norm1_g, norm2_g, a_wqkv, a_wo, a_q_norm, a_k_norm, a_lambda_q1, a_lambda_k1, a_lambda_q2, a_lambda_k2, a_subln_g, b_wqkv, b_wo, b_q_norm, b_k_norm, b_sink, f_wo, r_w1, r_b1, r_w2, r_b2, e_w13, e_w2):
    batch, seq, d = x.shape
    ctx_len = ctx.shape[1]
    depth = ada_w.shape[0]
    n = batch * (seq + ctx_len)
    assert seq % TOKEN_TILE == 0 and ctx_len % TOKEN_TILE == 0 and n % ROUTER_TILE == 0
    geom = Geom(batch, seq, ctx_len)

    xs = jnp.concatenate([x.reshape(batch * seq, d), ctx.reshape(batch * ctx_len, d)], axis=0)
    mod_rows = ((batch + 1 + SUBLANES - 1) // SUBLANES) * SUBLANES
    cvec = jnp.zeros((mod_rows, d), F32).at[0:batch].set(c).at[batch].set(c_ctx)
    mods_all = _ada_table(cvec, ada_w, ada_b).reshape(depth, mod_rows, 1, 6 * d)
    cos_t, sin_t = _rope_tables(seq, ctx_len)

    for i in range(depth):
        kind, j = i % N_MIXERS, i // N_MIXERS
        mods = mods_all[i]
        if kind == 0:
            lam_init = 0.8 - 0.6 * math.exp(-0.3 * i)
            nh = d // LANES
            gains = jnp.stack([jnp.tile(a_q_norm[j], 2), jnp.tile(a_k_norm[j], 2)])
            q, k, v1 = _proj_in(xs, mods, norm1_g[i], a_wqkv[j].astype(BF16), gains, cos_t, sin_t, geom,
                                n_q=nh, n_k=nh, n_plain=nh, ones_after_plain=True,
                                q_scale=HEAD_DIM ** -0.5 * math.log2(math.e))
            lam_vecs = jnp.stack([a_lambda_q1[j], a_lambda_k1[j], a_lambda_q2[j], a_lambda_k2[j]])
            o_lat, o_ctx = _diff_attention(q, k, v1, lam_vecs, a_subln_g[j], geom, lam_init)
            wo = a_wo[j].astype(BF16)
        elif kind == 1:
            q_heads = d // HEAD_DIM
            nq, nk = q_heads * HEAD_DIM, SWA_KV_HEADS * HEAD_DIM
            w = b_wqkv[j]
            w_pad = jnp.concatenate([_pad_heads_cols(w[:, :nq], q_heads),
                                     _pad_heads_cols(w[:, nq:nq + nk], SWA_KV_HEADS),
                                     _pad_heads_cols(w[:, nq + nk:], SWA_KV_HEADS)], axis=1).astype(BF16)
            zeros64 = jnp.zeros((HEAD_DIM,), F32)
            gains = jnp.stack([jnp.concatenate([b_q_norm[j], zeros64]), jnp.concatenate([b_k_norm[j], zeros64])])
            q, k, v = _proj_in(xs, mods, norm1_g[i], w_pad, gains, cos_t, sin_t, geom,
                               n_q=q_heads, n_k=SWA_KV_HEADS, n_plain=SWA_KV_HEADS)
            o_lat = o_ctx = _swa_attention(q, k, v, b_sink[j], geom)
            wo3 = b_wo[j].reshape(q_heads, HEAD_DIM, d)
            wo = jnp.concatenate([wo3, jnp.zeros_like(wo3)], axis=1).reshape(q_heads * LANES, d).astype(BF16)
        else:
            gd = d // FOURIER_GROUPS
            cd, msd = _dft_mats(gd, gd ** -0.5)
            eye = jnp.eye(FOURIER_GROUPS, dtype=BF16)
            w_cs = jnp.concatenate([jnp.kron(eye, cd), jnp.kron(eye, -msd)], axis=1)
            (z,) = _proj_in(xs, mods, norm1_g[i], w_cs, jnp.zeros((2, LANES), F32), cos_t, sin_t, geom,
                            n_q=0, n_k=0, n_plain=2 * d // LANES)
            o_lat = o_ctx = _seq_dft(z, _dft_mats(seq, seq ** -0.5), _dft_mats(ctx_len, ctx_len ** -0.5), geom, d)
            wo = f_wo[j].astype(BF16)
        xs, h2 = _proj_out(o_lat, o_ctx, wo, xs, mods, norm2_g[i], geom)
        xs = _hier_moe(h2, xs, mods, geom, r_w1[i], r_b1[i], r_w2[i], r_b2[i], e_w13[i], e_w2[i])

    return xs[:batch * seq].reshape(batch, seq, d)
```

```python
import functools
import math
from typing import NamedTuple

import numpy as np
import jax
import jax.numpy as jnp
from jax import lax
from jax.experimental import pallas as pl
from jax.experimental.pallas import tpu as pltpu

F32 = jnp.float32
BF16 = jnp.bfloat16
HIGHEST = lax.Precision.HIGHEST

GRID_W = 64
HEAD_DIM = 64
ROT_FREQS = HEAD_DIM // 4
ROPE_THETA = 10000.0
WINDOW = 128
N_MIXERS = 3
SWA_KV_HEADS = 4
FOURIER_GROUPS = 4
N_GROUPS = 4
EXPERTS_PER_GROUP = 8
N_EXPERTS = N_GROUPS * EXPERTS_PER_GROUP
EPS = 1e-6
NEG_INF = -1e30

LANES = 128
SUBLANES = 8
MXU_DIM = 256
TOKEN_TILE = 256
PROJ_IN_TILE = 512
MOE_HALF = 256
MOE_TILE = 2 * MOE_HALF
ROUTER_TILE = 512
VMEM_LIMIT = 48 * 1024 * 1024


def _cparams(*sem):
    return pltpu.CompilerParams(dimension_semantics=sem, vmem_limit_bytes=VMEM_LIMIT)


class Geom(NamedTuple):
    batch: int
    seq: int
    ctx_len: int

    @property
    def lat_tiles(self):
        return self.seq // TOKEN_TILE

    @property
    def n_lat_tiles(self):
        return self.batch * self.lat_tiles

    @property
    def n_tiles(self):
        return self.n_lat_tiles + self.batch * (self.ctx_len // TOKEN_TILE)

    def mod_row(self, i):
        return jnp.where(i < self.n_lat_tiles, i // self.lat_tiles, self.batch)

    def ctx_tile(self, b):
        return self.n_lat_tiles + b


def _ada_kernel(c_ref, w_ref, b_ref, o_ref):
    c = c_ref[...]
    s = c * (1.0 / (1.0 + jnp.exp(-c)))
    o_ref[...] = jnp.dot(s, w_ref[...], precision=HIGHEST, preferred_element_type=F32) + b_ref[...]


def _ada_table(cvec, ada_w, ada_b):
    depth, d, n6 = ada_w.shape
    rows = cvec.shape[0]
    tn = n6 // 4
    return pl.pallas_call(
        _ada_kernel,
        grid=(depth, n6 // tn),
        in_specs=[pl.BlockSpec((rows, d), lambda l, j: (0, 0)),
                  pl.BlockSpec((None, d, tn), lambda l, j: (l, 0, j)),
                  pl.BlockSpec((None, 1, tn), lambda l, j: (l, 0, j))],
        out_specs=pl.BlockSpec((None, rows, tn), lambda l, j: (l, 0, j)),
        out_shape=jax.ShapeDtypeStruct((depth, rows, n6), F32),
        compiler_params=_cparams("parallel", "parallel"),
        name="ada_table",
    )(cvec, ada_w, ada_b.reshape(depth, 1, n6))


def _head_mean_matrix():
    r = np.arange(LANES)
    return jnp.asarray((r[:, None] // HEAD_DIM == r[None, :] // HEAD_DIM).astype(np.float32) / HEAD_DIM, BF16)


def _proj_in_kernel(x_ref, mod_ref, g_ref, w_ref, hm_ref, gain_ref, cos_ref, sin_ref, *out_refs,
                    d, n_q, n_k, n_plain, ones_after_plain, q_scale):
    q_ref = out_refs[0] if n_q else None
    k_ref = out_refs[1] if n_k else None
    p_ref = out_refs[-1]
    x = x_ref[...]
    ms = jnp.mean(x * x, axis=-1, keepdims=True)
    h = x * lax.rsqrt(ms + EPS) * g_ref[...]
    h = h * (1.0 + mod_ref[:, d:2 * d]) + mod_ref[:, 0:d]
    hb = h.astype(BF16)
    lane = lax.broadcasted_iota(jnp.int32, (1, LANES), 1)
    first_half = (lane // ROT_FREQS) % 2 == 0
    n_chunks = n_q + n_k + n_plain
    for c2 in range(0, n_chunks, 2):
        width = min(2, n_chunks - c2) * LANES
        y2 = jnp.dot(hb, w_ref[:, c2 * LANES:c2 * LANES + width], preferred_element_type=F32)
        for half in range(width // LANES):
            c = c2 + half
            y = y2[:, half * LANES:(half + 1) * LANES]
            if c < n_q + n_k:
                msq = jnp.dot((y * y).astype(BF16), hm_ref[...], preferred_element_type=F32)
                is_q = c < n_q
                gain = gain_ref[0:1, :] if is_q else gain_ref[1:2, :]
                yn = y * lax.rsqrt(msq + EPS) * gain
                partner = jnp.where(first_half, pltpu.roll(yn, LANES - ROT_FREQS, 1), pltpu.roll(yn, ROT_FREQS, 1))
                out = yn * cos_ref[...] + partner * sin_ref[...]
                if is_q:
                    q_ref[:, c * LANES:(c + 1) * LANES] = (out * q_scale).astype(q_ref.dtype)
                else:
                    ck = c - n_q
                    k_ref[:, ck * LANES:(ck + 1) * LANES] = out.astype(k_ref.dtype)
            else:
                cp = c - n_q - n_k
                if ones_after_plain:
                    p_ref[:, 2 * cp * LANES:(2 * cp + 1) * LANES] = y.astype(p_ref.dtype)
                    p_ref[:, (2 * cp + 1) * LANES:(2 * cp + 2) * LANES] = jnp.ones(y.shape, p_ref.dtype)
                else:
                    p_ref[:, cp * LANES:(cp + 1) * LANES] = y.astype(p_ref.dtype)


def _proj_in(xs, mods, g, w, gains, cos_t, sin_t, geom, *, n_q, n_k, n_plain, ones_after_plain=False,
             q_scale=HEAD_DIM ** -0.5):
    n, d = xs.shape
    tm = PROJ_IN_TILE
    lat_tiles = geom.seq // tm
    n_lat = geom.batch * lat_tiles
    assert geom.seq % tm == 0 and n % tm == 0 and cos_t.shape[0] == geom.seq + tm

    def mod_row(i):
        return jnp.where(i < n_lat, i // lat_tiles, geom.batch)

    def rope_row(i):
        return jnp.where(i < n_lat, i % lat_tiles, lat_tiles)

    n_cols = (n_q + n_k + n_plain) * LANES
    assert w.shape == (d, n_cols)
    out_shape, out_specs = [], []
    for cnt in (n_q, n_k):
        if cnt:
            out_shape.append(jax.ShapeDtypeStruct((n, cnt * LANES), BF16))
            out_specs.append(pl.BlockSpec((tm, cnt * LANES), lambda i: (i, 0)))
    pw = n_plain * LANES * (2 if ones_after_plain else 1)
    out_shape.append(jax.ShapeDtypeStruct((n, pw), BF16))
    out_specs.append(pl.BlockSpec((tm, pw), lambda i: (i, 0)))
    kern = functools.partial(_proj_in_kernel, d=d, n_q=n_q, n_k=n_k, n_plain=n_plain,
                             ones_after_plain=ones_after_plain, q_scale=q_scale)
    return pl.pallas_call(
        kern,
        grid=(n // tm,),
        in_specs=[pl.BlockSpec((tm, d), lambda i: (i, 0)),
                  pl.BlockSpec((None, 1, mods.shape[-1]), lambda i: (mod_row(i), 0, 0)),
                  pl.BlockSpec((1, d), lambda i: (0, 0)),
                  pl.BlockSpec((d, n_cols), lambda i: (0, 0)),
                  pl.BlockSpec((LANES, LANES), lambda i: (0, 0)),
                  pl.BlockSpec((2, LANES), lambda i: (0, 0)),
                  pl.BlockSpec((tm, LANES), lambda i: (rope_row(i), 0)),
                  pl.BlockSpec((tm, LANES), lambda i: (rope_row(i), 0))],
        out_specs=out_specs,
        out_shape=out_shape,
        compiler_params=_cparams("parallel"),
        name="proj_in",
    )(xs, mods, g.reshape(1, d), w, _head_mean_matrix(), gains, cos_t, sin_t)


DIFF_KV_CHUNK = 512
DIFF_Q_TILE = 256


def _diff_attn_kernel(*refs, lam_init, n_kv, chunks):
    lam_ref, q_ref = refs[0], refs[1]
    k_refs = refs[2:2 + n_kv]
    v_refs = refs[2 + n_kv:2 + 2 * n_kv]
    sg_ref, o_ref, s_ref, m_ref = refs[2 + 2 * n_kv:]
    dv = o_ref.shape[1]
    in_grid = pl.program_id(0) < pl.num_programs(0)

    @pl.when(in_grid)
    def _():
        q = q_ref[...]
        lane = lax.broadcasted_iota(jnp.int32, q.shape, 1)
        zero = jnp.zeros_like(q)
        q_maps = (jnp.where(lane < HEAD_DIM, q, zero), jnp.where(lane >= HEAD_DIM, q, zero))
        for m in range(2):
            mx = None
            for slab, start, size, col in chunks:
                s = lax.dot_general(q_maps[m], k_refs[slab][start:start + size, :], (((1,), (1,)), ((), ())),
                                    preferred_element_type=F32)
                s_ref[m, :, col:col + size] = s
                cm = jnp.max(s, axis=1, keepdims=True)
                mx = cm if mx is None else jnp.maximum(mx, cm)
            m_ref[m] = mx

    @pl.when(in_grid)
    def _():
        outs = []
        for m in range(2):
            row_max = m_ref[m]
            acc = None
            for slab, start, size, col in chunks:
                p = jnp.exp2(s_ref[m, :, col:col + size] - row_max)
                pv = jnp.dot(p.astype(BF16), v_refs[slab][start:start + size, :], preferred_element_type=F32)
                acc = pv if acc is None else acc + pv
            outs.append(acc[:, 0:dv] * (1.0 / acc[:, dv:dv + 1]))
        lam_v = lam_ref[...]
        lam = (jnp.exp(jnp.sum(lam_v[0:1] * lam_v[1:2], axis=1, keepdims=True))
               - jnp.exp(jnp.sum(lam_v[2:3] * lam_v[3:4], axis=1, keepdims=True)) + lam_init)
        a = outs[0] - lam * outs[1]
        a = a * lax.rsqrt(jnp.mean(a * a, axis=-1, keepdims=True) + EPS) * sg_ref[...] * (1.0 - lam_init)
        o_ref[...] = a.astype(o_ref.dtype)


def _diff_attention(q, k, v1, lam_vecs, sub_g, geom, lam_init):
    dq = q.shape[1]
    batch, seq, ctx_len = geom
    heads = dq // (2 * HEAD_DIM)
    dv = 2 * HEAD_DIM
    tq = DIFF_Q_TILE
    q_tiles = seq // tq
    ctx_blk = seq // ctx_len * batch
    lam_spec = pl.BlockSpec((4, HEAD_DIM), lambda *_: (0, 0))
    sg_spec = pl.BlockSpec((1, dv), lambda *_: (0, 0))
    chunks = [(0, c * DIFF_KV_CHUNK, DIFF_KV_CHUNK, c * DIFF_KV_CHUNK) for c in range(seq // DIFF_KV_CHUNK)]
    chunks.append((1, 0, ctx_len, seq))
    o_lat = pl.pallas_call(
        functools.partial(_diff_attn_kernel, lam_init=lam_init, n_kv=2, chunks=chunks),
        grid=(batch, heads, q_tiles),
        in_specs=[lam_spec,
                  pl.BlockSpec((tq, 2 * HEAD_DIM), lambda b, h, i: (b * q_tiles + i, h)),
                  pl.BlockSpec((seq, 2 * HEAD_DIM), lambda b, h, i: (b, h)),
                  pl.BlockSpec((ctx_len, 2 * HEAD_DIM), lambda b, h, i: (ctx_blk + b, h)),
                  pl.BlockSpec((seq, 2 * dv), lambda b, h, i: (b, h)),
                  pl.BlockSpec((ctx_len, 2 * dv), lambda b, h, i: (ctx_blk + b, h)),
                  sg_spec],
        out_specs=pl.BlockSpec((tq, dv), lambda b, h, i: (b * q_tiles + i, h)),
        out_shape=jax.ShapeDtypeStruct((batch * seq, heads * dv), BF16),
        scratch_shapes=[pltpu.VMEM((2, tq, seq + ctx_len), F32), pltpu.VMEM((2, tq, 1), F32)],
        compiler_params=_cparams("parallel", "parallel", "arbitrary"),
        name="diff_attention",
    )(lam_vecs, q, k, k, v1, v1, sub_g.reshape(1, dv))
    o_ctx = pl.pallas_call(
        functools.partial(_diff_attn_kernel, lam_init=lam_init, n_kv=1, chunks=[(0, 0, ctx_len, 0)]),
        grid=(batch, heads),
        in_specs=[lam_spec,
                  pl.BlockSpec((ctx_len, 2 * HEAD_DIM), lambda b, h: (ctx_blk + b, h)),
                  pl.BlockSpec((ctx_len, 2 * HEAD_DIM), lambda b, h: (ctx_blk + b, h)),
                  pl.BlockSpec((ctx_len, 2 * dv), lambda b, h: (ctx_blk + b, h)),
                  sg_spec],
        out_specs=pl.BlockSpec((ctx_len, dv), lambda b, h: (b, h)),
        out_shape=jax.ShapeDtypeStruct((batch * ctx_len, heads * dv), BF16),
        scratch_shapes=[pltpu.VMEM((2, ctx_len, ctx_len), F32), pltpu.VMEM((2, ctx_len, 1), F32)],
        compiler_params=_cparams("parallel", "parallel"),
        name="diff_attention_ctx",
    )(lam_vecs, q, k, v1, sub_g.reshape(1, dv))
    return o_lat, o_ctx


def _swa_kernel(sink_ref, q_ref, kl_ref, kc_ref, vl_ref, vc_ref, o_ref, *, latent_tiles, group):
    i = pl.program_id(1)
    tq = q_ref.shape[0]
    seq, ctx_len = kl_ref.shape[0], kc_ref.shape[0]
    span = tq + 2 * WINDOW
    start = pl.multiple_of(jnp.clip(i * tq - WINDOW, 0, seq - span), WINDOW)
    qpos = i * tq + lax.broadcasted_iota(jnp.int32, (tq, span), 0)
    kpos = start + lax.broadcasted_iota(jnp.int32, (tq, span), 1)
    valid = (jnp.abs(kpos - qpos) <= WINDOW) & (i < latent_tiles)
    bias = jnp.concatenate([jnp.where(valid, 0.0, NEG_INF).astype(F32), jnp.zeros((tq, ctx_len), F32)], axis=1)
    kw = kl_ref[pl.ds(start, span), :]
    vw = vl_ref[pl.ds(start, span), :]
    kc = kc_ref[...]
    vc = vc_ref[...]
    for h in range(SWA_KV_HEADS):
        hs = slice(h * LANES, (h + 1) * LANES)
        keys = jnp.concatenate([kw[:, hs], kc[:, hs]], axis=0)
        vals = jnp.concatenate([vw[:, hs], vc[:, hs]], axis=0)
        for g in range(group):
            hq = h * group + g
            q = q_ref[:, hq * LANES:(hq + 1) * LANES]
            s = lax.dot_general(q, keys, (((1,), (1,)), ((), ())), preferred_element_type=F32) + bias
            sink = sink_ref[hq]
            m = jnp.maximum(jnp.max(s, axis=1, keepdims=True), sink)
            p = jnp.exp(s - m)
            denom = jnp.sum(p, axis=1, keepdims=True) + jnp.exp(sink - m)
            o = jnp.dot(p.astype(BF16), vals, preferred_element_type=F32) * (1.0 / denom)
            o_ref[:, hq * LANES:(hq + 1) * LANES] = o.astype(o_ref.dtype)


def _swa_attention(q, k, v, sink, geom):
    n, dq = q.shape
    batch, seq, ctx_len = geom
    assert ctx_len == TOKEN_TILE
    lt = geom.lat_tiles
    q_heads = dq // LANES
    kvw = SWA_KV_HEADS * LANES
    ctx_blk = seq // ctx_len * batch
    kern = functools.partial(_swa_kernel, latent_tiles=lt, group=q_heads // SWA_KV_HEADS)

    def tile(b, i):
        return jnp.where(i < lt, b * lt + i, geom.ctx_tile(b))

    lat_kv = pl.BlockSpec((seq, kvw), lambda b, i: (b, 0))
    ctx_kv = pl.BlockSpec((ctx_len, kvw), lambda b, i: (ctx_blk + b, 0))
    return pl.pallas_call(
        kern,
        grid=(batch, lt + 1),
        in_specs=[pl.BlockSpec(memory_space=pltpu.SMEM),
                  pl.BlockSpec((TOKEN_TILE, dq), lambda b, i: (tile(b, i), 0)),
                  lat_kv, ctx_kv, lat_kv, ctx_kv],
        out_specs=pl.BlockSpec((TOKEN_TILE, dq), lambda b, i: (tile(b, i), 0)),
        out_shape=jax.ShapeDtypeStruct((n, dq), BF16),
        compiler_params=_cparams("parallel", "arbitrary"),
        name="swa_attention",
    )(sink, q, k, k, v, v)


def _seq_dft_kernel(base_ref, fine_c_ref, fine_s_ref, cc_ref, cs_ref, zc_ref, zs_ref, yc_ref, ys_ref, o_ref,
                    *, latent_tiles):
    m = pl.program_id(2)

    def mix(a_cos, a_msin, z_cos, z_sin):
        acc = jnp.dot(a_cos, z_cos[...], preferred_element_type=F32)
        acc = acc + jnp.dot(a_msin, z_sin[...], preferred_element_type=F32)
        o_ref[...] = acc.astype(o_ref.dtype)

    @pl.when(m < latent_tiles)
    def _():
        c1, s1 = base_ref[0:1, :], base_ref[1:2, :]
        c2, s2 = fine_c_ref[...], fine_s_ref[...]
        a_cos = (c2 * c1 - s2 * s1).astype(BF16)
        a_msin = (-(c2 * s1) - s2 * c1).astype(BF16)
        mix(a_cos, a_msin, zc_ref, zs_ref)

    @pl.when(m >= latent_tiles)
    def _():
        mix(cc_ref[...], cs_ref[...], yc_ref, ys_ref)


def _dft_angle_tables(n, rows, scale):
    k = jnp.arange(n, dtype=jnp.int32)[None, :]
    step = 2.0 * math.pi / n
    ang0 = (((jnp.arange(n // rows, dtype=jnp.int32) * rows)[:, None] * k) % n).astype(F32) * step
    ang1 = ((jnp.arange(rows, dtype=jnp.int32)[:, None] * k) % n).astype(F32) * step
    base = jnp.stack([jnp.cos(ang0), jnp.sin(ang0)], axis=1)
    return base, jnp.cos(ang1) * scale, jnp.sin(ang1) * scale


def _seq_dft(z, ctx_mats, geom, d):
    batch, seq, ctx_len = geom
    assert ctx_len == TOKEN_TILE and seq % ctx_len == 0
    lt = geom.lat_tiles
    tm = TOKEN_TILE
    tn = d // 2
    n_col = d // tn
    ctx_blk = seq // ctx_len * batch
    lat_tables = _dft_angle_tables(seq, tm, seq ** -0.5)
    fine = pl.BlockSpec((tm, seq), lambda b, j, m: (0, 0))
    ctx_a = pl.BlockSpec((ctx_len, ctx_len), lambda b, j, m: (0, 0))
    return pl.pallas_call(
        functools.partial(_seq_dft_kernel, latent_tiles=lt),
        grid=(batch, n_col, lt + 1),
        in_specs=[pl.BlockSpec((None, 2, seq), lambda b, j, m: (jnp.minimum(m, lt - 1), 0, 0)),
                  fine, fine, ctx_a, ctx_a,
                  pl.BlockSpec((seq, tn), lambda b, j, m: (b, j)),
                  pl.BlockSpec((seq, tn), lambda b, j, m: (b, n_col + j)),
                  pl.BlockSpec((ctx_len, tn), lambda b, j, m: (ctx_blk + b, j)),
                  pl.BlockSpec((ctx_len, tn), lambda b, j, m: (ctx_blk + b, n_col + j))],
        out_specs=pl.BlockSpec((tm, tn), lambda b, j, m: (jnp.where(m < lt, b * lt + m, geom.ctx_tile(b)), j)),
        out_shape=jax.ShapeDtypeStruct((z.shape[0], d), BF16),
        compiler_params=_cparams("parallel", "parallel", "arbitrary"),
        name="seq_dft",
    )(*lat_tables, *ctx_mats, z, z, z, z)


def _dft_mats(n, scale):
    idx = (jnp.arange(n, dtype=jnp.int32)[:, None] * jnp.arange(n, dtype=jnp.int32)[None, :]) % n
    ang = idx.astype(F32) * (2.0 * math.pi / n)
    return (jnp.cos(ang) * scale).astype(BF16), (-jnp.sin(ang) * scale).astype(BF16)


def _load_token_tiles(ref):
    rows = ref.shape[0] // SUBLANES
    return jnp.concatenate([ref[pl.ds(s, rows, stride=SUBLANES), :] for s in range(SUBLANES)], axis=1)


def _store_token_tiles(ref, val):
    rows = ref.shape[0] // SUBLANES
    for s in range(SUBLANES):
        ref[pl.ds(s, rows, stride=SUBLANES), :] = val[:, s * LANES:(s + 1) * LANES]


def _proj_out_kernel(ol_ref, oc_ref, w_ref, x_ref, mod_ref, g_ref, xo_ref, h_ref, *, d, n_lat_tiles):
    def finish(o_ref):
        y = jnp.dot(o_ref[...], w_ref[...], preferred_element_type=F32)
        x = x_ref[...] + mod_ref[:, 2 * d:3 * d] * y
        xo_ref[...] = x
        ms = jnp.mean(x * x, axis=-1, keepdims=True)
        h = x * lax.rsqrt(ms + EPS) * g_ref[...]
        _store_token_tiles(h_ref, h * (1.0 + mod_ref[:, 4 * d:5 * d]) + mod_ref[:, 3 * d:4 * d])

    @pl.when(pl.program_id(0) < n_lat_tiles)
    def _():
        finish(ol_ref)

    @pl.when(pl.program_id(0) >= n_lat_tiles)
    def _():
        finish(oc_ref)


def _proj_out(o_lat, o_ctx, w, xs, mods, g2, geom):
    n, d = xs.shape
    ko = o_lat.shape[1]
    tm = TOKEN_TILE
    nl = geom.n_lat_tiles
    ctx_off = nl if o_ctx.shape[0] == n else 0
    return pl.pallas_call(
        functools.partial(_proj_out_kernel, d=d, n_lat_tiles=nl),
        grid=(n // tm,),
        in_specs=[pl.BlockSpec((tm, ko), lambda i: (jnp.minimum(i, nl - 1), 0)),
                  pl.BlockSpec((tm, ko), lambda i: (jnp.maximum(i, nl) - nl + ctx_off, 0)),
                  pl.BlockSpec((ko, d), lambda i: (0, 0)),
                  pl.BlockSpec((tm, d), lambda i: (i, 0)),
                  pl.BlockSpec((None, 1, mods.shape[-1]), lambda i: (geom.mod_row(i), 0, 0)),
                  pl.BlockSpec((1, d), lambda i: (0, 0))],
        out_specs=[pl.BlockSpec((tm, d), lambda i: (i, 0)),
                   pl.BlockSpec((tm * SUBLANES, LANES), lambda i: (i, 0))],
        out_shape=[jax.ShapeDtypeStruct((n, d), F32), jax.ShapeDtypeStruct((n * SUBLANES, LANES), F32)],
        compiler_params=_cparams("parallel"),
        name="proj_out",
    )(o_lat, o_ctx, w, xs, mods, g2.reshape(1, d))


ROUTER_ROWS = SUBLANES + N_EXPERTS


def _router_kernel(h_ref, wt_ref, b_ref, tri_ref, ints_ref, flt_ref, cnt_ref, carry_ref):
    step = pl.program_id(0)
    tr = h_ref.shape[0] // SUBLANES

    @pl.when(step == 0)
    def _():
        carry_ref[...] = jnp.zeros(carry_ref.shape, F32)

    logits = lax.dot_general(wt_ref[...], _load_token_tiles(h_ref), (((1,), (1,)), ((), ())),
                             precision=HIGHEST, preferred_element_type=F32) + b_ref[:, 0:1]
    row8 = lax.broadcasted_iota(jnp.int32, (SUBLANES, tr), 0)
    lg = jnp.where(row8 < N_GROUPS, logits[0:SUBLANES], NEG_INF)
    lg_max = jnp.max(lg, axis=0, keepdims=True)
    pg = 1.0 / jnp.sum(jnp.exp(lg - lg_max), axis=0, keepdims=True)
    grp = jnp.min(jnp.where(lg == lg_max, row8, SUBLANES), axis=0, keepdims=True)
    l2 = jnp.zeros((EXPERTS_PER_GROUP, tr), F32)
    for g in range(N_GROUPS):
        lo = SUBLANES + g * EXPERTS_PER_GROUP
        l2 = l2 + jnp.where(grp == g, logits[lo:lo + EXPERTS_PER_GROUP], 0.0)
    l2_max = jnp.max(l2, axis=0, keepdims=True)
    j0 = jnp.min(jnp.where(l2 == l2_max, row8, SUBLANES), axis=0, keepdims=True)
    rest = jnp.where(row8 == j0, NEG_INF, l2)
    r_max = jnp.max(rest, axis=0, keepdims=True)
    j1 = jnp.min(jnp.where(rest == r_max, row8, SUBLANES), axis=0, keepdims=True)
    e1 = jnp.exp(r_max - l2_max)
    inv = 1.0 / (1.0 + e1)
    w0 = pg * inv
    w1 = pg * e1 * inv
    ex0 = grp * EXPERTS_PER_GROUP + j0
    ex1 = grp * EXPERTS_PER_GROUP + j1

    rows = lax.broadcasted_iota(jnp.int32, (N_EXPERTS, tr), 0)
    oh0 = (rows == ex0).astype(F32)
    oh1 = (rows == ex1).astype(F32)
    both = oh0 + oh1
    before = jnp.dot(both.astype(BF16), tri_ref[...], preferred_element_type=F32) + carry_ref[:, 0:1]
    rank0 = jnp.sum(oh0 * before, axis=0, keepdims=True)
    rank1 = jnp.sum(oh1 * before, axis=0, keepdims=True)
    carry_ref[...] = carry_ref[...] + jnp.sum(both, axis=1, keepdims=True)
    cnt_ref[...] = carry_ref[...]

    zi = jnp.zeros((SUBLANES - 4, tr), jnp.int32)
    ints_ref[...] = jnp.concatenate([ex0, ex1, rank0.astype(jnp.int32), rank1.astype(jnp.int32), zi], axis=0)
    flt_ref[...] = jnp.concatenate([w0, w1, jnp.zeros((SUBLANES - 2, tr), F32)], axis=0)


def _router(h3, wt, bias, tri):
    n = h3.shape[0] // SUBLANES
    d = wt.shape[1]
    tr = ROUTER_TILE
    return pl.pallas_call(
        _router_kernel,
        grid=(n // tr,),
        in_specs=[pl.BlockSpec((tr * SUBLANES, LANES), lambda i: (i, 0)),
                  pl.BlockSpec((ROUTER_ROWS, d), lambda i: (0, 0)),
                  pl.BlockSpec((ROUTER_ROWS, LANES), lambda i: (0, 0)),
                  pl.BlockSpec((tr, tr), lambda i: (0, 0))],
        out_specs=[pl.BlockSpec((SUBLANES, tr), lambda i: (0, i)),
                   pl.BlockSpec((SUBLANES, tr), lambda i: (0, i)),
                   pl.BlockSpec((N_EXPERTS, LANES), lambda i: (0, 0))],
        out_shape=[jax.ShapeDtypeStruct((SUBLANES, n), jnp.int32),
                   jax.ShapeDtypeStruct((SUBLANES, n), F32),
                   jax.ShapeDtypeStruct((N_EXPERTS, LANES), F32)],
        scratch_shapes=[pltpu.VMEM((N_EXPERTS, LANES), F32)],
        compiler_params=_cparams("arbitrary"),
        name="router",
    )(h3, wt, bias, tri)


def _dest_kernel(pstart_ref, ints_ref, o_ref):
    ints = ints_ref[...]
    ex = ints[0:2]
    base = jnp.zeros(ex.shape, jnp.int32)
    for e in range(N_EXPERTS):
        base = jnp.where(ex == e, pstart_ref[e], base)
    o_ref[...] = jnp.concatenate([base + ints[2:4], jnp.zeros((SUBLANES - 2, ints.shape[1]), jnp.int32)], axis=0)


def _dest_rows(pstart, ints):
    n = ints.shape[1]
    tn = 2048 if n % 2048 == 0 else ROUTER_TILE
    return pl.pallas_call(
        _dest_kernel,
        grid=(n // tn,),
        in_specs=[pl.BlockSpec(memory_space=pltpu.SMEM),
                  pl.BlockSpec((SUBLANES, tn), lambda i: (0, i))],
        out_specs=pl.BlockSpec((SUBLANES, tn), lambda i: (0, i)),
        out_shape=jax.ShapeDtypeStruct((SUBLANES, n), jnp.int32),
        compiler_params=_cparams("parallel"),
        name="dest_rows",
    )(pstart, ints)


def _row_maps_kernel(pad_ref, d0_ref, d1_ref, out_ref, *, n_tokens):
    step = pl.program_id(0)
    td = d0_ref.shape[0]
    n_spans = pad_ref.shape[0] // 2

    @pl.when(step == 0)
    def _():
        def span(e, carry):
            def fill(r, c):
                row = pad_ref[e] + r
                out_ref[row] = 2 * n_tokens + (row & (MOE_HALF - 1))
                return c
            lax.fori_loop(0, pad_ref[n_spans + e], fill, 0)
            return carry
        lax.fori_loop(0, n_spans, span, 0)

    base = step * td

    def place(r, carry):
        n = base + r
        out_ref[d0_ref[r]] = n
        out_ref[d1_ref[r]] = n_tokens + n
        return carry
    lax.fori_loop(0, td, place, 0, unroll=8)


def _row_maps(pad_spans, dest0, dest1, n_rows):
    n = dest0.shape[0]
    td = 2048 if n % 2048 == 0 else ROUTER_TILE
    smem = functools.partial(pl.BlockSpec, memory_space=pltpu.SMEM)
    return pl.pallas_call(
        functools.partial(_row_maps_kernel, n_tokens=n),
        grid=(n // td,),
        in_specs=[smem(), smem((td,), lambda i: (i,)), smem((td,), lambda i: (i,))],
        out_specs=smem(),
        out_shape=jax.ShapeDtypeStruct((n_rows,), jnp.int32),
        compiler_params=_cparams("arbitrary"),
        name="moe_row_maps",
    )(pad_spans, dest0, dest1)


def _expert_kernel(blk_exp_ref, n_used_ref, map_first, map_half1, map_next, map_prev, map_half0, map_last,
                   h_ref, w13_ref, w2_ref, y_ref, xbuf0, xbuf1, ybuf0, ybuf1, w13b, w2b, gsem, ssem,
                   *, d_expert, n_tokens):
    j = pl.program_id(0)
    n_used = n_used_ref[0]
    xbuf = (xbuf0, xbuf1)
    ybuf = (ybuf0, ybuf1)

    def tile_of(ref, r):
        return ref.at[pl.ds(pl.multiple_of(r * SUBLANES, SUBLANES), SUBLANES)]

    def gather_row(idx_ref, buf, r):
        out_row = idx_ref[r]
        tok = out_row - jnp.where(out_row >= 2 * n_tokens, 2 * n_tokens, jnp.where(out_row >= n_tokens, n_tokens, 0))
        pltpu.make_async_copy(tile_of(h_ref, tok), tile_of(xbuf[buf], r), gsem.at[buf]).start()

    def scatter_row(idx_ref, buf, r):
        pltpu.make_async_copy(tile_of(ybuf[buf], r), tile_of(y_ref, idx_ref[r]), ssem.at[buf]).start()

    def wait_gather(buf):
        pltpu.make_async_copy(h_ref.at[pl.ds(0, MOE_HALF * SUBLANES)], xbuf[buf], gsem.at[buf]).wait()

    def wait_scatter(buf):
        pltpu.make_async_copy(ybuf[buf], y_ref.at[pl.ds(0, MOE_HALF * SUBLANES)], ssem.at[buf]).wait()

    def half_block(buf, gather_idx, scatter_idx):
        for r in range(MOE_HALF):
            gather_row(gather_idx, 1 - buf, r)
            scatter_row(scatter_idx, 1 - buf, r)
        x = _load_token_tiles(xbuf[buf]).astype(BF16)
        gu = jnp.dot(x, w13b[...], preferred_element_type=F32)
        g = gu[:, 0:d_expert]
        u = gu[:, d_expert:2 * d_expert]
        a = g * (1.0 / (1.0 + jnp.exp(-g))) * u
        _store_token_tiles(ybuf[buf], jnp.dot(a.astype(BF16), w2b[...], preferred_element_type=F32))

    @pl.when(j < n_used)
    def _():
        @pl.when(j == 0)
        def _():
            ybuf0[...] = jnp.zeros(ybuf0.shape, F32)
            ybuf1[...] = jnp.zeros(ybuf1.shape, F32)

            def first_rows(r, carry):
                pltpu.make_async_copy(tile_of(ybuf0, r), tile_of(y_ref, 2 * n_tokens + r), ssem.at[0]).start()
                gather_row(map_first, 0, r)
                return carry
            lax.fori_loop(0, MOE_HALF, first_rows, 0)
            wait_scatter(0)

        @pl.when((j == 0) | (blk_exp_ref[j] != blk_exp_ref[jnp.maximum(j - 1, 0)]))
        def _():
            w13b[...] = w13_ref[...].astype(BF16)
            w2b[...] = w2_ref[...].astype(BF16)

        wait_gather(0)

        @pl.when(j > 0)
        def _():
            wait_scatter(0)

        half_block(0, map_half1, map_prev)
        wait_gather(1)
        wait_scatter(1)
        half_block(1, map_next, map_half0)

        @pl.when(j == n_used - 1)
        def _():
            wait_gather(0)
            wait_scatter(0)

            def last_scatter(r, carry):
                scatter_row(map_last, 1, r)
                return carry
            lax.fori_loop(0, MOE_HALF, last_scatter, 0)
            wait_scatter(1)


def _experts(blk_exp, n_used, row_map, h3, w13, w2):
    n = h3.shape[0] // SUBLANES
    d, d_expert = w13.shape[1], w2.shape[1]
    n_steps = row_map.shape[0] // MOE_TILE
    smem = functools.partial(pl.BlockSpec, memory_space=pltpu.SMEM)

    def half_spec(half_of):
        def index(j, be, nu):
            last = 2 * nu[0] - 1
            return (jnp.clip(half_of(jnp.minimum(j, nu[0] - 1), last), 0, last),)
        return smem((MOE_HALF,), index)

    def expert(j, be, nu):
        return (be[jnp.minimum(j, nu[0] - 1)], 0, 0)

    token_tile = (MOE_HALF * SUBLANES, LANES)
    grid_spec = pltpu.PrefetchScalarGridSpec(
        num_scalar_prefetch=2,
        grid=(n_steps,),
        in_specs=[half_spec(lambda j, last: 0),
                  half_spec(lambda j, last: 2 * j + 1),
                  half_spec(lambda j, last: 2 * j + 2),
                  half_spec(lambda j, last: 2 * j - 1),
                  half_spec(lambda j, last: 2 * j),
                  half_spec(lambda j, last: last),
                  pl.BlockSpec(memory_space=pl.ANY),
                  pl.BlockSpec((None, d, 2 * d_expert), expert),
                  pl.BlockSpec((None, d_expert, d), expert)],
        out_specs=pl.BlockSpec(memory_space=pl.ANY),
        scratch_shapes=[pltpu.VMEM(token_tile, F32), pltpu.VMEM(token_tile, F32),
                        pltpu.VMEM(token_tile, F32), pltpu.VMEM(token_tile, F32),
                        pltpu.VMEM((d, 2 * d_expert), BF16), pltpu.VMEM((d_expert, d), BF16),
                        pltpu.SemaphoreType.DMA((2,)), pltpu.SemaphoreType.DMA((2,))],
    )
    return pl.pallas_call(
        functools.partial(_expert_kernel, d_expert=d_expert, n_tokens=n),
        grid_spec=grid_spec,
        out_shape=jax.ShapeDtypeStruct(((2 * n + MOE_HALF) * SUBLANES, LANES), F32),
        compiler_params=_cparams("arbitrary"),
        name="moe_experts",
    )(blk_exp, n_used, row_map, row_map, row_map, row_map, row_map, row_map, h3, w13, w2)


def _combine_kernel(w_ref, x_ref, mod_ref, y0_ref, y1_ref, o_ref, *, d):
    out = w_ref[:, 0:1] * _load_token_tiles(y0_ref) + w_ref[:, 1:2] * _load_token_tiles(y1_ref)
    o_ref[...] = x_ref[...] + mod_ref[:, 5 * d:6 * d] * out


def _combine(w_cols, xs, mods, y_tok, geom, n_out):
    n, d = xs.shape
    tc = TOKEN_TILE
    return pl.pallas_call(
        functools.partial(_combine_kernel, d=d),
        grid=(n_out // tc,),
        in_specs=[pl.BlockSpec((tc, 2), lambda i: (i, 0)),
                  pl.BlockSpec((tc, d), lambda i: (i, 0)),
                  pl.BlockSpec((None, 1, mods.shape[-1]), lambda i: (geom.mod_row(i), 0, 0)),
                  pl.BlockSpec((tc * SUBLANES, LANES), lambda i: (i, 0)),
                  pl.BlockSpec((tc * SUBLANES, LANES), lambda i: (n // tc + i, 0))],
        out_specs=pl.BlockSpec((tc, d), lambda i: (i, 0)),
        out_shape=jax.ShapeDtypeStruct((n_out, d), F32),
        compiler_params=_cparams("parallel"),
        name="moe_combine",
    )(w_cols, xs, mods, y_tok, y_tok)


def _hier_moe(h3, xs, mods, geom, w_r1, b_r1, w_r2, b_r2, w13, w2, n_out):
    n, d = xs.shape
    wt = jnp.zeros((ROUTER_ROWS, d), F32)
    wt = wt.at[0:N_GROUPS].set(w_r1.T)
    wt = wt.at[SUBLANES:].set(jnp.transpose(w_r2, (0, 2, 1)).reshape(N_EXPERTS, d))
    bias = jnp.zeros((ROUTER_ROWS,), F32).at[0:N_GROUPS].set(b_r1).at[SUBLANES:].set(b_r2.reshape(-1))
    bias = jnp.broadcast_to(bias[:, None], (ROUTER_ROWS, LANES))
    tri = jnp.asarray(np.triu(np.ones((ROUTER_TILE, ROUTER_TILE), np.float32), 1), BF16)
    ints, flt, cnt = _router(h3, wt, bias, tri)

    counts = cnt[:, 0].astype(jnp.int32)
    padded = ((counts + MOE_TILE - 1) // MOE_TILE) * MOE_TILE
    pend = jnp.cumsum(padded)
    pstart = pend - padded
    n_blocks = (2 * n) // MOE_TILE + N_EXPERTS
    n_rows = n_blocks * MOE_TILE
    blk_row0 = jnp.arange(n_blocks, dtype=jnp.int32) * MOE_TILE
    blk_exp = jnp.minimum(jnp.sum(pend[None, :] <= blk_row0[:, None], axis=1), N_EXPERTS - 1).astype(jnp.int32)

    n_used = (pend[-1] // MOE_TILE).astype(jnp.int32)
    pad_spans = jnp.concatenate([pstart + counts, pend[-1:], padded - counts, n_rows - pend[-1:]]).astype(jnp.int32)

    dest = _dest_rows(pstart.astype(jnp.int32), ints)
    row_map = _row_maps(pad_spans, dest[0], dest[1], n_rows)
    y_tok = _experts(blk_exp, n_used[None], row_map, h3, w13, w2)
    return _combine(flt[0:2].T, xs, mods, y_tok, geom, n_out)


def _rope_tables(seq, identity_rows):
    ctx_len = identity_rows
    rows = seq // GRID_W
    row = jnp.repeat(jnp.arange(rows, dtype=F32), GRID_W)
    col = jnp.tile(jnp.arange(GRID_W, dtype=F32), rows)
    inv = ROPE_THETA ** (-jnp.arange(ROT_FREQS, dtype=F32) / ROT_FREQS)
    ang_r, ang_c = row[:, None] * inv, col[:, None] * inv
    cos_h = jnp.concatenate([jnp.cos(ang_r)] * 2 + [jnp.cos(ang_c)] * 2, axis=1)
    sin_h = jnp.concatenate([-jnp.sin(ang_r), jnp.sin(ang_r), -jnp.sin(ang_c), jnp.sin(ang_c)], axis=1)
    cos_t = jnp.concatenate([jnp.tile(cos_h, (1, 2)), jnp.ones((ctx_len, LANES), F32)], axis=0)
    sin_t = jnp.concatenate([jnp.tile(sin_h, (1, 2)), jnp.zeros((ctx_len, LANES), F32)], axis=0)
    return cos_t, sin_t


def _pad_heads_cols(w, n_heads):
    d = w.shape[0]
    w3 = w.reshape(d, n_heads, HEAD_DIM)
    return jnp.concatenate([w3, jnp.zeros_like(w3)], axis=2).reshape(d, n_heads * LANES)


def kernel(x, c, ctx, c_ctx, ada_w, ada_b, norm1_g, norm2_g, a_wqkv, a_wo, a_q_norm, a_k_norm, a_lambda_q1, a_lambda_k1, a_lambda_q2, a_lambda_k2, a_subln_g, b_wqkv, b_wo, b_q_norm, b_k_norm, b_sink, f_wo, r_w1, r_b1, r_w2, r_b2, e_w13, e_w2):
    batch, seq, d = x.shape
    ctx_len = ctx.shape[1]
    depth = ada_w.shape[0]
    n = batch * (seq + ctx_len)
    assert seq % TOKEN_TILE == 0 and ctx_len % TOKEN_TILE == 0 and n % ROUTER_TILE == 0
    geom = Geom(batch, seq, ctx_len)

    xs = jnp.concatenate([x.reshape(batch * seq, d), ctx.reshape(batch * ctx_len, d)], axis=0)
    mod_rows = ((batch + 1 + SUBLANES - 1) // SUBLANES) * SUBLANES
    cvec = jnp.zeros((mod_rows, d), F32).at[0:batch].set(c).at[batch].set(c_ctx)
    mods_all = _ada_table(cvec, ada_w, ada_b).reshape(depth, mod_rows, 1, 6 * d)
    cos_t, sin_t = _rope_tables(seq, PROJ_IN_TILE)

    for i in range(depth):
        kind, j = i % N_MIXERS, i // N_MIXERS
        mods = mods_all[i]
        if kind == 0:
            lam_init = 0.8 - 0.6 * math.exp(-0.3 * i)
            nh = d // LANES
            gains = jnp.stack([jnp.tile(a_q_norm[j], 2), jnp.tile(a_k_norm[j], 2)])
            q, k, v1 = _proj_in(xs, mods, norm1_g[i], a_wqkv[j].astype(BF16), gains, cos_t, sin_t, geom,
                                n_q=nh, n_k=nh, n_plain=nh, ones_after_plain=True,
                                q_scale=HEAD_DIM ** -0.5 * math.log2(math.e))
            lam_vecs = jnp.stack([a_lambda_q1[j], a_lambda_k1[j], a_lambda_q2[j], a_lambda_k2[j]])
            o_lat, o_ctx = _diff_attention(q, k, v1, lam_vecs, a_subln_g[j], geom, lam_init)
            wo = a_wo[j].astype(BF16)
        elif kind == 1:
            q_heads = d // HEAD_DIM
            nq, nk = q_heads * HEAD_DIM, SWA_KV_HEADS * HEAD_DIM
            w = b_wqkv[j]
            w_pad = jnp.concatenate([_pad_heads_cols(w[:, :nq], q_heads),
                                     _pad_heads_cols(w[:, nq:nq + nk], SWA_KV_HEADS),
                                     _pad_heads_cols(w[:, nq + nk:], SWA_KV_HEADS)], axis=1).astype(BF16)
            zeros64 = jnp.zeros((HEAD_DIM,), F32)
            gains = jnp.stack([jnp.concatenate([b_q_norm[j], zeros64]), jnp.concatenate([b_k_norm[j], zeros64])])
            q, k, v = _proj_in(xs, mods, norm1_g[i], w_pad, gains, cos_t, sin_t, geom,
                               n_q=q_heads, n_k=SWA_KV_HEADS, n_plain=SWA_KV_HEADS)
            o_lat = o_ctx = _swa_attention(q, k, v, b_sink[j], geom)
            wo3 = b_wo[j].reshape(q_heads, HEAD_DIM, d)
            wo = jnp.concatenate([wo3, jnp.zeros_like(wo3)], axis=1).reshape(q_heads * LANES, d).astype(BF16)
        else:
            gd = d // FOURIER_GROUPS
            cd, msd = _dft_mats(gd, gd ** -0.5)
            eye = jnp.eye(FOURIER_GROUPS, dtype=BF16)
            w_cs = jnp.concatenate([jnp.kron(eye, cd), jnp.kron(eye, -msd)], axis=1)
            (z,) = _proj_in(xs, mods, norm1_g[i], w_cs, jnp.zeros((2, LANES), F32), cos_t, sin_t, geom,
                            n_q=0, n_k=0, n_plain=2 * d // LANES)
            o_lat = o_ctx = _seq_dft(z, _dft_mats(ctx_len, ctx_len ** -0.5), geom, d)
            wo = f_wo[j].astype(BF16)
        xs, h2 = _proj_out(o_lat, o_ctx, wo, xs, mods, norm2_g[i], geom)
        n_out = n if i < depth - 1 else batch * seq
        xs = _hier_moe(h2, xs, mods, geom, r_w1[i], r_b1[i], r_w2[i], r_b2[i], e_w13[i], e_w2[i], n_out)

    return xs.reshape(batch, seq, d)
```

```python
import functools
import math
from typing import NamedTuple

import numpy as np
import jax
import jax.numpy as jnp
from jax import lax
from jax.experimental import pallas as pl
from jax.experimental.pallas import tpu as pltpu

F32 = jnp.float32
BF16 = jnp.bfloat16
HIGHEST = lax.Precision.HIGHEST

GRID_W = 64
HEAD_DIM = 64
ROT_FREQS = HEAD_DIM // 4
ROPE_THETA = 10000.0
WINDOW = 128
N_MIXERS = 3
SWA_KV_HEADS = 4
FOURIER_GROUPS = 4
N_GROUPS = 4
EXPERTS_PER_GROUP = 8
N_EXPERTS = N_GROUPS * EXPERTS_PER_GROUP
EPS = 1e-6
NEG_INF = -1e30

LANES = 128
SUBLANES = 8
MXU_DIM = 256
TOKEN_TILE = 256
PROJ_IN_TILE = 512
MOE_HALF = 256
MOE_TILE = 2 * MOE_HALF
ROUTER_TILE = 512
VMEM_LIMIT = 48 * 1024 * 1024


def _cparams(*sem):
    return pltpu.CompilerParams(dimension_semantics=sem, vmem_limit_bytes=VMEM_LIMIT)


class Geom(NamedTuple):
    batch: int
    seq: int
    ctx_len: int

    @property
    def lat_tiles(self):
        return self.seq // TOKEN_TILE

    @property
    def n_lat_tiles(self):
        return self.batch * self.lat_tiles

    @property
    def n_tiles(self):
        return self.n_lat_tiles + self.batch * (self.ctx_len // TOKEN_TILE)

    def mod_row(self, i):
        return jnp.where(i < self.n_lat_tiles, i // self.lat_tiles, self.batch)

    def ctx_tile(self, b):
        return self.n_lat_tiles + b


def _ada_kernel(c_ref, w_ref, b_ref, o_ref):
    c = c_ref[...]
    s = c * (1.0 / (1.0 + jnp.exp(-c)))
    o_ref[...] = jnp.dot(s, w_ref[...], precision=HIGHEST, preferred_element_type=F32) + b_ref[...]


def _ada_table(cvec, ada_w, ada_b):
    depth, d, n6 = ada_w.shape
    rows = cvec.shape[0]
    tn = n6 // 4
    return pl.pallas_call(
        _ada_kernel,
        grid=(depth, n6 // tn),
        in_specs=[pl.BlockSpec((rows, d), lambda l, j: (0, 0)),
                  pl.BlockSpec((None, d, tn), lambda l, j: (l, 0, j)),
                  pl.BlockSpec((None, 1, tn), lambda l, j: (l, 0, j))],
        out_specs=pl.BlockSpec((None, rows, tn), lambda l, j: (l, 0, j)),
        out_shape=jax.ShapeDtypeStruct((depth, rows, n6), F32),
        compiler_params=_cparams("parallel", "parallel"),
        name="ada_table",
    )(cvec, ada_w, ada_b.reshape(depth, 1, n6))


def _head_mean_matrix():
    r = np.arange(LANES)
    return jnp.asarray((r[:, None] // HEAD_DIM == r[None, :] // HEAD_DIM).astype(np.float32) / HEAD_DIM, BF16)


def _proj_in_kernel(x_ref, mod_ref, g_ref, w_ref, hm_ref, gain_ref, cos_ref, sin_ref, *out_refs,
                    d, n_q, n_k, n_plain, ones_after_plain, q_scale):
    q_ref = out_refs[0] if n_q else None
    k_ref = out_refs[1] if n_k else None
    p_ref = out_refs[-1]
    x = x_ref[...]
    ms = jnp.mean(x * x, axis=-1, keepdims=True)
    h = x * lax.rsqrt(ms + EPS) * g_ref[...]
    h = h * (1.0 + mod_ref[:, d:2 * d]) + mod_ref[:, 0:d]
    hb = h.astype(BF16)
    lane = lax.broadcasted_iota(jnp.int32, (1, LANES), 1)
    first_half = (lane // ROT_FREQS) % 2 == 0
    n_chunks = n_q + n_k + n_plain
    for c2 in range(0, n_chunks, 2):
        width = min(2, n_chunks - c2) * LANES
        y2 = jnp.dot(hb, w_ref[:, c2 * LANES:c2 * LANES + width], preferred_element_type=F32)
        for half in range(width // LANES):
            c = c2 + half
            y = y2[:, half * LANES:(half + 1) * LANES]
            if c < n_q + n_k:
                msq = jnp.dot((y * y).astype(BF16), hm_ref[...], preferred_element_type=F32)
                is_q = c < n_q
                gain = gain_ref[0:1, :] if is_q else gain_ref[1:2, :]
                yn = y * lax.rsqrt(msq + EPS) * gain
                partner = jnp.where(first_half, pltpu.roll(yn, LANES - ROT_FREQS, 1), pltpu.roll(yn, ROT_FREQS, 1))
                out = yn * cos_ref[...] + partner * sin_ref[...]
                if is_q:
                    q_ref[:, c * LANES:(c + 1) * LANES] = (out * q_scale).astype(q_ref.dtype)
                else:
                    ck = c - n_q
                    k_ref[:, ck * LANES:(ck + 1) * LANES] = out.astype(k_ref.dtype)
            else:
                cp = c - n_q - n_k
                if ones_after_plain:
                    p_ref[:, 2 * cp * LANES:(2 * cp + 1) * LANES] = y.astype(p_ref.dtype)
                    p_ref[:, (2 * cp + 1) * LANES:(2 * cp + 2) * LANES] = jnp.ones(y.shape, p_ref.dtype)
                else:
                    p_ref[:, cp * LANES:(cp + 1) * LANES] = y.astype(p_ref.dtype)


Q_SCALE_LOG2 = HEAD_DIM ** -0.5 * math.log2(math.e)


def _proj_in(xs, mods, g, w, gains, cos_t, sin_t, geom, *, n_q, n_k, n_plain, ones_after_plain=False,
             q_scale=Q_SCALE_LOG2):
    n, d = xs.shape
    tm = PROJ_IN_TILE
    lat_tiles = geom.seq // tm
    n_lat = geom.batch * lat_tiles
    assert geom.seq % tm == 0 and n % tm == 0 and cos_t.shape[0] == geom.seq + tm

    def mod_row(i):
        return jnp.where(i < n_lat, i // lat_tiles, geom.batch)

    def rope_row(i):
        return jnp.where(i < n_lat, i % lat_tiles, lat_tiles)

    n_cols = (n_q + n_k + n_plain) * LANES
    assert w.shape == (d, n_cols)
    out_shape, out_specs = [], []
    for cnt in (n_q, n_k):
        if cnt:
            out_shape.append(jax.ShapeDtypeStruct((n, cnt * LANES), BF16))
            out_specs.append(pl.BlockSpec((tm, cnt * LANES), lambda i: (i, 0)))
    pw = n_plain * LANES * (2 if ones_after_plain else 1)
    out_shape.append(jax.ShapeDtypeStruct((n, pw), BF16))
    out_specs.append(pl.BlockSpec((tm, pw), lambda i: (i, 0)))
    kern = functools.partial(_proj_in_kernel, d=d, n_q=n_q, n_k=n_k, n_plain=n_plain,
                             ones_after_plain=ones_after_plain, q_scale=q_scale)
    return pl.pallas_call(
        kern,
        grid=(n // tm,),
        in_specs=[pl.BlockSpec((tm, d), lambda i: (i, 0)),
                  pl.BlockSpec((None, 1, mods.shape[-1]), lambda i: (mod_row(i), 0, 0)),
                  pl.BlockSpec((1, d), lambda i: (0, 0)),
                  pl.BlockSpec((d, n_cols), lambda i: (0, 0)),
                  pl.BlockSpec((LANES, LANES), lambda i: (0, 0)),
                  pl.BlockSpec((2, LANES), lambda i: (0, 0)),
                  pl.BlockSpec((tm, LANES), lambda i: (rope_row(i), 0)),
                  pl.BlockSpec((tm, LANES), lambda i: (rope_row(i), 0))],
        out_specs=out_specs,
        out_shape=out_shape,
        compiler_params=_cparams("parallel"),
        name="proj_in",
    )(xs, mods, g.reshape(1, d), w, _head_mean_matrix(), gains, cos_t, sin_t)


DIFF_KV_CHUNK = 512
DIFF_Q_TILE = 256
DIFF_HEADS_PER_STEP = 2


def _diff_attn_kernel(*refs, lam_init, n_kv, chunks, heads):
    lam_ref, q_ref = refs[0], refs[1]
    k_refs = refs[2:2 + n_kv]
    v_refs = refs[2 + n_kv:2 + 2 * n_kv]
    sg_ref, o_ref, s_ref, m_ref = refs[2 + 2 * n_kv:]
    dq = 2 * HEAD_DIM
    dv = o_ref.shape[1] // heads
    in_grid = pl.program_id(0) < pl.num_programs(0)

    @pl.when(in_grid)
    def _():
        for h in range(heads):
            q = q_ref[:, h * dq:(h + 1) * dq]
            lane = lax.broadcasted_iota(jnp.int32, q.shape, 1)
            zero = jnp.zeros_like(q)
            q_maps = (jnp.where(lane < HEAD_DIM, q, zero), jnp.where(lane >= HEAD_DIM, q, zero))
            for m in range(2):
                mx = None
                for slab, start, size, col in chunks:
                    s = lax.dot_general(q_maps[m], k_refs[slab][start:start + size, h * dq:(h + 1) * dq],
                                        (((1,), (1,)), ((), ())), preferred_element_type=F32)
                    s_ref[2 * h + m, :, col:col + size] = s
                    cm = jnp.max(s, axis=1, keepdims=True)
                    mx = cm if mx is None else jnp.maximum(mx, cm)
                m_ref[2 * h + m] = mx

    @pl.when(in_grid)
    def _():
        lam_v = lam_ref[...]
        lam = (jnp.exp(jnp.sum(lam_v[0:1] * lam_v[1:2], axis=1, keepdims=True))
               - jnp.exp(jnp.sum(lam_v[2:3] * lam_v[3:4], axis=1, keepdims=True)) + lam_init)
        for h in range(heads):
            outs = []
            for m in range(2):
                row_max = m_ref[2 * h + m]
                acc = None
                for slab, start, size, col in chunks:
                    p = jnp.exp2(s_ref[2 * h + m, :, col:col + size] - row_max)
                    pv = jnp.dot(p.astype(BF16), v_refs[slab][start:start + size, 2 * h * dv:2 * (h + 1) * dv],
                                 preferred_element_type=F32)
                    acc = pv if acc is None else acc + pv
                outs.append(acc[:, 0:dv] * (1.0 / acc[:, dv:dv + 1]))
            a = outs[0] - lam * outs[1]
            a = a * lax.rsqrt(jnp.mean(a * a, axis=-1, keepdims=True) + EPS) * sg_ref[...] * (1.0 - lam_init)
            o_ref[:, h * dv:(h + 1) * dv] = a.astype(o_ref.dtype)


def _diff_attention(q, k, v1, lam_vecs, sub_g, geom, lam_init):
    dq = q.shape[1]
    batch, seq, ctx_len = geom
    heads = dq // (2 * HEAD_DIM)
    dv = 2 * HEAD_DIM
    tq = DIFF_Q_TILE
    q_tiles = seq // tq
    ctx_blk = seq // ctx_len * batch
    lam_spec = pl.BlockSpec((4, HEAD_DIM), lambda *_: (0, 0))
    sg_spec = pl.BlockSpec((1, dv), lambda *_: (0, 0))
    chunks = [(0, c * DIFF_KV_CHUNK, DIFF_KV_CHUNK, c * DIFF_KV_CHUNK) for c in range(seq // DIFF_KV_CHUNK)]
    chunks.append((1, 0, ctx_len, seq))
    hp = DIFF_HEADS_PER_STEP
    assert heads % hp == 0
    o_lat = pl.pallas_call(
        functools.partial(_diff_attn_kernel, lam_init=lam_init, n_kv=2, chunks=chunks, heads=hp),
        grid=(batch, heads // hp, q_tiles),
        in_specs=[lam_spec,
                  pl.BlockSpec((tq, hp * 2 * HEAD_DIM), lambda b, h, i: (b * q_tiles + i, h)),
                  pl.BlockSpec((seq, hp * 2 * HEAD_DIM), lambda b, h, i: (b, h)),
                  pl.BlockSpec((ctx_len, hp * 2 * HEAD_DIM), lambda b, h, i: (ctx_blk + b, h)),
                  pl.BlockSpec((seq, hp * 2 * dv), lambda b, h, i: (b, h)),
                  pl.BlockSpec((ctx_len, hp * 2 * dv), lambda b, h, i: (ctx_blk + b, h)),
                  sg_spec],
        out_specs=pl.BlockSpec((tq, hp * dv), lambda b, h, i: (b * q_tiles + i, h)),
        out_shape=jax.ShapeDtypeStruct((batch * seq, heads * dv), BF16),
        scratch_shapes=[pltpu.VMEM((2 * hp, tq, seq + ctx_len), F32), pltpu.VMEM((2 * hp, tq, 1), F32)],
        compiler_params=_cparams("parallel", "parallel", "arbitrary"),
        name="diff_attention",
    )(lam_vecs, q, k, k, v1, v1, sub_g.reshape(1, dv))
    o_ctx = pl.pallas_call(
        functools.partial(_diff_attn_kernel, lam_init=lam_init, n_kv=1, chunks=[(0, 0, ctx_len, 0)], heads=1),
        grid=(batch, heads),
        in_specs=[lam_spec,
                  pl.BlockSpec((ctx_len, 2 * HEAD_DIM), lambda b, h: (ctx_blk + b, h)),
                  pl.BlockSpec((ctx_len, 2 * HEAD_DIM), lambda b, h: (ctx_blk + b, h)),
                  pl.BlockSpec((ctx_len, 2 * dv), lambda b, h: (ctx_blk + b, h)),
                  sg_spec],
        out_specs=pl.BlockSpec((ctx_len, dv), lambda b, h: (b, h)),
        out_shape=jax.ShapeDtypeStruct((batch * ctx_len, heads * dv), BF16),
        scratch_shapes=[pltpu.VMEM((2, ctx_len, ctx_len), F32), pltpu.VMEM((2, ctx_len, 1), F32)],
        compiler_params=_cparams("parallel", "parallel"),
        name="diff_attention_ctx",
    )(lam_vecs, q, k, v1, sub_g.reshape(1, dv))
    return o_lat, o_ctx


def _swa_kernel(sink_ref, q_ref, kl_ref, kc_ref, vl_ref, vc_ref, o_ref, *, latent_tiles, group):
    i = pl.program_id(1)
    tq = q_ref.shape[0]
    seq, ctx_len = kl_ref.shape[0], kc_ref.shape[0]
    span = tq + 2 * WINDOW
    start = pl.multiple_of(jnp.clip(i * tq - WINDOW, 0, seq - span), WINDOW)
    qpos = i * tq + lax.broadcasted_iota(jnp.int32, (tq, span), 0)
    kpos = start + lax.broadcasted_iota(jnp.int32, (tq, span), 1)
    valid = (jnp.abs(kpos - qpos) <= WINDOW) & (i < latent_tiles)
    bias = jnp.where(valid, 0.0, NEG_INF).astype(F32)
    nt = (((1,), (1,)), ((), ()))
    for h in range(SWA_KV_HEADS):
        hs = slice(h * LANES, (h + 1) * LANES)
        kw = kl_ref[pl.ds(start, span), hs]
        vw = vl_ref[pl.ds(start, span), hs]
        kc = kc_ref[:, hs]
        vc = vc_ref[:, hs]
        for g in range(group):
            hq = h * group + g
            q = q_ref[:, hq * LANES:(hq + 1) * LANES]
            s_w = lax.dot_general(q, kw, nt, preferred_element_type=F32) + bias
            s_c = lax.dot_general(q, kc, nt, preferred_element_type=F32)
            sink = sink_ref[hq] * math.log2(math.e)
            m = jnp.maximum(jnp.maximum(jnp.max(s_w, axis=1, keepdims=True), jnp.max(s_c, axis=1, keepdims=True)),
                            sink)
            p_w = jnp.exp2(s_w - m)
            p_c = jnp.exp2(s_c - m)
            denom = (jnp.sum(p_w, axis=1, keepdims=True) + jnp.sum(p_c, axis=1, keepdims=True)
                     + jnp.exp2(sink - m))
            o = (jnp.dot(p_w.astype(BF16), vw, preferred_element_type=F32)
                 + jnp.dot(p_c.astype(BF16), vc, preferred_element_type=F32)) * (1.0 / denom)
            o_ref[:, hq * LANES:(hq + 1) * LANES] = o.astype(o_ref.dtype)


def _swa_attention(q, k, v, sink, geom):
    n, dq = q.shape
    batch, seq, ctx_len = geom
    assert ctx_len == TOKEN_TILE
    lt = geom.lat_tiles
    q_heads = dq // LANES
    kvw = SWA_KV_HEADS * LANES
    ctx_blk = seq // ctx_len * batch
    kern = functools.partial(_swa_kernel, latent_tiles=lt, group=q_heads // SWA_KV_HEADS)

    def tile(b, i):
        return jnp.where(i < lt, b * lt + i, geom.ctx_tile(b))

    lat_kv = pl.BlockSpec((seq, kvw), lambda b, i: (b, 0))
    ctx_kv = pl.BlockSpec((ctx_len, kvw), lambda b, i: (ctx_blk + b, 0))
    return pl.pallas_call(
        kern,
        grid=(batch, lt + 1),
        in_specs=[pl.BlockSpec(memory_space=pltpu.SMEM),
                  pl.BlockSpec((TOKEN_TILE, dq), lambda b, i: (tile(b, i), 0)),
                  lat_kv, ctx_kv, lat_kv, ctx_kv],
        out_specs=pl.BlockSpec((TOKEN_TILE, dq), lambda b, i: (tile(b, i), 0)),
        out_shape=jax.ShapeDtypeStruct((n, dq), BF16),
        compiler_params=_cparams("parallel", "arbitrary"),
        name="swa_attention",
    )(sink, q, k, k, v, v)


def _seq_dft_kernel(base_ref, fine_c_ref, fine_s_ref, cc_ref, cs_ref, zc_ref, zs_ref, yc_ref, ys_ref, o_ref,
                    *, latent_tiles):
    m = pl.program_id(2)

    def mix(a_cos, a_msin, z_cos, z_sin):
        acc = jnp.dot(a_cos, z_cos[...], preferred_element_type=F32)
        acc = acc + jnp.dot(a_msin, z_sin[...], preferred_element_type=F32)
        o_ref[...] = acc.astype(o_ref.dtype)

    @pl.when(m < latent_tiles)
    def _():
        c1, s1 = base_ref[0:1, :], base_ref[1:2, :]
        c2, s2 = fine_c_ref[...], fine_s_ref[...]
        a_cos = (c2 * c1 - s2 * s1).astype(BF16)
        a_msin = (-(c2 * s1) - s2 * c1).astype(BF16)
        mix(a_cos, a_msin, zc_ref, zs_ref)

    @pl.when(m >= latent_tiles)
    def _():
        mix(cc_ref[...], cs_ref[...], yc_ref, ys_ref)


def _dft_angle_tables(n, rows, scale):
    k = jnp.arange(n, dtype=jnp.int32)[None, :]
    step = 2.0 * math.pi / n
    ang0 = (((jnp.arange(n // rows, dtype=jnp.int32) * rows)[:, None] * k) % n).astype(F32) * step
    ang1 = ((jnp.arange(rows, dtype=jnp.int32)[:, None] * k) % n).astype(F32) * step
    base = jnp.stack([jnp.cos(ang0), jnp.sin(ang0)], axis=1)
    return base, jnp.cos(ang1) * scale, jnp.sin(ang1) * scale


def _seq_dft(z, ctx_mats, geom, d):
    batch, seq, ctx_len = geom
    assert ctx_len == TOKEN_TILE and seq % ctx_len == 0
    lt = geom.lat_tiles
    tm = TOKEN_TILE
    tn = d // 2
    n_col = d // tn
    ctx_blk = seq // ctx_len * batch
    lat_tables = _dft_angle_tables(seq, tm, seq ** -0.5)
    fine = pl.BlockSpec((tm, seq), lambda b, j, m: (0, 0))
    ctx_a = pl.BlockSpec((ctx_len, ctx_len), lambda b, j, m: (0, 0))
    return pl.pallas_call(
        functools.partial(_seq_dft_kernel, latent_tiles=lt),
        grid=(batch, n_col, lt + 1),
        in_specs=[pl.BlockSpec((None, 2, seq), lambda b, j, m: (jnp.minimum(m, lt - 1), 0, 0)),
                  fine, fine, ctx_a, ctx_a,
                  pl.BlockSpec((seq, tn), lambda b, j, m: (b, j)),
                  pl.BlockSpec((seq, tn), lambda b, j, m: (b, n_col + j)),
                  pl.BlockSpec((ctx_len, tn), lambda b, j, m: (ctx_blk + b, j)),
                  pl.BlockSpec((ctx_len, tn), lambda b, j, m: (ctx_blk + b, n_col + j))],
        out_specs=pl.BlockSpec((tm, tn), lambda b, j, m: (jnp.where(m < lt, b * lt + m, geom.ctx_tile(b)), j)),
        out_shape=jax.ShapeDtypeStruct((z.shape[0], d), BF16),
        compiler_params=_cparams("parallel", "parallel", "arbitrary"),
        name="seq_dft",
    )(*lat_tables, *ctx_mats, z, z, z, z)


def _dft_mats(n, scale):
    idx = (jnp.arange(n, dtype=jnp.int32)[:, None] * jnp.arange(n, dtype=jnp.int32)[None, :]) % n
    ang = idx.astype(F32) * (2.0 * math.pi / n)
    return (jnp.cos(ang) * scale).astype(BF16), (-jnp.sin(ang) * scale).astype(BF16)


def _load_token_tiles(ref):
    rows = ref.shape[0] // SUBLANES
    return jnp.concatenate([ref[pl.ds(s, rows, stride=SUBLANES), :] for s in range(SUBLANES)], axis=1)


def _store_token_tiles(ref, val):
    rows = ref.shape[0] // SUBLANES
    for s in range(SUBLANES):
        ref[pl.ds(s, rows, stride=SUBLANES), :] = val[:, s * LANES:(s + 1) * LANES]


def _proj_out_kernel(ol_ref, oc_ref, w_ref, x_ref, mod_ref, g_ref, xo_ref, h_ref, *, d, n_lat_tiles):
    def finish(o_ref):
        y = jnp.dot(o_ref[...], w_ref[...], preferred_element_type=F32)
        x = x_ref[...] + mod_ref[:, 2 * d:3 * d] * y
        xo_ref[...] = x
        ms = jnp.mean(x * x, axis=-1, keepdims=True)
        h = x * lax.rsqrt(ms + EPS) * g_ref[...]
        _store_token_tiles(h_ref, h * (1.0 + mod_ref[:, 4 * d:5 * d]) + mod_ref[:, 3 * d:4 * d])

    @pl.when(pl.program_id(0) < n_lat_tiles)
    def _():
        finish(ol_ref)

    @pl.when(pl.program_id(0) >= n_lat_tiles)
    def _():
        finish(oc_ref)


def _proj_out(o_lat, o_ctx, w, xs, mods, g2, geom):
    n, d = xs.shape
    ko = o_lat.shape[1]
    tm = TOKEN_TILE
    nl = geom.n_lat_tiles
    ctx_off = nl if o_ctx.shape[0] == n else 0
    return pl.pallas_call(
        functools.partial(_proj_out_kernel, d=d, n_lat_tiles=nl),
        grid=(n // tm,),
        in_specs=[pl.BlockSpec((tm, ko), lambda i: (jnp.minimum(i, nl - 1), 0)),
                  pl.BlockSpec((tm, ko), lambda i: (jnp.maximum(i, nl) - nl + ctx_off, 0)),
                  pl.BlockSpec((ko, d), lambda i: (0, 0)),
                  pl.BlockSpec((tm, d), lambda i: (i, 0)),
                  pl.BlockSpec((None, 1, mods.shape[-1]), lambda i: (geom.mod_row(i), 0, 0)),
                  pl.BlockSpec((1, d), lambda i: (0, 0))],
        out_specs=[pl.BlockSpec((tm, d), lambda i: (i, 0)),
                   pl.BlockSpec((tm * SUBLANES, LANES), lambda i: (i, 0))],
        out_shape=[jax.ShapeDtypeStruct((n, d), F32), jax.ShapeDtypeStruct((n * SUBLANES, LANES), F32)],
        compiler_params=_cparams("parallel"),
        name="proj_out",
    )(o_lat, o_ctx, w, xs, mods, g2.reshape(1, d))


ROUTER_ROWS = SUBLANES + N_EXPERTS


def _router_kernel(h_ref, wt_ref, b_ref, tri_ref, ints_ref, flt_ref, cnt_ref, carry_ref):
    step = pl.program_id(0)
    tr = h_ref.shape[0] // SUBLANES

    @pl.when(step == 0)
    def _():
        carry_ref[...] = jnp.zeros(carry_ref.shape, F32)

    logits = lax.dot_general(wt_ref[...], _load_token_tiles(h_ref), (((1,), (1,)), ((), ())),
                             precision=HIGHEST, preferred_element_type=F32) + b_ref[:, 0:1]
    row8 = lax.broadcasted_iota(jnp.int32, (SUBLANES, tr), 0)
    lg = jnp.where(row8 < N_GROUPS, logits[0:SUBLANES], NEG_INF)
    lg_max = jnp.max(lg, axis=0, keepdims=True)
    pg = 1.0 / jnp.sum(jnp.exp(lg - lg_max), axis=0, keepdims=True)
    grp = jnp.min(jnp.where(lg == lg_max, row8, SUBLANES), axis=0, keepdims=True)
    l2 = jnp.zeros((EXPERTS_PER_GROUP, tr), F32)
    for g in range(N_GROUPS):
        lo = SUBLANES + g * EXPERTS_PER_GROUP
        l2 = l2 + jnp.where(grp == g, logits[lo:lo + EXPERTS_PER_GROUP], 0.0)
    l2_max = jnp.max(l2, axis=0, keepdims=True)
    j0 = jnp.min(jnp.where(l2 == l2_max, row8, SUBLANES), axis=0, keepdims=True)
    rest = jnp.where(row8 == j0, NEG_INF, l2)
    r_max = jnp.max(rest, axis=0, keepdims=True)
    j1 = jnp.min(jnp.where(rest == r_max, row8, SUBLANES), axis=0, keepdims=True)
    e1 = jnp.exp(r_max - l2_max)
    inv = 1.0 / (1.0 + e1)
    w0 = pg * inv
    w1 = pg * e1 * inv
    ex0 = grp * EXPERTS_PER_GROUP + j0
    ex1 = grp * EXPERTS_PER_GROUP + j1

    rows = lax.broadcasted_iota(jnp.int32, (N_EXPERTS, tr), 0)
    oh0 = (rows == ex0).astype(F32)
    oh1 = (rows == ex1).astype(F32)
    both = oh0 + oh1
    before = jnp.dot(both.astype(BF16), tri_ref[...], preferred_element_type=F32) + carry_ref[:, 0:1]
    rank0 = jnp.sum(oh0 * before, axis=0, keepdims=True)
    rank1 = jnp.sum(oh1 * before, axis=0, keepdims=True)
    carry_ref[...] = carry_ref[...] + jnp.sum(both, axis=1, keepdims=True)
    cnt_ref[...] = carry_ref[...]

    zi = jnp.zeros((SUBLANES - 4, tr), jnp.int32)
    ints_ref[...] = jnp.concatenate([ex0, ex1, rank0.astype(jnp.int32), rank1.astype(jnp.int32), zi], axis=0)
    flt_ref[...] = jnp.concatenate([w0, w1, jnp.zeros((SUBLANES - 2, tr), F32)], axis=0)


def _router(h3, wt, bias, tri):
    n = h3.shape[0] // SUBLANES
    d = wt.shape[1]
    tr = ROUTER_TILE
    return pl.pallas_call(
        _router_kernel,
        grid=(n // tr,),
        in_specs=[pl.BlockSpec((tr * SUBLANES, LANES), lambda i: (i, 0)),
                  pl.BlockSpec((ROUTER_ROWS, d), lambda i: (0, 0)),
                  pl.BlockSpec((ROUTER_ROWS, LANES), lambda i: (0, 0)),
                  pl.BlockSpec((tr, tr), lambda i: (0, 0))],
        out_specs=[pl.BlockSpec((SUBLANES, tr), lambda i: (0, i)),
                   pl.BlockSpec((SUBLANES, tr), lambda i: (0, i)),
                   pl.BlockSpec((N_EXPERTS, LANES), lambda i: (0, 0))],
        out_shape=[jax.ShapeDtypeStruct((SUBLANES, n), jnp.int32),
                   jax.ShapeDtypeStruct((SUBLANES, n), F32),
                   jax.ShapeDtypeStruct((N_EXPERTS, LANES), F32)],
        scratch_shapes=[pltpu.VMEM((N_EXPERTS, LANES), F32)],
        compiler_params=_cparams("arbitrary"),
        name="router",
    )(h3, wt, bias, tri)


def _dest_kernel(pstart_ref, ints_ref, o_ref):
    ints = ints_ref[...]
    ex = ints[0:2]
    base = jnp.zeros(ex.shape, jnp.int32)
    for e in range(N_EXPERTS):
        base = jnp.where(ex == e, pstart_ref[e], base)
    o_ref[...] = jnp.concatenate([base + ints[2:4], jnp.zeros((SUBLANES - 2, ints.shape[1]), jnp.int32)], axis=0)


def _dest_rows(pstart, ints):
    n = ints.shape[1]
    tn = 2048 if n % 2048 == 0 else ROUTER_TILE
    return pl.pallas_call(
        _dest_kernel,
        grid=(n // tn,),
        in_specs=[pl.BlockSpec(memory_space=pltpu.SMEM),
                  pl.BlockSpec((SUBLANES, tn), lambda i: (0, i))],
        out_specs=pl.BlockSpec((SUBLANES, tn), lambda i: (0, i)),
        out_shape=jax.ShapeDtypeStruct((SUBLANES, n), jnp.int32),
        compiler_params=_cparams("parallel"),
        name="dest_rows",
    )(pstart, ints)


def _row_maps_kernel(pad_ref, d0_ref, d1_ref, out_ref, *, n_tokens):
    step = pl.program_id(0)
    td = d0_ref.shape[0]
    n_spans = pad_ref.shape[0] // 2

    @pl.when(step == 0)
    def _():
        def span(e, carry):
            def fill(r, c):
                row = pad_ref[e] + r
                out_ref[row] = 2 * n_tokens + (row & (MOE_HALF - 1))
                return c
            lax.fori_loop(0, pad_ref[n_spans + e], fill, 0)
            return carry
        lax.fori_loop(0, n_spans, span, 0)

    base = step * td

    def place(r, carry):
        n = base + r
        out_ref[d0_ref[r]] = n
        out_ref[d1_ref[r]] = n_tokens + n
        return carry
    lax.fori_loop(0, td, place, 0, unroll=8)


def _row_maps(pad_spans, dest0, dest1, n_rows):
    n = dest0.shape[0]
    td = 2048 if n % 2048 == 0 else ROUTER_TILE
    smem = functools.partial(pl.BlockSpec, memory_space=pltpu.SMEM)
    return pl.pallas_call(
        functools.partial(_row_maps_kernel, n_tokens=n),
        grid=(n // td,),
        in_specs=[smem(), smem((td,), lambda i: (i,)), smem((td,), lambda i: (i,))],
        out_specs=smem(),
        out_shape=jax.ShapeDtypeStruct((n_rows,), jnp.int32),
        compiler_params=_cparams("arbitrary"),
        name="moe_row_maps",
    )(pad_spans, dest0, dest1)


def _expert_kernel(blk_exp_ref, n_used_ref, map_first, map_half1, map_next, map_prev, map_half0, map_last,
                   h_ref, w13_ref, w2_ref, y_ref, xbuf0, xbuf1, ybuf0, ybuf1, w13b, w2b, gsem, ssem,
                   *, d_expert, n_tokens):
    j = pl.program_id(0)
    n_used = n_used_ref[0]
    xbuf = (xbuf0, xbuf1)
    ybuf = (ybuf0, ybuf1)

    def tile_of(ref, r):
        return ref.at[pl.ds(pl.multiple_of(r * SUBLANES, SUBLANES), SUBLANES)]

    def gather_row(idx_ref, buf, r):
        out_row = idx_ref[r]
        tok = out_row - jnp.where(out_row >= 2 * n_tokens, 2 * n_tokens, jnp.where(out_row >= n_tokens, n_tokens, 0))
        pltpu.make_async_copy(tile_of(h_ref, tok), tile_of(xbuf[buf], r), gsem.at[buf]).start()

    def scatter_row(idx_ref, buf, r):
        pltpu.make_async_copy(tile_of(ybuf[buf], r), tile_of(y_ref, idx_ref[r]), ssem.at[buf]).start()

    def wait_gather(buf):
        pltpu.make_async_copy(h_ref.at[pl.ds(0, MOE_HALF * SUBLANES)], xbuf[buf], gsem.at[buf]).wait()

    def wait_scatter(buf):
        pltpu.make_async_copy(ybuf[buf], y_ref.at[pl.ds(0, MOE_HALF * SUBLANES)], ssem.at[buf]).wait()

    def half_block(buf, gather_idx, scatter_idx):
        for r in range(MOE_HALF):
            gather_row(gather_idx, 1 - buf, r)
            scatter_row(scatter_idx, 1 - buf, r)
        x = _load_token_tiles(xbuf[buf]).astype(BF16)
        gu = jnp.dot(x, w13b[...], preferred_element_type=F32)
        g = gu[:, 0:d_expert]
        u = gu[:, d_expert:2 * d_expert]
        a = g * (1.0 / (1.0 + jnp.exp(-g))) * u
        _store_token_tiles(ybuf[buf], jnp.dot(a.astype(BF16), w2b[...], preferred_element_type=F32))

    @pl.when(j < n_used)
    def _():
        @pl.when(j == 0)
        def _():
            ybuf0[...] = jnp.zeros(ybuf0.shape, F32)
            ybuf1[...] = jnp.zeros(ybuf1.shape, F32)

            def first_rows(r, carry):
                pltpu.make_async_copy(tile_of(ybuf0, r), tile_of(y_ref, 2 * n_tokens + r), ssem.at[0]).start()
                gather_row(map_first, 0, r)
                return carry
            lax.fori_loop(0, MOE_HALF, first_rows, 0)
            wait_scatter(0)

        @pl.when((j == 0) | (blk_exp_ref[j] != blk_exp_ref[jnp.maximum(j - 1, 0)]))
        def _():
            w13b[...] = w13_ref[...].astype(BF16)
            w2b[...] = w2_ref[...].astype(BF16)

        wait_gather(0)

        @pl.when(j > 0)
        def _():
            wait_scatter(0)

        half_block(0, map_half1, map_prev)
        wait_gather(1)
        wait_scatter(1)
        half_block(1, map_next, map_half0)

        @pl.when(j == n_used - 1)
        def _():
            wait_gather(0)
            wait_scatter(0)

            def last_scatter(r, carry):
                scatter_row(map_last, 1, r)
                return carry
            lax.fori_loop(0, MOE_HALF, last_scatter, 0)
            wait_scatter(1)


def _experts(blk_exp, n_used, row_map, h3, w13_all, w2_all, layer):
    n = h3.shape[0] // SUBLANES
    d, d_expert = w13_all.shape[2], w2_all.shape[2]
    n_steps = row_map.shape[0] // MOE_TILE
    smem = functools.partial(pl.BlockSpec, memory_space=pltpu.SMEM)

    def half_spec(half_of):
        def index(j, be, nu):
            last = 2 * nu[0] - 1
            return (jnp.clip(half_of(jnp.minimum(j, nu[0] - 1), last), 0, last),)
        return smem((MOE_HALF,), index)

    def expert(j, be, nu):
        return (layer, be[jnp.minimum(j, nu[0] - 1)], 0, 0)

    token_tile = (MOE_HALF * SUBLANES, LANES)
    grid_spec = pltpu.PrefetchScalarGridSpec(
        num_scalar_prefetch=2,
        grid=(n_steps,),
        in_specs=[half_spec(lambda j, last: 0),
                  half_spec(lambda j, last: 2 * j + 1),
                  half_spec(lambda j, last: 2 * j + 2),
                  half_spec(lambda j, last: 2 * j - 1),
                  half_spec(lambda j, last: 2 * j),
                  half_spec(lambda j, last: last),
                  pl.BlockSpec(memory_space=pl.ANY),
                  pl.BlockSpec((None, None, d, 2 * d_expert), expert),
                  pl.BlockSpec((None, None, d_expert, d), expert)],
        out_specs=pl.BlockSpec(memory_space=pl.ANY),
        scratch_shapes=[pltpu.VMEM(token_tile, F32), pltpu.VMEM(token_tile, F32),
                        pltpu.VMEM(token_tile, F32), pltpu.VMEM(token_tile, F32),
                        pltpu.VMEM((d, 2 * d_expert), BF16), pltpu.VMEM((d_expert, d), BF16),
                        pltpu.SemaphoreType.DMA((2,)), pltpu.SemaphoreType.DMA((2,))],
    )
    return pl.pallas_call(
        functools.partial(_expert_kernel, d_expert=d_expert, n_tokens=n),
        grid_spec=grid_spec,
        out_shape=jax.ShapeDtypeStruct(((2 * n + MOE_HALF) * SUBLANES, LANES), F32),
        compiler_params=_cparams("arbitrary"),
        name="moe_experts",
    )(blk_exp, n_used, row_map, row_map, row_map, row_map, row_map, row_map, h3, w13_all, w2_all)


def _combine_kernel(w_ref, x_ref, mod_ref, y0_ref, y1_ref, o_ref, *, d):
    out = w_ref[:, 0:1] * _load_token_tiles(y0_ref) + w_ref[:, 1:2] * _load_token_tiles(y1_ref)
    o_ref[...] = x_ref[...] + mod_ref[:, 5 * d:6 * d] * out


def _combine(w_cols, xs, mods, y_tok, geom, n_out):
    n, d = xs.shape
    tc = TOKEN_TILE
    return pl.pallas_call(
        functools.partial(_combine_kernel, d=d),
        grid=(n_out // tc,),
        in_specs=[pl.BlockSpec((tc, 2), lambda i: (i, 0)),
                  pl.BlockSpec((tc, d), lambda i: (i, 0)),
                  pl.BlockSpec((None, 1, mods.shape[-1]), lambda i: (geom.mod_row(i), 0, 0)),
                  pl.BlockSpec((tc * SUBLANES, LANES), lambda i: (i, 0)),
                  pl.BlockSpec((tc * SUBLANES, LANES), lambda i: (n // tc + i, 0))],
        out_specs=pl.BlockSpec((tc, d), lambda i: (i, 0)),
        out_shape=jax.ShapeDtypeStruct((n_out, d), F32),
        compiler_params=_cparams("parallel"),
        name="moe_combine",
    )(w_cols, xs, mods, y_tok, y_tok)


def _hier_moe(h3, xs, mods, geom, w_r1, b_r1, w_r2, b_r2, w13_all, w2_all, layer, n_out):
    n, d = xs.shape
    wt = jnp.zeros((ROUTER_ROWS, d), F32)
    wt = wt.at[0:N_GROUPS].set(w_r1.T)
    wt = wt.at[SUBLANES:].set(jnp.transpose(w_r2, (0, 2, 1)).reshape(N_EXPERTS, d))
    bias = jnp.zeros((ROUTER_ROWS,), F32).at[0:N_GROUPS].set(b_r1).at[SUBLANES:].set(b_r2.reshape(-1))
    bias = jnp.broadcast_to(bias[:, None], (ROUTER_ROWS, LANES))
    tri = jnp.asarray(np.triu(np.ones((ROUTER_TILE, ROUTER_TILE), np.float32), 1), BF16)
    ints, flt, cnt = _router(h3, wt, bias, tri)

    counts = cnt[:, 0].astype(jnp.int32)
    padded = ((counts + MOE_TILE - 1) // MOE_TILE) * MOE_TILE
    pend = jnp.cumsum(padded)
    pstart = pend - padded
    n_blocks = (2 * n) // MOE_TILE + N_EXPERTS
    n_rows = n_blocks * MOE_TILE
    blk_row0 = jnp.arange(n_blocks, dtype=jnp.int32) * MOE_TILE
    blk_exp = jnp.minimum(jnp.sum(pend[None, :] <= blk_row0[:, None], axis=1), N_EXPERTS - 1).astype(jnp.int32)

    n_used = (pend[-1] // MOE_TILE).astype(jnp.int32)
    pad_spans = jnp.concatenate([pstart + counts, pend[-1:], padded - counts, n_rows - pend[-1:]]).astype(jnp.int32)

    dest = _dest_rows(pstart.astype(jnp.int32), ints)
    row_map = _row_maps(pad_spans, dest[0], dest[1], n_rows)
    y_tok = _experts(blk_exp, n_used[None], row_map, h3, w13_all, w2_all, layer)
    return _combine(flt[0:2].T, xs, mods, y_tok, geom, n_out)


def _rope_tables(seq, identity_rows):
    ctx_len = identity_rows
    rows = seq // GRID_W
    row = jnp.repeat(jnp.arange(rows, dtype=F32), GRID_W)
    col = jnp.tile(jnp.arange(GRID_W, dtype=F32), rows)
    inv = ROPE_THETA ** (-jnp.arange(ROT_FREQS, dtype=F32) / ROT_FREQS)
    ang_r, ang_c = row[:, None] * inv, col[:, None] * inv
    cos_h = jnp.concatenate([jnp.cos(ang_r)] * 2 + [jnp.cos(ang_c)] * 2, axis=1)
    sin_h = jnp.concatenate([-jnp.sin(ang_r), jnp.sin(ang_r), -jnp.sin(ang_c), jnp.sin(ang_c)], axis=1)
    cos_t = jnp.concatenate([jnp.tile(cos_h, (1, 2)), jnp.ones((ctx_len, LANES), F32)], axis=0)
    sin_t = jnp.concatenate([jnp.tile(sin_h, (1, 2)), jnp.zeros((ctx_len, LANES), F32)], axis=0)
    return cos_t, sin_t


def _pad_heads_cols(w, n_heads):
    d = w.shape[0]
    w3 = w.reshape(d, n_heads, HEAD_DIM)
    return jnp.concatenate([w3, jnp.zeros_like(w3)], axis=2).reshape(d, n_heads * LANES)


def kernel(x, c, ctx, c_ctx, ada_w, ada_b, norm1_g, norm2_g, a_wqkv, a_wo, a_q_norm, a_k_norm, a_lambda_q1, a_lambda_k1, a_lambda_q2, a_lambda_k2, a_subln_g, b_wqkv, b_wo, b_q_norm, b_k_norm, b_sink, f_wo, r_w1, r_b1, r_w2, r_b2, e_w13, e_w2):
    batch, seq, d = x.shape
    ctx_len = ctx.shape[1]
    depth = ada_w.shape[0]
    n = batch * (seq + ctx_len)
    assert seq % TOKEN_TILE == 0 and ctx_len % TOKEN_TILE == 0 and n % ROUTER_TILE == 0
    geom = Geom(batch, seq, ctx_len)

    xs = jnp.concatenate([x.reshape(batch * seq, d), ctx.reshape(batch * ctx_len, d)], axis=0)
    mod_rows = ((batch + 1 + SUBLANES - 1) // SUBLANES) * SUBLANES
    cvec = jnp.zeros((mod_rows, d), F32).at[0:batch].set(c).at[batch].set(c_ctx)
    mods_all = _ada_table(cvec, ada_w, ada_b).reshape(depth, mod_rows, 1, 6 * d)
    cos_t, sin_t = _rope_tables(seq, PROJ_IN_TILE)

    for i in range(depth):
        kind, j = i % N_MIXERS, i // N_MIXERS
        mods = mods_all[i]
        if kind == 0:
            lam_init = 0.8 - 0.6 * math.exp(-0.3 * i)
            nh = d // LANES
            gains = jnp.stack([jnp.tile(a_q_norm[j], 2), jnp.tile(a_k_norm[j], 2)])
            q, k, v1 = _proj_in(xs, mods, norm1_g[i], a_wqkv[j].astype(BF16), gains, cos_t, sin_t, geom,
                                n_q=nh, n_k=nh, n_plain=nh, ones_after_plain=True)
            lam_vecs = jnp.stack([a_lambda_q1[j], a_lambda_k1[j], a_lambda_q2[j], a_lambda_k2[j]])
            o_lat, o_ctx = _diff_attention(q, k, v1, lam_vecs, a_subln_g[j], geom, lam_init)
            wo = a_wo[j].astype(BF16)
        elif kind == 1:
            q_heads = d // HEAD_DIM
            nq, nk = q_heads * HEAD_DIM, SWA_KV_HEADS * HEAD_DIM
            w = b_wqkv[j]
            w_pad = jnp.concatenate([_pad_heads_cols(w[:, :nq], q_heads),
                                     _pad_heads_cols(w[:, nq:nq + nk], SWA_KV_HEADS),
                                     _pad_heads_cols(w[:, nq + nk:], SWA_KV_HEADS)], axis=1).astype(BF16)
            zeros64 = jnp.zeros((HEAD_DIM,), F32)
            gains = jnp.stack([jnp.concatenate([b_q_norm[j], zeros64]), jnp.concatenate([b_k_norm[j], zeros64])])
            q, k, v = _proj_in(xs, mods, norm1_g[i], w_pad, gains, cos_t, sin_t, geom,
                               n_q=q_heads, n_k=SWA_KV_HEADS, n_plain=SWA_KV_HEADS)
            o_lat = o_ctx = _swa_attention(q, k, v, b_sink[j], geom)
            wo3 = b_wo[j].reshape(q_heads, HEAD_DIM, d)
            wo = jnp.concatenate([wo3, jnp.zeros_like(wo3)], axis=1).reshape(q_heads * LANES, d).astype(BF16)
        else:
            gd = d // FOURIER_GROUPS
            cd, msd = _dft_mats(gd, gd ** -0.5)
            eye = jnp.eye(FOURIER_GROUPS, dtype=BF16)
            w_cs = jnp.concatenate([jnp.kron(eye, cd), jnp.kron(eye, -msd)], axis=1)
            (z,) = _proj_in(xs, mods, norm1_g[i], w_cs, jnp.zeros((2, LANES), F32), cos_t, sin_t, geom,
                            n_q=0, n_k=0, n_plain=2 * d // LANES)
            o_lat = o_ctx = _seq_dft(z, _dft_mats(ctx_len, ctx_len ** -0.5), geom, d)
            wo = f_wo[j].astype(BF16)
        xs, h2 = _proj_out(o_lat, o_ctx, wo, xs, mods, norm2_g[i], geom)
        n_out = n if i < depth - 1 else batch * seq
        xs = _hier_moe(h2, xs, mods, geom, r_w1[i], r_b1[i], r_w2[i], r_b2[i], e_w13, e_w2, i, n_out)

    return xs.reshape(batch, seq, d)
```

```python
import functools
import math
from typing import NamedTuple

import numpy as np
import jax
import jax.numpy as jnp
from jax import lax
from jax.experimental import pallas as pl
from jax.experimental.pallas import tpu as pltpu

F32 = jnp.float32
BF16 = jnp.bfloat16
HIGHEST = lax.Precision.HIGHEST

GRID_W = 64
HEAD_DIM = 64
ROT_FREQS = HEAD_DIM // 4
ROPE_THETA = 10000.0
WINDOW = 128
N_MIXERS = 3
SWA_KV_HEADS = 4
FOURIER_GROUPS = 4
N_GROUPS = 4
EXPERTS_PER_GROUP = 8
N_EXPERTS = N_GROUPS * EXPERTS_PER_GROUP
EPS = 1e-6
NEG_INF = -1e30

LANES = 128
SUBLANES = 8
MXU_DIM = 256
TOKEN_TILE = 256
PROJ_IN_TILE = 512
MOE_HALF = 256
MOE_TILE = 2 * MOE_HALF
ROUTER_TILE = 512
VMEM_LIMIT = 48 * 1024 * 1024


def _cparams(*sem):
    return pltpu.CompilerParams(dimension_semantics=sem, vmem_limit_bytes=VMEM_LIMIT)


class Geom(NamedTuple):
    batch: int
    seq: int
    ctx_len: int

    @property
    def lat_tiles(self):
        return self.seq // TOKEN_TILE

    @property
    def n_lat_tiles(self):
        return self.batch * self.lat_tiles

    @property
    def n_tiles(self):
        return self.n_lat_tiles + self.batch * (self.ctx_len // TOKEN_TILE)

    def mod_row(self, i):
        return jnp.where(i < self.n_lat_tiles, i // self.lat_tiles, self.batch)

    def ctx_tile(self, b):
        return self.n_lat_tiles + b


def _ada_kernel(c_ref, w_ref, b_ref, o_ref):
    c = c_ref[...]
    s = c * (1.0 / (1.0 + jnp.exp(-c)))
    o_ref[...] = jnp.dot(s, w_ref[...], precision=HIGHEST, preferred_element_type=F32) + b_ref[...]


def _ada_table(cvec, ada_w, ada_b):
    depth, d, n6 = ada_w.shape
    rows = cvec.shape[0]
    tn = n6 // 4
    return pl.pallas_call(
        _ada_kernel,
        grid=(depth, n6 // tn),
        in_specs=[pl.BlockSpec((rows, d), lambda l, j: (0, 0)),
                  pl.BlockSpec((None, d, tn), lambda l, j: (l, 0, j)),
                  pl.BlockSpec((None, 1, tn), lambda l, j: (l, 0, j))],
        out_specs=pl.BlockSpec((None, rows, tn), lambda l, j: (l, 0, j)),
        out_shape=jax.ShapeDtypeStruct((depth, rows, n6), F32),
        compiler_params=_cparams("parallel", "parallel"),
        name="ada_table",
    )(cvec, ada_w, ada_b.reshape(depth, 1, n6))


def _head_mean_matrix():
    r = np.arange(LANES)
    return jnp.asarray((r[:, None] // HEAD_DIM == r[None, :] // HEAD_DIM).astype(np.float32) / HEAD_DIM, BF16)


def _proj_in_kernel(x_ref, mod_ref, g_ref, w_ref, hm_ref, gain_ref, cos_ref, sin_ref, *out_refs,
                    d, n_q, n_k, n_plain, ones_after_plain, q_scale):
    q_ref = out_refs[0] if n_q else None
    k_ref = out_refs[1] if n_k else None
    p_ref = out_refs[-1]
    x = x_ref[...]
    ms = jnp.mean(x * x, axis=-1, keepdims=True)
    h = x * lax.rsqrt(ms + EPS) * g_ref[...]
    h = h * (1.0 + mod_ref[:, d:2 * d]) + mod_ref[:, 0:d]
    hb = h.astype(BF16)
    lane = lax.broadcasted_iota(jnp.int32, (1, LANES), 1)
    first_half = (lane // ROT_FREQS) % 2 == 0
    n_chunks = n_q + n_k + n_plain
    for c2 in range(0, n_chunks, 2):
        width = min(2, n_chunks - c2) * LANES
        y2 = jnp.dot(hb, w_ref[:, c2 * LANES:c2 * LANES + width], preferred_element_type=F32)
        for half in range(width // LANES):
            c = c2 + half
            y = y2[:, half * LANES:(half + 1) * LANES]
            if c < n_q + n_k:
                msq = jnp.dot((y * y).astype(BF16), hm_ref[...], preferred_element_type=F32)
                is_q = c < n_q
                gain = gain_ref[0:1, :] if is_q else gain_ref[1:2, :]
                yn = y * lax.rsqrt(msq + EPS) * gain
                partner = jnp.where(first_half, pltpu.roll(yn, LANES - ROT_FREQS, 1), pltpu.roll(yn, ROT_FREQS, 1))
                out = yn * cos_ref[...] + partner * sin_ref[...]
                if is_q:
                    q_ref[:, c * LANES:(c + 1) * LANES] = (out * q_scale).astype(q_ref.dtype)
                else:
                    ck = c - n_q
                    k_ref[:, ck * LANES:(ck + 1) * LANES] = out.astype(k_ref.dtype)
            else:
                cp = c - n_q - n_k
                if ones_after_plain:
                    p_ref[:, 2 * cp * LANES:(2 * cp + 1) * LANES] = y.astype(p_ref.dtype)
                    p_ref[:, (2 * cp + 1) * LANES:(2 * cp + 2) * LANES] = jnp.ones(y.shape, p_ref.dtype)
                else:
                    p_ref[:, cp * LANES:(cp + 1) * LANES] = y.astype(p_ref.dtype)


Q_SCALE_LOG2 = HEAD_DIM ** -0.5 * math.log2(math.e)


def _proj_in(xs, mods, g, w, gains, cos_t, sin_t, geom, *, n_q, n_k, n_plain, ones_after_plain=False,
             q_scale=Q_SCALE_LOG2):
    n, d = xs.shape
    tm = PROJ_IN_TILE
    lat_tiles = geom.seq // tm
    n_lat = geom.batch * lat_tiles
    assert geom.seq % tm == 0 and n % tm == 0 and cos_t.shape[0] == geom.seq + tm

    def mod_row(i):
        return jnp.where(i < n_lat, i // lat_tiles, geom.batch)

    def rope_row(i):
        return jnp.where(i < n_lat, i % lat_tiles, lat_tiles)

    n_cols = (n_q + n_k + n_plain) * LANES
    assert w.shape == (d, n_cols)
    out_shape, out_specs = [], []
    for cnt in (n_q, n_k):
        if cnt:
            out_shape.append(jax.ShapeDtypeStruct((n, cnt * LANES), BF16))
            out_specs.append(pl.BlockSpec((tm, cnt * LANES), lambda i: (i, 0)))
    pw = n_plain * LANES * (2 if ones_after_plain else 1)
    out_shape.append(jax.ShapeDtypeStruct((n, pw), BF16))
    out_specs.append(pl.BlockSpec((tm, pw), lambda i: (i, 0)))
    kern = functools.partial(_proj_in_kernel, d=d, n_q=n_q, n_k=n_k, n_plain=n_plain,
                             ones_after_plain=ones_after_plain, q_scale=q_scale)
    return pl.pallas_call(
        kern,
        grid=(n // tm,),
        in_specs=[pl.BlockSpec((tm, d), lambda i: (i, 0)),
                  pl.BlockSpec((None, 1, mods.shape[-1]), lambda i: (mod_row(i), 0, 0)),
                  pl.BlockSpec((1, d), lambda i: (0, 0)),
                  pl.BlockSpec((d, n_cols), lambda i: (0, 0)),
                  pl.BlockSpec((LANES, LANES), lambda i: (0, 0)),
                  pl.BlockSpec((2, LANES), lambda i: (0, 0)),
                  pl.BlockSpec((tm, LANES), lambda i: (rope_row(i), 0)),
                  pl.BlockSpec((tm, LANES), lambda i: (rope_row(i), 0))],
        out_specs=out_specs,
        out_shape=out_shape,
        compiler_params=_cparams("parallel"),
        name="proj_in",
    )(xs, mods, g.reshape(1, d), w, _head_mean_matrix(), gains, cos_t, sin_t)


DIFF_KV_CHUNK = 512
DIFF_Q_TILE = 256
DIFF_HEADS_PER_STEP = 2


def _diff_attn_kernel(*refs, lam_init, n_kv, chunks, heads):
    lam_ref, q_ref = refs[0], refs[1]
    k_refs = refs[2:2 + n_kv]
    v_refs = refs[2 + n_kv:2 + 2 * n_kv]
    sg_ref, o_ref, s_ref, m_ref = refs[2 + 2 * n_kv:]
    dq = 2 * HEAD_DIM
    dv = o_ref.shape[1] // heads
    in_grid = pl.program_id(0) < pl.num_programs(0)

    @pl.when(in_grid)
    def _():
        for h in range(heads):
            q = q_ref[:, h * dq:(h + 1) * dq]
            lane = lax.broadcasted_iota(jnp.int32, q.shape, 1)
            zero = jnp.zeros_like(q)
            q_maps = (jnp.where(lane < HEAD_DIM, q, zero), jnp.where(lane >= HEAD_DIM, q, zero))
            for m in range(2):
                mx = None
                for slab, start, size, col in chunks:
                    s = lax.dot_general(q_maps[m], k_refs[slab][start:start + size, h * dq:(h + 1) * dq],
                                        (((1,), (1,)), ((), ())), preferred_element_type=F32)
                    s_ref[2 * h + m, :, col:col + size] = s
                    cm = jnp.max(s, axis=1, keepdims=True)
                    mx = cm if mx is None else jnp.maximum(mx, cm)
                m_ref[2 * h + m] = mx

    @pl.when(in_grid)
    def _():
        lam_v = lam_ref[...]
        lam = (jnp.exp(jnp.sum(lam_v[0:1] * lam_v[1:2], axis=1, keepdims=True))
               - jnp.exp(jnp.sum(lam_v[2:3] * lam_v[3:4], axis=1, keepdims=True)) + lam_init)
        for h in range(heads):
            outs = []
            for m in range(2):
                row_max = m_ref[2 * h + m]
                acc = None
                for slab, start, size, col in chunks:
                    p = jnp.exp2(s_ref[2 * h + m, :, col:col + size] - row_max)
                    pv = jnp.dot(p.astype(BF16), v_refs[slab][start:start + size, 2 * h * dv:2 * (h + 1) * dv],
                                 preferred_element_type=F32)
                    acc = pv if acc is None else acc + pv
                outs.append(acc[:, 0:dv] * (1.0 / acc[:, dv:dv + 1]))
            a = outs[0] - lam * outs[1]
            a = a * lax.rsqrt(jnp.mean(a * a, axis=-1, keepdims=True) + EPS) * sg_ref[...] * (1.0 - lam_init)
            o_ref[:, h * dv:(h + 1) * dv] = a.astype(o_ref.dtype)


def _diff_attention(q, k, v1, lam_vecs, sub_g, geom, lam_init):
    dq = q.shape[1]
    batch, seq, ctx_len = geom
    heads = dq // (2 * HEAD_DIM)
    dv = 2 * HEAD_DIM
    tq = DIFF_Q_TILE
    q_tiles = seq // tq
    ctx_blk = seq // ctx_len * batch
    lam_spec = pl.BlockSpec((4, HEAD_DIM), lambda *_: (0, 0))
    sg_spec = pl.BlockSpec((1, dv), lambda *_: (0, 0))
    chunks = [(0, c * DIFF_KV_CHUNK, DIFF_KV_CHUNK, c * DIFF_KV_CHUNK) for c in range(seq // DIFF_KV_CHUNK)]
    chunks.append((1, 0, ctx_len, seq))
    hp = DIFF_HEADS_PER_STEP
    assert heads % hp == 0
    o_lat = pl.pallas_call(
        functools.partial(_diff_attn_kernel, lam_init=lam_init, n_kv=2, chunks=chunks, heads=hp),
        grid=(batch, heads // hp, q_tiles),
        in_specs=[lam_spec,
                  pl.BlockSpec((tq, hp * 2 * HEAD_DIM), lambda b, h, i: (b * q_tiles + i, h)),
                  pl.BlockSpec((seq, hp * 2 * HEAD_DIM), lambda b, h, i: (b, h)),
                  pl.BlockSpec((ctx_len, hp * 2 * HEAD_DIM), lambda b, h, i: (ctx_blk + b, h)),
                  pl.BlockSpec((seq, hp * 2 * dv), lambda b, h, i: (b, h)),
                  pl.BlockSpec((ctx_len, hp * 2 * dv), lambda b, h, i: (ctx_blk + b, h)),
                  sg_spec],
        out_specs=pl.BlockSpec((tq, hp * dv), lambda b, h, i: (b * q_tiles + i, h)),
        out_shape=jax.ShapeDtypeStruct((batch * seq, heads * dv), BF16),
        scratch_shapes=[pltpu.VMEM((2 * hp, tq, seq + ctx_len), F32), pltpu.VMEM((2 * hp, tq, 1), F32)],
        compiler_params=_cparams("parallel", "parallel", "arbitrary"),
        name="diff_attention",
    )(lam_vecs, q, k, k, v1, v1, sub_g.reshape(1, dv))
    o_ctx = pl.pallas_call(
        functools.partial(_diff_attn_kernel, lam_init=lam_init, n_kv=1, chunks=[(0, 0, ctx_len, 0)], heads=1),
        grid=(batch, heads),
        in_specs=[lam_spec,
                  pl.BlockSpec((ctx_len, 2 * HEAD_DIM), lambda b, h: (ctx_blk + b, h)),
                  pl.BlockSpec((ctx_len, 2 * HEAD_DIM), lambda b, h: (ctx_blk + b, h)),
                  pl.BlockSpec((ctx_len, 2 * dv), lambda b, h: (ctx_blk + b, h)),
                  sg_spec],
        out_specs=pl.BlockSpec((ctx_len, dv), lambda b, h: (b, h)),
        out_shape=jax.ShapeDtypeStruct((batch * ctx_len, heads * dv), BF16),
        scratch_shapes=[pltpu.VMEM((2, ctx_len, ctx_len), F32), pltpu.VMEM((2, ctx_len, 1), F32)],
        compiler_params=_cparams("parallel", "parallel"),
        name="diff_attention_ctx",
    )(lam_vecs, q, k, v1, sub_g.reshape(1, dv))
    return o_lat, o_ctx


def _swa_kernel(sink_ref, q_ref, kl_ref, kc_ref, vl_ref, vc_ref, o_ref, *, latent_tiles, group):
    i = pl.program_id(1)
    tq = q_ref.shape[0]
    seq, ctx_len = kl_ref.shape[0], kc_ref.shape[0]
    span = tq + 2 * WINDOW
    start = pl.multiple_of(jnp.clip(i * tq - WINDOW, 0, seq - span), WINDOW)
    qpos = i * tq + lax.broadcasted_iota(jnp.int32, (tq, span), 0)
    kpos = start + lax.broadcasted_iota(jnp.int32, (tq, span), 1)
    valid = (jnp.abs(kpos - qpos) <= WINDOW) & (i < latent_tiles)
    bias = jnp.where(valid, 0.0, NEG_INF).astype(F32)
    nt = (((1,), (1,)), ((), ()))
    for h in range(SWA_KV_HEADS):
        hs = slice(h * LANES, (h + 1) * LANES)
        kw = kl_ref[pl.ds(start, span), hs]
        vw = vl_ref[pl.ds(start, span), hs]
        kc = kc_ref[:, hs]
        vc = vc_ref[:, hs]
        for g in range(group):
            hq = h * group + g
            q = q_ref[:, hq * LANES:(hq + 1) * LANES]
            s_w = lax.dot_general(q, kw, nt, preferred_element_type=F32) + bias
            s_c = lax.dot_general(q, kc, nt, preferred_element_type=F32)
            sink = sink_ref[hq] * math.log2(math.e)
            m = jnp.maximum(jnp.maximum(jnp.max(s_w, axis=1, keepdims=True), jnp.max(s_c, axis=1, keepdims=True)),
                            sink)
            p_w = jnp.exp2(s_w - m)
            p_c = jnp.exp2(s_c - m)
            denom = (jnp.sum(p_w, axis=1, keepdims=True) + jnp.sum(p_c, axis=1, keepdims=True)
                     + jnp.exp2(sink - m))
            o = (jnp.dot(p_w.astype(BF16), vw, preferred_element_type=F32)
                 + jnp.dot(p_c.astype(BF16), vc, preferred_element_type=F32)) * (1.0 / denom)
            o_ref[:, hq * LANES:(hq + 1) * LANES] = o.astype(o_ref.dtype)


def _swa_attention(q, k, v, sink, geom):
    n, dq = q.shape
    batch, seq, ctx_len = geom
    assert ctx_len == TOKEN_TILE
    lt = geom.lat_tiles
    q_heads = dq // LANES
    kvw = SWA_KV_HEADS * LANES
    ctx_blk = seq // ctx_len * batch
    kern = functools.partial(_swa_kernel, latent_tiles=lt, group=q_heads // SWA_KV_HEADS)

    def tile(b, i):
        return jnp.where(i < lt, b * lt + i, geom.ctx_tile(b))

    lat_kv = pl.BlockSpec((seq, kvw), lambda b, i: (b, 0))
    ctx_kv = pl.BlockSpec((ctx_len, kvw), lambda b, i: (ctx_blk + b, 0))
    return pl.pallas_call(
        kern,
        grid=(batch, lt + 1),
        in_specs=[pl.BlockSpec(memory_space=pltpu.SMEM),
                  pl.BlockSpec((TOKEN_TILE, dq), lambda b, i: (tile(b, i), 0)),
                  lat_kv, ctx_kv, lat_kv, ctx_kv],
        out_specs=pl.BlockSpec((TOKEN_TILE, dq), lambda b, i: (tile(b, i), 0)),
        out_shape=jax.ShapeDtypeStruct((n, dq), BF16),
        compiler_params=_cparams("parallel", "arbitrary"),
        name="swa_attention",
    )(sink, q, k, k, v, v)


def _seq_dft_kernel(base_ref, fine_c_ref, fine_s_ref, cc_ref, cs_ref, zc_ref, zs_ref, yc_ref, ys_ref, o_ref,
                    *, latent_tiles):
    m = pl.program_id(2)

    def mix(a_cos, a_msin, z_cos, z_sin):
        acc = jnp.dot(a_cos, z_cos[...], preferred_element_type=F32)
        acc = acc + jnp.dot(a_msin, z_sin[...], preferred_element_type=F32)
        o_ref[...] = acc.astype(o_ref.dtype)

    @pl.when(m < latent_tiles)
    def _():
        c1, s1 = base_ref[0:1, :], base_ref[1:2, :]
        c2, s2 = fine_c_ref[...], fine_s_ref[...]
        a_cos = (c2 * c1 - s2 * s1).astype(BF16)
        a_msin = (-(c2 * s1) - s2 * c1).astype(BF16)
        mix(a_cos, a_msin, zc_ref, zs_ref)

    @pl.when(m >= latent_tiles)
    def _():
        mix(cc_ref[...], cs_ref[...], yc_ref, ys_ref)


def _dft_angle_tables(n, rows, scale):
    k = jnp.arange(n, dtype=jnp.int32)[None, :]
    step = 2.0 * math.pi / n
    ang0 = (((jnp.arange(n // rows, dtype=jnp.int32) * rows)[:, None] * k) % n).astype(F32) * step
    ang1 = ((jnp.arange(rows, dtype=jnp.int32)[:, None] * k) % n).astype(F32) * step
    base = jnp.stack([jnp.cos(ang0), jnp.sin(ang0)], axis=1)
    return base, jnp.cos(ang1) * scale, jnp.sin(ang1) * scale


def _seq_dft(z, ctx_mats, geom, d):
    batch, seq, ctx_len = geom
    assert ctx_len == TOKEN_TILE and seq % ctx_len == 0
    lt = geom.lat_tiles
    tm = TOKEN_TILE
    tn = d // 2
    n_col = d // tn
    ctx_blk = seq // ctx_len * batch
    lat_tables = _dft_angle_tables(seq, tm, seq ** -0.5)
    fine = pl.BlockSpec((tm, seq), lambda b, j, m: (0, 0))
    ctx_a = pl.BlockSpec((ctx_len, ctx_len), lambda b, j, m: (0, 0))
    return pl.pallas_call(
        functools.partial(_seq_dft_kernel, latent_tiles=lt),
        grid=(batch, n_col, lt + 1),
        in_specs=[pl.BlockSpec((None, 2, seq), lambda b, j, m: (jnp.minimum(m, lt - 1), 0, 0)),
                  fine, fine, ctx_a, ctx_a,
                  pl.BlockSpec((seq, tn), lambda b, j, m: (b, j)),
                  pl.BlockSpec((seq, tn), lambda b, j, m: (b, n_col + j)),
                  pl.BlockSpec((ctx_len, tn), lambda b, j, m: (ctx_blk + b, j)),
                  pl.BlockSpec((ctx_len, tn), lambda b, j, m: (ctx_blk + b, n_col + j))],
        out_specs=pl.BlockSpec((tm, tn), lambda b, j, m: (jnp.where(m < lt, b * lt + m, geom.ctx_tile(b)), j)),
        out_shape=jax.ShapeDtypeStruct((z.shape[0], d), BF16),
        compiler_params=_cparams("parallel", "parallel", "arbitrary"),
        name="seq_dft",
    )(*lat_tables, *ctx_mats, z, z, z, z)


def _dft_mats(n, scale):
    idx = (jnp.arange(n, dtype=jnp.int32)[:, None] * jnp.arange(n, dtype=jnp.int32)[None, :]) % n
    ang = idx.astype(F32) * (2.0 * math.pi / n)
    return (jnp.cos(ang) * scale).astype(BF16), (-jnp.sin(ang) * scale).astype(BF16)


def _load_token_tiles(ref):
    rows = ref.shape[0] // SUBLANES
    return jnp.concatenate([ref[pl.ds(s, rows, stride=SUBLANES), :] for s in range(SUBLANES)], axis=1)


def _store_token_tiles(ref, val):
    rows = ref.shape[0] // SUBLANES
    for s in range(SUBLANES):
        ref[pl.ds(s, rows, stride=SUBLANES), :] = val[:, s * LANES:(s + 1) * LANES]


def _proj_out_kernel(ol_ref, oc_ref, w_ref, x_ref, mod_ref, g_ref, xo_ref, h_ref, *, d, n_lat_tiles):
    def finish(o_ref):
        y = jnp.dot(o_ref[...], w_ref[...], preferred_element_type=F32)
        x = x_ref[...] + mod_ref[:, 2 * d:3 * d] * y
        xo_ref[...] = x
        ms = jnp.mean(x * x, axis=-1, keepdims=True)
        h = x * lax.rsqrt(ms + EPS) * g_ref[...]
        _store_token_tiles(h_ref, h * (1.0 + mod_ref[:, 4 * d:5 * d]) + mod_ref[:, 3 * d:4 * d])

    @pl.when(pl.program_id(0) < n_lat_tiles)
    def _():
        finish(ol_ref)

    @pl.when(pl.program_id(0) >= n_lat_tiles)
    def _():
        finish(oc_ref)


def _proj_out(o_lat, o_ctx, w, xs, mods, g2, geom):
    n, d = xs.shape
    ko = o_lat.shape[1]
    tm = TOKEN_TILE
    nl = geom.n_lat_tiles
    ctx_off = nl if o_ctx.shape[0] == n else 0
    return pl.pallas_call(
        functools.partial(_proj_out_kernel, d=d, n_lat_tiles=nl),
        grid=(n // tm,),
        in_specs=[pl.BlockSpec((tm, ko), lambda i: (jnp.minimum(i, nl - 1), 0)),
                  pl.BlockSpec((tm, ko), lambda i: (jnp.maximum(i, nl) - nl + ctx_off, 0)),
                  pl.BlockSpec((ko, d), lambda i: (0, 0)),
                  pl.BlockSpec((tm, d), lambda i: (i, 0)),
                  pl.BlockSpec((None, 1, mods.shape[-1]), lambda i: (geom.mod_row(i), 0, 0)),
                  pl.BlockSpec((1, d), lambda i: (0, 0))],
        out_specs=[pl.BlockSpec((tm, d), lambda i: (i, 0)),
                   pl.BlockSpec((tm * SUBLANES, LANES), lambda i: (i, 0))],
        out_shape=[jax.ShapeDtypeStruct((n, d), F32), jax.ShapeDtypeStruct((n * SUBLANES, LANES), F32)],
        compiler_params=_cparams("parallel"),
        name="proj_out",
    )(o_lat, o_ctx, w, xs, mods, g2.reshape(1, d))


ROUTER_ROWS = SUBLANES + N_EXPERTS


def _router_kernel(h_ref, wt_ref, b_ref, tri_ref, ints_ref, flt_ref, cnt_ref, carry_ref):
    step = pl.program_id(0)
    tr = h_ref.shape[0] // SUBLANES

    @pl.when(step == 0)
    def _():
        carry_ref[...] = jnp.zeros(carry_ref.shape, F32)

    logits = lax.dot_general(wt_ref[...], _load_token_tiles(h_ref), (((1,), (1,)), ((), ())),
                             precision=HIGHEST, preferred_element_type=F32) + b_ref[:, 0:1]
    row8 = lax.broadcasted_iota(jnp.int32, (SUBLANES, tr), 0)
    lg = jnp.where(row8 < N_GROUPS, logits[0:SUBLANES], NEG_INF)
    lg_max = jnp.max(lg, axis=0, keepdims=True)
    pg = 1.0 / jnp.sum(jnp.exp(lg - lg_max), axis=0, keepdims=True)
    grp = jnp.min(jnp.where(lg == lg_max, row8, SUBLANES), axis=0, keepdims=True)
    l2 = jnp.zeros((EXPERTS_PER_GROUP, tr), F32)
    for g in range(N_GROUPS):
        lo = SUBLANES + g * EXPERTS_PER_GROUP
        l2 = l2 + jnp.where(grp == g, logits[lo:lo + EXPERTS_PER_GROUP], 0.0)
    l2_max = jnp.max(l2, axis=0, keepdims=True)
    j0 = jnp.min(jnp.where(l2 == l2_max, row8, SUBLANES), axis=0, keepdims=True)
    rest = jnp.where(row8 == j0, NEG_INF, l2)
    r_max = jnp.max(rest, axis=0, keepdims=True)
    j1 = jnp.min(jnp.where(rest == r_max, row8, SUBLANES), axis=0, keepdims=True)
    e1 = jnp.exp(r_max - l2_max)
    inv = 1.0 / (1.0 + e1)
    w0 = pg * inv
    w1 = pg * e1 * inv
    ex0 = grp * EXPERTS_PER_GROUP + j0
    ex1 = grp * EXPERTS_PER_GROUP + j1

    rows = lax.broadcasted_iota(jnp.int32, (N_EXPERTS, tr), 0)
    oh0 = (rows == ex0).astype(F32)
    oh1 = (rows == ex1).astype(F32)
    both = oh0 + oh1
    before = jnp.dot(both.astype(BF16), tri_ref[...], preferred_element_type=F32) + carry_ref[:, 0:1]
    rank0 = jnp.sum(oh0 * before, axis=0, keepdims=True)
    rank1 = jnp.sum(oh1 * before, axis=0, keepdims=True)
    carry_ref[...] = carry_ref[...] + jnp.sum(both, axis=1, keepdims=True)
    cnt_ref[...] = carry_ref[...]

    zi = jnp.zeros((SUBLANES - 4, tr), jnp.int32)
    ints_ref[...] = jnp.concatenate([ex0, ex1, rank0.astype(jnp.int32), rank1.astype(jnp.int32), zi], axis=0)
    flt_ref[...] = jnp.concatenate([w0, w1, jnp.zeros((SUBLANES - 2, tr), F32)], axis=0)


def _router(h3, wt, bias, tri):
    n = h3.shape[0] // SUBLANES
    d = wt.shape[1]
    tr = ROUTER_TILE
    return pl.pallas_call(
        _router_kernel,
        grid=(n // tr,),
        in_specs=[pl.BlockSpec((tr * SUBLANES, LANES), lambda i: (i, 0)),
                  pl.BlockSpec((ROUTER_ROWS, d), lambda i: (0, 0)),
                  pl.BlockSpec((ROUTER_ROWS, LANES), lambda i: (0, 0)),
                  pl.BlockSpec((tr, tr), lambda i: (0, 0))],
        out_specs=[pl.BlockSpec((SUBLANES, tr), lambda i: (0, i)),
                   pl.BlockSpec((SUBLANES, tr), lambda i: (0, i)),
                   pl.BlockSpec((N_EXPERTS, LANES), lambda i: (0, 0))],
        out_shape=[jax.ShapeDtypeStruct((SUBLANES, n), jnp.int32),
                   jax.ShapeDtypeStruct((SUBLANES, n), F32),
                   jax.ShapeDtypeStruct((N_EXPERTS, LANES), F32)],
        scratch_shapes=[pltpu.VMEM((N_EXPERTS, LANES), F32)],
        compiler_params=_cparams("arbitrary"),
        name="router",
    )(h3, wt, bias, tri)


def _dest_kernel(pstart_ref, ints_ref, o_ref):
    ints = ints_ref[...]
    ex = ints[0:2]
    base = jnp.zeros(ex.shape, jnp.int32)
    for e in range(N_EXPERTS):
        base = jnp.where(ex == e, pstart_ref[e], base)
    o_ref[...] = jnp.concatenate([base + ints[2:4], jnp.zeros((SUBLANES - 2, ints.shape[1]), jnp.int32)], axis=0)


def _dest_rows(pstart, ints):
    n = ints.shape[1]
    tn = 2048 if n % 2048 == 0 else ROUTER_TILE
    return pl.pallas_call(
        _dest_kernel,
        grid=(n // tn,),
        in_specs=[pl.BlockSpec(memory_space=pltpu.SMEM),
                  pl.BlockSpec((SUBLANES, tn), lambda i: (0, i))],
        out_specs=pl.BlockSpec((SUBLANES, tn), lambda i: (0, i)),
        out_shape=jax.ShapeDtypeStruct((SUBLANES, n), jnp.int32),
        compiler_params=_cparams("parallel"),
        name="dest_rows",
    )(pstart, ints)


def _row_maps_kernel(pad_ref, d0_ref, d1_ref, out_ref, *, n_tokens):
    step = pl.program_id(0)
    td = d0_ref.shape[0]
    n_spans = pad_ref.shape[0] // 2

    @pl.when(step == 0)
    def _():
        def span(e, carry):
            def fill(r, c):
                row = pad_ref[e] + r
                out_ref[row] = 2 * n_tokens + (row & (MOE_HALF - 1))
                return c
            lax.fori_loop(0, pad_ref[n_spans + e], fill, 0)
            return carry
        lax.fori_loop(0, n_spans, span, 0)

    base = step * td

    def place(r, carry):
        n = base + r
        out_ref[d0_ref[r]] = n
        out_ref[d1_ref[r]] = n_tokens + n
        return carry
    lax.fori_loop(0, td, place, 0, unroll=8)


def _row_maps(pad_spans, dest0, dest1, n_rows):
    n = dest0.shape[0]
    td = 2048 if n % 2048 == 0 else ROUTER_TILE
    smem = functools.partial(pl.BlockSpec, memory_space=pltpu.SMEM)
    return pl.pallas_call(
        functools.partial(_row_maps_kernel, n_tokens=n),
        grid=(n // td,),
        in_specs=[smem(), smem((td,), lambda i: (i,)), smem((td,), lambda i: (i,))],
        out_specs=smem(),
        out_shape=jax.ShapeDtypeStruct((n_rows,), jnp.int32),
        compiler_params=_cparams("arbitrary"),
        name="moe_row_maps",
    )(pad_spans, dest0, dest1)


def _token_tile(ref, r):
    return ref.at[pl.ds(pl.multiple_of(r * SUBLANES, SUBLANES), SUBLANES)]


def _expert_kernel(blk_exp_ref, n_used_ref, map_first, map_half1, map_next, h_ref, w13_ref, w2_ref, y_ref,
                   xbuf0, xbuf1, w13b, w2b, gsem, *, d_expert, n_tokens):
    j = pl.program_id(0)
    n_used = n_used_ref[0]
    xbuf = (xbuf0, xbuf1)
    half_rows = MOE_HALF * SUBLANES

    def gather_row(idx_ref, buf, r):
        v = idx_ref[r]
        tok = v - jnp.where(v >= 2 * n_tokens, 2 * n_tokens, jnp.where(v >= n_tokens, n_tokens, 0))
        pltpu.make_async_copy(_token_tile(h_ref, tok), _token_tile(xbuf[buf], r), gsem.at[buf]).start()

    def wait_gather(buf):
        pltpu.make_async_copy(h_ref.at[pl.ds(0, half_rows)], xbuf[buf], gsem.at[buf]).wait()

    def half_block(buf, gather_idx):
        for r in range(MOE_HALF):
            gather_row(gather_idx, 1 - buf, r)
        x = _load_token_tiles(xbuf[buf]).astype(BF16)
        gu = jnp.dot(x, w13b[...], preferred_element_type=F32)
        g = gu[:, 0:d_expert]
        u = gu[:, d_expert:2 * d_expert]
        a = g * (1.0 / (1.0 + jnp.exp(-g))) * u
        _store_token_tiles(y_ref.at[pl.ds(buf * half_rows, half_rows)],
                           jnp.dot(a.astype(BF16), w2b[...], preferred_element_type=F32))

    @pl.when(j >= n_used)
    def _():
        y_ref[...] = jnp.zeros(y_ref.shape, F32)

    @pl.when(j < n_used)
    def _():
        @pl.when(j == 0)
        def _():
            def first_rows(r, carry):
                gather_row(map_first, 0, r)
                return carry
            lax.fori_loop(0, MOE_HALF, first_rows, 0)

        @pl.when((j == 0) | (blk_exp_ref[j] != blk_exp_ref[jnp.maximum(j - 1, 0)]))
        def _():
            w13b[...] = w13_ref[...].astype(BF16)
            w2b[...] = w2_ref[...].astype(BF16)

        wait_gather(0)
        half_block(0, map_half1)
        wait_gather(1)
        half_block(1, map_next)

        @pl.when(j == n_used - 1)
        def _():
            wait_gather(0)


def _experts(blk_exp, n_used, row_map, h3, w13_all, w2_all, layer):
    n = h3.shape[0] // SUBLANES
    d, d_expert = w13_all.shape[2], w2_all.shape[2]
    n_steps = row_map.shape[0] // MOE_TILE
    smem = functools.partial(pl.BlockSpec, memory_space=pltpu.SMEM)

    def half_spec(half_of):
        def index(j, be, nu):
            last = 2 * nu[0] - 1
            return (jnp.clip(half_of(jnp.minimum(j, nu[0] - 1), last), 0, last),)
        return smem((MOE_HALF,), index)

    def expert(j, be, nu):
        return (layer, be[jnp.minimum(j, nu[0] - 1)], 0, 0)

    token_tile = (MOE_HALF * SUBLANES, LANES)
    grid_spec = pltpu.PrefetchScalarGridSpec(
        num_scalar_prefetch=2,
        grid=(n_steps,),
        in_specs=[half_spec(lambda j, last: 0),
                  half_spec(lambda j, last: 2 * j + 1),
                  half_spec(lambda j, last: 2 * j + 2),
                  pl.BlockSpec(memory_space=pl.ANY),
                  pl.BlockSpec((None, None, d, 2 * d_expert), expert),
                  pl.BlockSpec((None, None, d_expert, d), expert)],
        out_specs=pl.BlockSpec((MOE_TILE * SUBLANES, LANES), lambda j, be, nu: (j, 0)),
        scratch_shapes=[pltpu.VMEM(token_tile, F32), pltpu.VMEM(token_tile, F32),
                        pltpu.VMEM((d, 2 * d_expert), BF16), pltpu.VMEM((d_expert, d), BF16),
                        pltpu.SemaphoreType.DMA((2,))],
    )
    return pl.pallas_call(
        functools.partial(_expert_kernel, d_expert=d_expert, n_tokens=n),
        grid_spec=grid_spec,
        out_shape=jax.ShapeDtypeStruct((n_steps * MOE_TILE * SUBLANES, LANES), F32),
        compiler_params=_cparams("arbitrary"),
        name="moe_experts",
    )(blk_exp, n_used, row_map, row_map, row_map, h3, w13_all, w2_all)


def _combine_kernel(d0_ref, d1_ref, w_ref, x_ref, mod_ref, y_ref, o_ref, b0_ref, b1_ref, sem_ref, *, d):
    tc = x_ref.shape[0]

    def issue(r, carry):
        pltpu.make_async_copy(_token_tile(y_ref, d0_ref[r]), _token_tile(b0_ref, r), sem_ref.at[0]).start()
        pltpu.make_async_copy(_token_tile(y_ref, d1_ref[r]), _token_tile(b1_ref, r), sem_ref.at[1]).start()
        return carry
    lax.fori_loop(0, tc, issue, 0, unroll=8)
    pltpu.make_async_copy(y_ref.at[pl.ds(0, tc * SUBLANES)], b0_ref, sem_ref.at[0]).wait()
    pltpu.make_async_copy(y_ref.at[pl.ds(0, tc * SUBLANES)], b1_ref, sem_ref.at[1]).wait()
    out = w_ref[:, 0:1] * _load_token_tiles(b0_ref) + w_ref[:, 1:2] * _load_token_tiles(b1_ref)
    o_ref[...] = x_ref[...] + mod_ref[:, 5 * d:6 * d] * out


def _combine(dest0, dest1, w_cols, xs, mods, y_sorted, geom, n_out):
    n, d = xs.shape
    tc = TOKEN_TILE
    smem = functools.partial(pl.BlockSpec, memory_space=pltpu.SMEM)
    return pl.pallas_call(
        functools.partial(_combine_kernel, d=d),
        grid=(n_out // tc,),
        in_specs=[smem((tc,), lambda i: (i,)),
                  smem((tc,), lambda i: (i,)),
                  pl.BlockSpec((tc, 2), lambda i: (i, 0)),
                  pl.BlockSpec((tc, d), lambda i: (i, 0)),
                  pl.BlockSpec((None, 1, mods.shape[-1]), lambda i: (geom.mod_row(i), 0, 0)),
                  pl.BlockSpec(memory_space=pl.ANY)],
        out_specs=pl.BlockSpec((tc, d), lambda i: (i, 0)),
        out_shape=jax.ShapeDtypeStruct((n_out, d), F32),
        scratch_shapes=[pltpu.VMEM((tc * SUBLANES, LANES), F32), pltpu.VMEM((tc * SUBLANES, LANES), F32),
                        pltpu.SemaphoreType.DMA((2,))],
        compiler_params=_cparams("arbitrary"),
        name="moe_combine",
    )(dest0, dest1, w_cols, xs, mods, y_sorted)


def _hier_moe(h3, xs, mods, geom, w_r1, b_r1, w_r2, b_r2, w13_all, w2_all, layer, n_out):
    n, d = xs.shape
    wt = jnp.zeros((ROUTER_ROWS, d), F32)
    wt = wt.at[0:N_GROUPS].set(w_r1.T)
    wt = wt.at[SUBLANES:].set(jnp.transpose(w_r2, (0, 2, 1)).reshape(N_EXPERTS, d))
    bias = jnp.zeros((ROUTER_ROWS,), F32).at[0:N_GROUPS].set(b_r1).at[SUBLANES:].set(b_r2.reshape(-1))
    bias = jnp.broadcast_to(bias[:, None], (ROUTER_ROWS, LANES))
    tri = jnp.asarray(np.triu(np.ones((ROUTER_TILE, ROUTER_TILE), np.float32), 1), BF16)
    ints, flt, cnt = _router(h3, wt, bias, tri)

    counts = cnt[:, 0].astype(jnp.int32)
    padded = ((counts + MOE_TILE - 1) // MOE_TILE) * MOE_TILE
    pend = jnp.cumsum(padded)
    pstart = pend - padded
    n_blocks = (2 * n) // MOE_TILE + N_EXPERTS
    n_rows = n_blocks * MOE_TILE
    blk_row0 = jnp.arange(n_blocks, dtype=jnp.int32) * MOE_TILE
    blk_exp = jnp.minimum(jnp.sum(pend[None, :] <= blk_row0[:, None], axis=1), N_EXPERTS - 1).astype(jnp.int32)

    n_used = (pend[-1] // MOE_TILE).astype(jnp.int32)
    pad_spans = jnp.concatenate([pstart + counts, pend[-1:], padded - counts, n_rows - pend[-1:]]).astype(jnp.int32)

    dest = _dest_rows(pstart.astype(jnp.int32), ints)
    row_map = _row_maps(pad_spans, dest[0], dest[1], n_rows)
    y_sorted = _experts(blk_exp, n_used[None], row_map, h3, w13_all, w2_all, layer)
    return _combine(dest[0], dest[1], flt[0:2].T, xs, mods, y_sorted, geom, n_out)


def _rope_tables(seq, identity_rows):
    ctx_len = identity_rows
    rows = seq // GRID_W
    row = jnp.repeat(jnp.arange(rows, dtype=F32), GRID_W)
    col = jnp.tile(jnp.arange(GRID_W, dtype=F32), rows)
    inv = ROPE_THETA ** (-jnp.arange(ROT_FREQS, dtype=F32) / ROT_FREQS)
    ang_r, ang_c = row[:, None] * inv, col[:, None] * inv
    cos_h = jnp.concatenate([jnp.cos(ang_r)] * 2 + [jnp.cos(ang_c)] * 2, axis=1)
    sin_h = jnp.concatenate([-jnp.sin(ang_r), jnp.sin(ang_r), -jnp.sin(ang_c), jnp.sin(ang_c)], axis=1)
    cos_t = jnp.concatenate([jnp.tile(cos_h, (1, 2)), jnp.ones((ctx_len, LANES), F32)], axis=0)
    sin_t = jnp.concatenate([jnp.tile(sin_h, (1, 2)), jnp.zeros((ctx_len, LANES), F32)], axis=0)
    return cos_t, sin_t


def _pad_heads_cols(w, n_heads):
    d = w.shape[0]
    w3 = w.reshape(d, n_heads, HEAD_DIM)
    return jnp.concatenate([w3, jnp.zeros_like(w3)], axis=2).reshape(d, n_heads * LANES)


def kernel(x, c, ctx, c_ctx, ada_w, ada_b, norm1_g, norm2_g, a_wqkv, a_wo, a_q_norm, a_k_norm, a_lambda_q1, a_lambda_k1, a_lambda_q2, a_lambda_k2, a_subln_g, b_wqkv, b_wo, b_q_norm, b_k_norm, b_sink, f_wo, r_w1, r_b1, r_w2, r_b2, e_w13, e_w2):
    batch, seq, d = x.shape
    ctx_len = ctx.shape[1]
    depth = ada_w.shape[0]
    n = batch * (seq + ctx_len)
    assert seq % TOKEN_TILE == 0 and ctx_len % TOKEN_TILE == 0 and n % ROUTER_TILE == 0
    geom = Geom(batch, seq, ctx_len)

    xs = jnp.concatenate([x.reshape(batch * seq, d), ctx.reshape(batch * ctx_len, d)], axis=0)
    mod_rows = ((batch + 1 + SUBLANES - 1) // SUBLANES) * SUBLANES
    cvec = jnp.zeros((mod_rows, d), F32).at[0:batch].set(c).at[batch].set(c_ctx)
    mods_all = _ada_table(cvec, ada_w, ada_b).reshape(depth, mod_rows, 1, 6 * d)
    cos_t, sin_t = _rope_tables(seq, PROJ_IN_TILE)

    for i in range(depth):
        kind, j = i % N_MIXERS, i // N_MIXERS
        mods = mods_all[i]
        if kind == 0:
            lam_init = 0.8 - 0.6 * math.exp(-0.3 * i)
            nh = d // LANES
            gains = jnp.stack([jnp.tile(a_q_norm[j], 2), jnp.tile(a_k_norm[j], 2)])
            q, k, v1 = _proj_in(xs, mods, norm1_g[i], a_wqkv[j].astype(BF16), gains, cos_t, sin_t, geom,
                                n_q=nh, n_k=nh, n_plain=nh, ones_after_plain=True)
            lam_vecs = jnp.stack([a_lambda_q1[j], a_lambda_k1[j], a_lambda_q2[j], a_lambda_k2[j]])
            o_lat, o_ctx = _diff_attention(q, k, v1, lam_vecs, a_subln_g[j], geom, lam_init)
            wo = a_wo[j].astype(BF16)
        elif kind == 1:
            q_heads = d // HEAD_DIM
            nq, nk = q_heads * HEAD_DIM, SWA_KV_HEADS * HEAD_DIM
            w = b_wqkv[j]
            w_pad = jnp.concatenate([_pad_heads_cols(w[:, :nq], q_heads),
                                     _pad_heads_cols(w[:, nq:nq + nk], SWA_KV_HEADS),
                                     _pad_heads_cols(w[:, nq + nk:], SWA_KV_HEADS)], axis=1).astype(BF16)
            zeros64 = jnp.zeros((HEAD_DIM,), F32)
            gains = jnp.stack([jnp.concatenate([b_q_norm[j], zeros64]), jnp.concatenate([b_k_norm[j], zeros64])])
            q, k, v = _proj_in(xs, mods, norm1_g[i], w_pad, gains, cos_t, sin_t, geom,
                               n_q=q_heads, n_k=SWA_KV_HEADS, n_plain=SWA_KV_HEADS)
            o_lat = o_ctx = _swa_attention(q, k, v, b_sink[j], geom)
            wo3 = b_wo[j].reshape(q_heads, HEAD_DIM, d)
            wo = jnp.concatenate([wo3, jnp.zeros_like(wo3)], axis=1).reshape(q_heads * LANES, d).astype(BF16)
        else:
            gd = d // FOURIER_GROUPS
            cd, msd = _dft_mats(gd, gd ** -0.5)
            eye = jnp.eye(FOURIER_GROUPS, dtype=BF16)
            w_cs = jnp.concatenate([jnp.kron(eye, cd), jnp.kron(eye, -msd)], axis=1)
            (z,) = _proj_in(xs, mods, norm1_g[i], w_cs, jnp.zeros((2, LANES), F32), cos_t, sin_t, geom,
                            n_q=0, n_k=0, n_plain=2 * d // LANES)
            o_lat = o_ctx = _seq_dft(z, _dft_mats(ctx_len, ctx_len ** -0.5), geom, d)
            wo = f_wo[j].astype(BF16)
        xs, h2 = _proj_out(o_lat, o_ctx, wo, xs, mods, norm2_g[i], geom)
        n_out = n if i < depth - 1 else batch * seq
        xs = _hier_moe(h2, xs, mods, geom, r_w1[i], r_b1[i], r_w2[i], r_b2[i], e_w13, e_w2, i, n_out)

    return xs.reshape(batch, seq, d)
```

```python
import functools
import math
from typing import NamedTuple

import numpy as np
import jax
import jax.numpy as jnp
from jax import lax
from jax.experimental import pallas as pl
from jax.experimental.pallas import tpu as pltpu

F32 = jnp.float32
BF16 = jnp.bfloat16
HIGHEST = lax.Precision.HIGHEST

GRID_W = 64
HEAD_DIM = 64
ROT_FREQS = HEAD_DIM // 4
ROPE_THETA = 10000.0
WINDOW = 128
N_MIXERS = 3
SWA_KV_HEADS = 4
FOURIER_GROUPS = 4
N_GROUPS = 4
EXPERTS_PER_GROUP = 8
N_EXPERTS = N_GROUPS * EXPERTS_PER_GROUP
EPS = 1e-6
NEG_INF = -1e30

LANES = 128
SUBLANES = 8
MXU_DIM = 256
TOKEN_TILE = 256
PROJ_IN_TILE = 512
MOE_HALF = 256
MOE_TILE = 2 * MOE_HALF
ROUTER_TILE = 512
VMEM_LIMIT = 48 * 1024 * 1024


def _cparams(*sem):
    return pltpu.CompilerParams(dimension_semantics=sem, vmem_limit_bytes=VMEM_LIMIT)


class Geom(NamedTuple):
    batch: int
    seq: int
    ctx_len: int

    @property
    def lat_tiles(self):
        return self.seq // TOKEN_TILE

    @property
    def n_lat_tiles(self):
        return self.batch * self.lat_tiles

    @property
    def n_tiles(self):
        return self.n_lat_tiles + self.batch * (self.ctx_len // TOKEN_TILE)

    def mod_row(self, i):
        return jnp.where(i < self.n_lat_tiles, i // self.lat_tiles, self.batch)

    def ctx_tile(self, b):
        return self.n_lat_tiles + b


def _ada_kernel(c_ref, w_ref, b_ref, o_ref):
    c = c_ref[...]
    s = c * (1.0 / (1.0 + jnp.exp(-c)))
    o_ref[...] = jnp.dot(s, w_ref[...], precision=HIGHEST, preferred_element_type=F32) + b_ref[...]


def _ada_table(cvec, ada_w, ada_b):
    depth, d, n6 = ada_w.shape
    rows = cvec.shape[0]
    tn = n6 // 4
    return pl.pallas_call(
        _ada_kernel,
        grid=(depth, n6 // tn),
        in_specs=[pl.BlockSpec((rows, d), lambda l, j: (0, 0)),
                  pl.BlockSpec((None, d, tn), lambda l, j: (l, 0, j)),
                  pl.BlockSpec((None, 1, tn), lambda l, j: (l, 0, j))],
        out_specs=pl.BlockSpec((None, rows, tn), lambda l, j: (l, 0, j)),
        out_shape=jax.ShapeDtypeStruct((depth, rows, n6), F32),
        compiler_params=_cparams("parallel", "parallel"),
        name="ada_table",
    )(cvec, ada_w, ada_b.reshape(depth, 1, n6))


def _head_mean_matrix():
    r = np.arange(LANES)
    return jnp.asarray((r[:, None] // HEAD_DIM == r[None, :] // HEAD_DIM).astype(np.float32) / HEAD_DIM, BF16)


def _proj_in_kernel(x_ref, mod_ref, g_ref, w_ref, hm_ref, gain_ref, cos_ref, sin_ref, *out_refs,
                    d, n_q, n_k, n_plain, ones_after_plain, q_scale):
    q_ref = out_refs[0] if n_q else None
    k_ref = out_refs[1] if n_k else None
    p_ref = out_refs[-2]
    y_ref = out_refs[-1]
    n_chunks = n_q + n_k + n_plain
    in_grid = pl.program_id(0) < pl.num_programs(0)

    @pl.when(in_grid)
    def _():
        x = x_ref[...]
        ms = jnp.mean(x * x, axis=-1, keepdims=True)
        h = x * lax.rsqrt(ms + EPS) * g_ref[...]
        h = h * (1.0 + mod_ref[:, d:2 * d]) + mod_ref[:, 0:d]
        hb = h.astype(BF16)
        for c2 in range(0, n_chunks, 2):
            cols = slice(c2 * LANES, (c2 + min(2, n_chunks - c2)) * LANES)
            y_ref[:, cols] = jnp.dot(hb, w_ref[:, cols], preferred_element_type=F32)

    @pl.when(in_grid)
    def _():
        lane = lax.broadcasted_iota(jnp.int32, (1, LANES), 1)
        first_half = (lane // ROT_FREQS) % 2 == 0
        for c in range(n_chunks):
            y = y_ref[:, c * LANES:(c + 1) * LANES]
            if c < n_q + n_k:
                msq = jnp.dot((y * y).astype(BF16), hm_ref[...], preferred_element_type=F32)
                is_q = c < n_q
                gain = gain_ref[0:1, :] if is_q else gain_ref[1:2, :]
                yn = y * lax.rsqrt(msq + EPS) * gain
                partner = jnp.where(first_half, pltpu.roll(yn, LANES - ROT_FREQS, 1), pltpu.roll(yn, ROT_FREQS, 1))
                out = yn * cos_ref[...] + partner * sin_ref[...]
                if is_q:
                    q_ref[:, c * LANES:(c + 1) * LANES] = (out * q_scale).astype(q_ref.dtype)
                else:
                    ck = c - n_q
                    k_ref[:, ck * LANES:(ck + 1) * LANES] = out.astype(k_ref.dtype)
            else:
                cp = c - n_q - n_k
                if ones_after_plain:
                    p_ref[:, 2 * cp * LANES:(2 * cp + 1) * LANES] = y.astype(p_ref.dtype)
                    p_ref[:, (2 * cp + 1) * LANES:(2 * cp + 2) * LANES] = jnp.ones(y.shape, p_ref.dtype)
                else:
                    p_ref[:, cp * LANES:(cp + 1) * LANES] = y.astype(p_ref.dtype)


Q_SCALE_LOG2 = HEAD_DIM ** -0.5 * math.log2(math.e)


def _proj_in(xs, mods, g, w, gains, cos_t, sin_t, geom, *, n_q, n_k, n_plain, ones_after_plain=False,
             q_scale=Q_SCALE_LOG2):
    n, d = xs.shape
    tm = PROJ_IN_TILE
    lat_tiles = geom.seq // tm
    n_lat = geom.batch * lat_tiles
    assert geom.seq % tm == 0 and n % tm == 0 and cos_t.shape[0] == geom.seq + tm

    def mod_row(i):
        return jnp.where(i < n_lat, i // lat_tiles, geom.batch)

    def rope_row(i):
        return jnp.where(i < n_lat, i % lat_tiles, lat_tiles)

    n_cols = (n_q + n_k + n_plain) * LANES
    assert w.shape == (d, n_cols)
    out_shape, out_specs = [], []
    for cnt in (n_q, n_k):
        if cnt:
            out_shape.append(jax.ShapeDtypeStruct((n, cnt * LANES), BF16))
            out_specs.append(pl.BlockSpec((tm, cnt * LANES), lambda i: (i, 0)))
    pw = n_plain * LANES * (2 if ones_after_plain else 1)
    out_shape.append(jax.ShapeDtypeStruct((n, pw), BF16))
    out_specs.append(pl.BlockSpec((tm, pw), lambda i: (i, 0)))
    kern = functools.partial(_proj_in_kernel, d=d, n_q=n_q, n_k=n_k, n_plain=n_plain,
                             ones_after_plain=ones_after_plain, q_scale=q_scale)
    return pl.pallas_call(
        kern,
        grid=(n // tm,),
        in_specs=[pl.BlockSpec((tm, d), lambda i: (i, 0)),
                  pl.BlockSpec((None, 1, mods.shape[-1]), lambda i: (mod_row(i), 0, 0)),
                  pl.BlockSpec((1, d), lambda i: (0, 0)),
                  pl.BlockSpec((d, n_cols), lambda i: (0, 0)),
                  pl.BlockSpec((LANES, LANES), lambda i: (0, 0)),
                  pl.BlockSpec((2, LANES), lambda i: (0, 0)),
                  pl.BlockSpec((tm, LANES), lambda i: (rope_row(i), 0)),
                  pl.BlockSpec((tm, LANES), lambda i: (rope_row(i), 0))],
        out_specs=out_specs,
        out_shape=out_shape,
        scratch_shapes=[pltpu.VMEM((tm, n_cols), F32)],
        compiler_params=_cparams("parallel"),
        name="proj_in",
    )(xs, mods, g.reshape(1, d), w, _head_mean_matrix(), gains, cos_t, sin_t)


DIFF_KV_CHUNK = 1024
DIFF_Q_TILE = 256
DIFF_HEADS_PER_STEP = 2


def _diff_attn_kernel(*refs, lam_init, n_kv, chunks, heads):
    lam_ref, q_ref = refs[0], refs[1]
    k_refs = refs[2:2 + n_kv]
    v_refs = refs[2 + n_kv:2 + 2 * n_kv]
    sg_ref, o_ref, s_ref, m_ref = refs[2 + 2 * n_kv:]
    dq = 2 * HEAD_DIM
    dv = o_ref.shape[1] // heads
    in_grid = pl.program_id(0) < pl.num_programs(0)

    @pl.when(in_grid)
    def _():
        for h in range(heads):
            q = q_ref[:, h * dq:(h + 1) * dq]
            lane = lax.broadcasted_iota(jnp.int32, q.shape, 1)
            zero = jnp.zeros_like(q)
            q_maps = (jnp.where(lane < HEAD_DIM, q, zero), jnp.where(lane >= HEAD_DIM, q, zero))
            for m in range(2):
                mx = None
                for slab, start, size, col in chunks:
                    s = lax.dot_general(q_maps[m], k_refs[slab][start:start + size, h * dq:(h + 1) * dq],
                                        (((1,), (1,)), ((), ())), preferred_element_type=F32)
                    s_ref[2 * h + m, :, col:col + size] = s
                    cm = jnp.max(s, axis=1, keepdims=True)
                    mx = cm if mx is None else jnp.maximum(mx, cm)
                m_ref[2 * h + m] = mx

    @pl.when(in_grid)
    def _():
        lam_v = lam_ref[...]
        lam = (jnp.exp(jnp.sum(lam_v[0:1] * lam_v[1:2], axis=1, keepdims=True))
               - jnp.exp(jnp.sum(lam_v[2:3] * lam_v[3:4], axis=1, keepdims=True)) + lam_init)
        for h in range(heads):
            outs = []
            for m in range(2):
                row_max = m_ref[2 * h + m]
                acc = None
                for slab, start, size, col in chunks:
                    p = jnp.exp2(s_ref[2 * h + m, :, col:col + size] - row_max)
                    pv = jnp.dot(p.astype(BF16), v_refs[slab][start:start + size, 2 * h * dv:2 * (h + 1) * dv],
                                 preferred_element_type=F32)
                    acc = pv if acc is None else acc + pv
                outs.append(acc[:, 0:dv] * (1.0 / acc[:, dv:dv + 1]))
            a = outs[0] - lam * outs[1]
            a = a * lax.rsqrt(jnp.mean(a * a, axis=-1, keepdims=True) + EPS) * sg_ref[...] * (1.0 - lam_init)
            o_ref[:, h * dv:(h + 1) * dv] = a.astype(o_ref.dtype)


def _diff_attention(q, k, v1, lam_vecs, sub_g, geom, lam_init):
    dq = q.shape[1]
    batch, seq, ctx_len = geom
    heads = dq // (2 * HEAD_DIM)
    dv = 2 * HEAD_DIM
    tq = DIFF_Q_TILE
    q_tiles = seq // tq
    ctx_blk = seq // ctx_len * batch
    lam_spec = pl.BlockSpec((4, HEAD_DIM), lambda *_: (0, 0))
    sg_spec = pl.BlockSpec((1, dv), lambda *_: (0, 0))
    kv_chunk = min(DIFF_KV_CHUNK, seq)
    assert seq % kv_chunk == 0
    chunks = [(0, c * kv_chunk, kv_chunk, c * kv_chunk) for c in range(seq // kv_chunk)]
    chunks.append((1, 0, ctx_len, seq))
    hp = DIFF_HEADS_PER_STEP
    assert heads % hp == 0
    o_lat = pl.pallas_call(
        functools.partial(_diff_attn_kernel, lam_init=lam_init, n_kv=2, chunks=chunks, heads=hp),
        grid=(batch, heads // hp, q_tiles),
        in_specs=[lam_spec,
                  pl.BlockSpec((tq, hp * 2 * HEAD_DIM), lambda b, h, i: (b * q_tiles + i, h)),
                  pl.BlockSpec((seq, hp * 2 * HEAD_DIM), lambda b, h, i: (b, h)),
                  pl.BlockSpec((ctx_len, hp * 2 * HEAD_DIM), lambda b, h, i: (ctx_blk + b, h)),
                  pl.BlockSpec((seq, hp * 2 * dv), lambda b, h, i: (b, h)),
                  pl.BlockSpec((ctx_len, hp * 2 * dv), lambda b, h, i: (ctx_blk + b, h)),
                  sg_spec],
        out_specs=pl.BlockSpec((tq, hp * dv), lambda b, h, i: (b * q_tiles + i, h)),
        out_shape=jax.ShapeDtypeStruct((batch * seq, heads * dv), BF16),
        scratch_shapes=[pltpu.VMEM((2 * hp, tq, seq + ctx_len), F32), pltpu.VMEM((2 * hp, tq, 1), F32)],
        compiler_params=_cparams("parallel", "parallel", "arbitrary"),
        name="diff_attention",
    )(lam_vecs, q, k, k, v1, v1, sub_g.reshape(1, dv))
    o_ctx = pl.pallas_call(
        functools.partial(_diff_attn_kernel, lam_init=lam_init, n_kv=1, chunks=[(0, 0, ctx_len, 0)], heads=1),
        grid=(batch, heads),
        in_specs=[lam_spec,
                  pl.BlockSpec((ctx_len, 2 * HEAD_DIM), lambda b, h: (ctx_blk + b, h)),
                  pl.BlockSpec((ctx_len, 2 * HEAD_DIM), lambda b, h: (ctx_blk + b, h)),
                  pl.BlockSpec((ctx_len, 2 * dv), lambda b, h: (ctx_blk + b, h)),
                  sg_spec],
        out_specs=pl.BlockSpec((ctx_len, dv), lambda b, h: (b, h)),
        out_shape=jax.ShapeDtypeStruct((batch * ctx_len, heads * dv), BF16),
        scratch_shapes=[pltpu.VMEM((2, ctx_len, ctx_len), F32), pltpu.VMEM((2, ctx_len, 1), F32)],
        compiler_params=_cparams("parallel", "parallel"),
        name="diff_attention_ctx",
    )(lam_vecs, q, k, v1, sub_g.reshape(1, dv))
    return o_lat, o_ctx


def _swa_kernel(sink_ref, q_ref, kl_ref, kc_ref, vl_ref, vc_ref, o_ref, *, latent_tiles, group):
    i = pl.program_id(1)
    tq = q_ref.shape[0]
    seq, ctx_len = kl_ref.shape[0], kc_ref.shape[0]
    span = tq + 2 * WINDOW
    start = pl.multiple_of(jnp.clip(i * tq - WINDOW, 0, seq - span), WINDOW)
    qpos = i * tq + lax.broadcasted_iota(jnp.int32, (tq, span), 0)
    kpos = start + lax.broadcasted_iota(jnp.int32, (tq, span), 1)
    valid = (jnp.abs(kpos - qpos) <= WINDOW) & (i < latent_tiles)
    bias = jnp.where(valid, 0.0, NEG_INF).astype(F32)
    nt = (((1,), (1,)), ((), ()))
    for h in range(SWA_KV_HEADS):
        hs = slice(h * LANES, (h + 1) * LANES)
        kw = kl_ref[pl.ds(start, span), hs]
        vw = vl_ref[pl.ds(start, span), hs]
        kc = kc_ref[:, hs]
        vc = vc_ref[:, hs]
        for g in range(group):
            hq = h * group + g
            q = q_ref[:, hq * LANES:(hq + 1) * LANES]
            s_w = lax.dot_general(q, kw, nt, preferred_element_type=F32) + bias
            s_c = lax.dot_general(q, kc, nt, preferred_element_type=F32)
            sink = sink_ref[hq] * math.log2(math.e)
            m = jnp.maximum(jnp.maximum(jnp.max(s_w, axis=1, keepdims=True), jnp.max(s_c, axis=1, keepdims=True)),
                            sink)
            p_w = jnp.exp2(s_w - m)
            p_c = jnp.exp2(s_c - m)
            denom = (jnp.sum(p_w, axis=1, keepdims=True) + jnp.sum(p_c, axis=1, keepdims=True)
                     + jnp.exp2(sink - m))
            o = (jnp.dot(p_w.astype(BF16), vw, preferred_element_type=F32)
                 + jnp.dot(p_c.astype(BF16), vc, preferred_element_type=F32)) * (1.0 / denom)
            o_ref[:, hq * LANES:(hq + 1) * LANES] = o.astype(o_ref.dtype)


def _swa_attention(q, k, v, sink, geom):
    n, dq = q.shape
    batch, seq, ctx_len = geom
    assert ctx_len == TOKEN_TILE
    lt = geom.lat_tiles
    q_heads = dq // LANES
    kvw = SWA_KV_HEADS * LANES
    ctx_blk = seq // ctx_len * batch
    kern = functools.partial(_swa_kernel, latent_tiles=lt, group=q_heads // SWA_KV_HEADS)

    def tile(b, i):
        return jnp.where(i < lt, b * lt + i, geom.ctx_tile(b))

    lat_kv = pl.BlockSpec((seq, kvw), lambda b, i: (b, 0))
    ctx_kv = pl.BlockSpec((ctx_len, kvw), lambda b, i: (ctx_blk + b, 0))
    return pl.pallas_call(
        kern,
        grid=(batch, lt + 1),
        in_specs=[pl.BlockSpec(memory_space=pltpu.SMEM),
                  pl.BlockSpec((TOKEN_TILE, dq), lambda b, i: (tile(b, i), 0)),
                  lat_kv, ctx_kv, lat_kv, ctx_kv],
        out_specs=pl.BlockSpec((TOKEN_TILE, dq), lambda b, i: (tile(b, i), 0)),
        out_shape=jax.ShapeDtypeStruct((n, dq), BF16),
        compiler_params=_cparams("parallel", "arbitrary"),
        name="swa_attention",
    )(sink, q, k, k, v, v)


def _seq_dft_kernel(base_ref, fine_c_ref, fine_s_ref, cc_ref, cs_ref, zc_ref, zs_ref, yc_ref, ys_ref, o_ref,
                    *, latent_tiles):
    m = pl.program_id(2)

    def mix(a_cos, a_msin, z_cos, z_sin):
        acc = jnp.dot(a_cos, z_cos[...], preferred_element_type=F32)
        acc = acc + jnp.dot(a_msin, z_sin[...], preferred_element_type=F32)
        o_ref[...] = acc.astype(o_ref.dtype)

    @pl.when(m < latent_tiles)
    def _():
        c1, s1 = base_ref[0:1, :], base_ref[1:2, :]
        c2, s2 = fine_c_ref[...], fine_s_ref[...]
        a_cos = (c2 * c1 - s2 * s1).astype(BF16)
        a_msin = (-(c2 * s1) - s2 * c1).astype(BF16)
        mix(a_cos, a_msin, zc_ref, zs_ref)

    @pl.when(m >= latent_tiles)
    def _():
        mix(cc_ref[...], cs_ref[...], yc_ref, ys_ref)


def _dft_angle_tables(n, rows, scale):
    k = jnp.arange(n, dtype=jnp.int32)[None, :]
    step = 2.0 * math.pi / n
    ang0 = (((jnp.arange(n // rows, dtype=jnp.int32) * rows)[:, None] * k) % n).astype(F32) * step
    ang1 = ((jnp.arange(rows, dtype=jnp.int32)[:, None] * k) % n).astype(F32) * step
    base = jnp.stack([jnp.cos(ang0), jnp.sin(ang0)], axis=1)
    return base, jnp.cos(ang1) * scale, jnp.sin(ang1) * scale


def _seq_dft(z, ctx_mats, geom, d):
    batch, seq, ctx_len = geom
    assert ctx_len == TOKEN_TILE and seq % ctx_len == 0
    lt = geom.lat_tiles
    tm = TOKEN_TILE
    tn = d // 2
    n_col = d // tn
    ctx_blk = seq // ctx_len * batch
    lat_tables = _dft_angle_tables(seq, tm, seq ** -0.5)
    fine = pl.BlockSpec((tm, seq), lambda b, j, m: (0, 0))
    ctx_a = pl.BlockSpec((ctx_len, ctx_len), lambda b, j, m: (0, 0))
    return pl.pallas_call(
        functools.partial(_seq_dft_kernel, latent_tiles=lt),
        grid=(batch, n_col, lt + 1),
        in_specs=[pl.BlockSpec((None, 2, seq), lambda b, j, m: (jnp.minimum(m, lt - 1), 0, 0)),
                  fine, fine, ctx_a, ctx_a,
                  pl.BlockSpec((seq, tn), lambda b, j, m: (b, j)),
                  pl.BlockSpec((seq, tn), lambda b, j, m: (b, n_col + j)),
                  pl.BlockSpec((ctx_len, tn), lambda b, j, m: (ctx_blk + b, j)),
                  pl.BlockSpec((ctx_len, tn), lambda b, j, m: (ctx_blk + b, n_col + j))],
        out_specs=pl.BlockSpec((tm, tn), lambda b, j, m: (jnp.where(m < lt, b * lt + m, geom.ctx_tile(b)), j)),
        out_shape=jax.ShapeDtypeStruct((z.shape[0], d), BF16),
        compiler_params=_cparams("parallel", "parallel", "arbitrary"),
        name="seq_dft",
    )(*lat_tables, *ctx_mats, z, z, z, z)


def _dft_mats(n, scale):
    idx = (jnp.arange(n, dtype=jnp.int32)[:, None] * jnp.arange(n, dtype=jnp.int32)[None, :]) % n
    ang = idx.astype(F32) * (2.0 * math.pi / n)
    return (jnp.cos(ang) * scale).astype(BF16), (-jnp.sin(ang) * scale).astype(BF16)


def _load_token_tiles(ref):
    rows = ref.shape[0] // SUBLANES
    return jnp.concatenate([ref[pl.ds(s, rows, stride=SUBLANES), :] for s in range(SUBLANES)], axis=1)


def _store_token_tiles(ref, val):
    rows = ref.shape[0] // SUBLANES
    for s in range(SUBLANES):
        ref[pl.ds(s, rows, stride=SUBLANES), :] = val[:, s * LANES:(s + 1) * LANES]


def _proj_out_kernel(ol_ref, oc_ref, w_ref, x_ref, mod_ref, g_ref, xo_ref, h_ref, *, d, n_lat_tiles):
    def finish(o_ref):
        y = jnp.dot(o_ref[...], w_ref[...], preferred_element_type=F32)
        x = x_ref[...] + mod_ref[:, 2 * d:3 * d] * y
        xo_ref[...] = x
        ms = jnp.mean(x * x, axis=-1, keepdims=True)
        h = x * lax.rsqrt(ms + EPS) * g_ref[...]
        _store_token_tiles(h_ref, h * (1.0 + mod_ref[:, 4 * d:5 * d]) + mod_ref[:, 3 * d:4 * d])

    @pl.when(pl.program_id(0) < n_lat_tiles)
    def _():
        finish(ol_ref)

    @pl.when(pl.program_id(0) >= n_lat_tiles)
    def _():
        finish(oc_ref)


def _proj_out(o_lat, o_ctx, w, xs, mods, g2, geom):
    n, d = xs.shape
    ko = o_lat.shape[1]
    tm = TOKEN_TILE
    nl = geom.n_lat_tiles
    ctx_off = nl if o_ctx.shape[0] == n else 0
    return pl.pallas_call(
        functools.partial(_proj_out_kernel, d=d, n_lat_tiles=nl),
        grid=(n // tm,),
        in_specs=[pl.BlockSpec((tm, ko), lambda i: (jnp.minimum(i, nl - 1), 0)),
                  pl.BlockSpec((tm, ko), lambda i: (jnp.maximum(i, nl) - nl + ctx_off, 0)),
                  pl.BlockSpec((ko, d), lambda i: (0, 0)),
                  pl.BlockSpec((tm, d), lambda i: (i, 0)),
                  pl.BlockSpec((None, 1, mods.shape[-1]), lambda i: (geom.mod_row(i), 0, 0)),
                  pl.BlockSpec((1, d), lambda i: (0, 0))],
        out_specs=[pl.BlockSpec((tm, d), lambda i: (i, 0)),
                   pl.BlockSpec((tm * SUBLANES, LANES), lambda i: (i, 0))],
        out_shape=[jax.ShapeDtypeStruct((n, d), F32), jax.ShapeDtypeStruct((n * SUBLANES, LANES), F32)],
        compiler_params=_cparams("parallel"),
        name="proj_out",
    )(o_lat, o_ctx, w, xs, mods, g2.reshape(1, d))


ROUTER_ROWS = SUBLANES + N_EXPERTS


def _router_kernel(h_ref, wt_ref, b_ref, tri_ref, ints_ref, flt_ref, cnt_ref, carry_ref):
    step = pl.program_id(0)
    tr = h_ref.shape[0] // SUBLANES

    @pl.when(step == 0)
    def _():
        carry_ref[...] = jnp.zeros(carry_ref.shape, F32)

    logits = lax.dot_general(wt_ref[...], _load_token_tiles(h_ref), (((1,), (1,)), ((), ())),
                             precision=HIGHEST, preferred_element_type=F32) + b_ref[:, 0:1]
    row8 = lax.broadcasted_iota(jnp.int32, (SUBLANES, tr), 0)
    lg = jnp.where(row8 < N_GROUPS, logits[0:SUBLANES], NEG_INF)
    lg_max = jnp.max(lg, axis=0, keepdims=True)
    pg = 1.0 / jnp.sum(jnp.exp(lg - lg_max), axis=0, keepdims=True)
    grp = jnp.min(jnp.where(lg == lg_max, row8, SUBLANES), axis=0, keepdims=True)
    l2 = jnp.zeros((EXPERTS_PER_GROUP, tr), F32)
    for g in range(N_GROUPS):
        lo = SUBLANES + g * EXPERTS_PER_GROUP
        l2 = l2 + jnp.where(grp == g, logits[lo:lo + EXPERTS_PER_GROUP], 0.0)
    l2_max = jnp.max(l2, axis=0, keepdims=True)
    j0 = jnp.min(jnp.where(l2 == l2_max, row8, SUBLANES), axis=0, keepdims=True)
    rest = jnp.where(row8 == j0, NEG_INF, l2)
    r_max = jnp.max(rest, axis=0, keepdims=True)
    j1 = jnp.min(jnp.where(rest == r_max, row8, SUBLANES), axis=0, keepdims=True)
    e1 = jnp.exp(r_max - l2_max)
    inv = 1.0 / (1.0 + e1)
    w0 = pg * inv
    w1 = pg * e1 * inv
    ex0 = grp * EXPERTS_PER_GROUP + j0
    ex1 = grp * EXPERTS_PER_GROUP + j1

    rows = lax.broadcasted_iota(jnp.int32, (N_EXPERTS, tr), 0)
    oh0 = (rows == ex0).astype(F32)
    oh1 = (rows == ex1).astype(F32)
    both = oh0 + oh1
    before = jnp.dot(both.astype(BF16), tri_ref[...], preferred_element_type=F32) + carry_ref[:, 0:1]
    rank0 = jnp.sum(oh0 * before, axis=0, keepdims=True)
    rank1 = jnp.sum(oh1 * before, axis=0, keepdims=True)
    carry_ref[...] = carry_ref[...] + jnp.sum(both, axis=1, keepdims=True)
    cnt_ref[...] = carry_ref[...]

    zi = jnp.zeros((SUBLANES - 4, tr), jnp.int32)
    ints_ref[...] = jnp.concatenate([ex0, ex1, rank0.astype(jnp.int32), rank1.astype(jnp.int32), zi], axis=0)
    flt_ref[...] = jnp.concatenate([w0, w1, jnp.zeros((SUBLANES - 2, tr), F32)], axis=0)


def _router(h3, wt, bias, tri):
    n = h3.shape[0] // SUBLANES
    d = wt.shape[1]
    tr = ROUTER_TILE
    return pl.pallas_call(
        _router_kernel,
        grid=(n // tr,),
        in_specs=[pl.BlockSpec((tr * SUBLANES, LANES), lambda i: (i, 0)),
                  pl.BlockSpec((ROUTER_ROWS, d), lambda i: (0, 0)),
                  pl.BlockSpec((ROUTER_ROWS, LANES), lambda i: (0, 0)),
                  pl.BlockSpec((tr, tr), lambda i: (0, 0))],
        out_specs=[pl.BlockSpec((SUBLANES, tr), lambda i: (0, i)),
                   pl.BlockSpec((SUBLANES, tr), lambda i: (0, i)),
                   pl.BlockSpec((N_EXPERTS, LANES), lambda i: (0, 0))],
        out_shape=[jax.ShapeDtypeStruct((SUBLANES, n), jnp.int32),
                   jax.ShapeDtypeStruct((SUBLANES, n), F32),
                   jax.ShapeDtypeStruct((N_EXPERTS, LANES), F32)],
        scratch_shapes=[pltpu.VMEM((N_EXPERTS, LANES), F32)],
        compiler_params=_cparams("arbitrary"),
        name="router",
    )(h3, wt, bias, tri)


def _dest_kernel(pstart_ref, ints_ref, o_ref):
    ints = ints_ref[...]
    ex = ints[0:2]
    base = jnp.zeros(ex.shape, jnp.int32)
    for e in range(N_EXPERTS):
        base = jnp.where(ex == e, pstart_ref[e], base)
    o_ref[...] = jnp.concatenate([base + ints[2:4], jnp.zeros((SUBLANES - 2, ints.shape[1]), jnp.int32)], axis=0)


def _dest_rows(pstart, ints):
    n = ints.shape[1]
    tn = 2048 if n % 2048 == 0 else ROUTER_TILE
    return pl.pallas_call(
        _dest_kernel,
        grid=(n // tn,),
        in_specs=[pl.BlockSpec(memory_space=pltpu.SMEM),
                  pl.BlockSpec((SUBLANES, tn), lambda i: (0, i))],
        out_specs=pl.BlockSpec((SUBLANES, tn), lambda i: (0, i)),
        out_shape=jax.ShapeDtypeStruct((SUBLANES, n), jnp.int32),
        compiler_params=_cparams("parallel"),
        name="dest_rows",
    )(pstart, ints)


def _row_maps_kernel(pad_ref, d0_ref, d1_ref, out_ref, *, n_tokens):
    step = pl.program_id(0)
    td = d0_ref.shape[0]
    n_spans = pad_ref.shape[0] // 2

    @pl.when(step == 0)
    def _():
        def span(e, carry):
            def fill(r, c):
                row = pad_ref[e] + r
                out_ref[row] = 2 * n_tokens + (row & (MOE_HALF - 1))
                return c
            lax.fori_loop(0, pad_ref[n_spans + e], fill, 0)
            return carry
        lax.fori_loop(0, n_spans, span, 0)

    base = step * td

    def place(r, carry):
        n = base + r
        out_ref[d0_ref[r]] = n
        out_ref[d1_ref[r]] = n_tokens + n
        return carry
    lax.fori_loop(0, td, place, 0, unroll=8)


def _row_maps(pad_spans, dest0, dest1, n_rows):
    n = dest0.shape[0]
    td = 2048 if n % 2048 == 0 else ROUTER_TILE
    smem = functools.partial(pl.BlockSpec, memory_space=pltpu.SMEM)
    return pl.pallas_call(
        functools.partial(_row_maps_kernel, n_tokens=n),
        grid=(n // td,),
        in_specs=[smem(), smem((td,), lambda i: (i,)), smem((td,), lambda i: (i,))],
        out_specs=smem(),
        out_shape=jax.ShapeDtypeStruct((n_rows,), jnp.int32),
        compiler_params=_cparams("arbitrary"),
        name="moe_row_maps",
    )(pad_spans, dest0, dest1)


def _expert_kernel(blk_exp_ref, n_used_ref, map_first, map_half1, map_next, map_prev, map_half0, map_last,
                   h_ref, w13_ref, w2_ref, y_ref, xbuf0, xbuf1, ybuf0, ybuf1, w13b, w2b, gsem, ssem,
                   *, d_expert, n_tokens):
    j = pl.program_id(0)
    n_used = n_used_ref[0]
    xbuf = (xbuf0, xbuf1)
    ybuf = (ybuf0, ybuf1)

    def tile_of(ref, r):
        return ref.at[pl.ds(pl.multiple_of(r * SUBLANES, SUBLANES), SUBLANES)]

    def gather_row(idx_ref, buf, r):
        out_row = idx_ref[r]
        tok = out_row - jnp.where(out_row >= 2 * n_tokens, 2 * n_tokens, jnp.where(out_row >= n_tokens, n_tokens, 0))
        pltpu.make_async_copy(tile_of(h_ref, tok), tile_of(xbuf[buf], r), gsem.at[buf]).start()

    def scatter_row(idx_ref, buf, r):
        pltpu.make_async_copy(tile_of(ybuf[buf], r), tile_of(y_ref, idx_ref[r]), ssem.at[buf]).start()

    def wait_gather(buf):
        pltpu.make_async_copy(h_ref.at[pl.ds(0, MOE_HALF * SUBLANES)], xbuf[buf], gsem.at[buf]).wait()

    def wait_scatter(buf):
        pltpu.make_async_copy(ybuf[buf], y_ref.at[pl.ds(0, MOE_HALF * SUBLANES)], ssem.at[buf]).wait()

    def half_block(buf, gather_idx, scatter_idx):
        for r in range(MOE_HALF):
            gather_row(gather_idx, 1 - buf, r)
            scatter_row(scatter_idx, 1 - buf, r)
        x = _load_token_tiles(xbuf[buf]).astype(BF16)
        gu = jnp.dot(x, w13b[...], preferred_element_type=F32)
        g = gu[:, 0:d_expert]
        u = gu[:, d_expert:2 * d_expert]
        a = g * (1.0 / (1.0 + jnp.exp(-g))) * u
        _store_token_tiles(ybuf[buf], jnp.dot(a.astype(BF16), w2b[...], preferred_element_type=F32))

    @pl.when(j < n_used)
    def _():
        @pl.when(j == 0)
        def _():
            ybuf0[...] = jnp.zeros(ybuf0.shape, F32)
            ybuf1[...] = jnp.zeros(ybuf1.shape, F32)

            def first_rows(r, carry):
                pltpu.make_async_copy(tile_of(ybuf0, r), tile_of(y_ref, 2 * n_tokens + r), ssem.at[0]).start()
                gather_row(map_first, 0, r)
                return carry
            lax.fori_loop(0, MOE_HALF, first_rows, 0)
            wait_scatter(0)

        @pl.when((j == 0) | (blk_exp_ref[j] != blk_exp_ref[jnp.maximum(j - 1, 0)]))
        def _():
            w13b[...] = w13_ref[...].astype(BF16)
            w2b[...] = w2_ref[...].astype(BF16)

        wait_gather(0)

        @pl.when(j > 0)
        def _():
            wait_scatter(0)

        half_block(0, map_half1, map_prev)
        wait_gather(1)
        wait_scatter(1)
        half_block(1, map_next, map_half0)

        @pl.when(j == n_used - 1)
        def _():
            wait_gather(0)
            wait_scatter(0)

            def last_scatter(r, carry):
                scatter_row(map_last, 1, r)
                return carry
            lax.fori_loop(0, MOE_HALF, last_scatter, 0)
            wait_scatter(1)


def _experts(blk_exp, n_used, row_map, h3, w13_all, w2_all, layer):
    n = h3.shape[0] // SUBLANES
    d, d_expert = w13_all.shape[2], w2_all.shape[2]
    n_steps = row_map.shape[0] // MOE_TILE
    smem = functools.partial(pl.BlockSpec, memory_space=pltpu.SMEM)

    def half_spec(half_of):
        def index(j, be, nu):
            last = 2 * nu[0] - 1
            return (jnp.clip(half_of(jnp.minimum(j, nu[0] - 1), last), 0, last),)
        return smem((MOE_HALF,), index)

    def expert(j, be, nu):
        return (layer, be[jnp.minimum(j, nu[0] - 1)], 0, 0)

    token_tile = (MOE_HALF * SUBLANES, LANES)
    grid_spec = pltpu.PrefetchScalarGridSpec(
        num_scalar_prefetch=2,
        grid=(n_steps,),
        in_specs=[half_spec(lambda j, last: 0),
                  half_spec(lambda j, last: 2 * j + 1),
                  half_spec(lambda j, last: 2 * j + 2),
                  half_spec(lambda j, last: 2 * j - 1),
                  half_spec(lambda j, last: 2 * j),
                  half_spec(lambda j, last: last),
                  pl.BlockSpec(memory_space=pl.ANY),
                  pl.BlockSpec((None, None, d, 2 * d_expert), expert),
                  pl.BlockSpec((None, None, d_expert, d), expert)],
        out_specs=pl.BlockSpec(memory_space=pl.ANY),
        scratch_shapes=[pltpu.VMEM(token_tile, F32), pltpu.VMEM(token_tile, F32),
                        pltpu.VMEM(token_tile, F32), pltpu.VMEM(token_tile, F32),
                        pltpu.VMEM((d, 2 * d_expert), BF16), pltpu.VMEM((d_expert, d), BF16),
                        pltpu.SemaphoreType.DMA((2,)), pltpu.SemaphoreType.DMA((2,))],
    )
    return pl.pallas_call(
        functools.partial(_expert_kernel, d_expert=d_expert, n_tokens=n),
        grid_spec=grid_spec,
        out_shape=jax.ShapeDtypeStruct(((2 * n + MOE_HALF) * SUBLANES, LANES), F32),
        compiler_params=_cparams("arbitrary"),
        name="moe_experts",
    )(blk_exp, n_used, row_map, row_map, row_map, row_map, row_map, row_map, h3, w13_all, w2_all)


def _combine_kernel(w_ref, x_ref, mod_ref, y0_ref, y1_ref, o_ref, *, d):
    out = w_ref[:, 0:1] * _load_token_tiles(y0_ref) + w_ref[:, 1:2] * _load_token_tiles(y1_ref)
    o_ref[...] = x_ref[...] + mod_ref[:, 5 * d:6 * d] * out


def _combine(w_cols, xs, mods, y_tok, geom, n_out):
    n, d = xs.shape
    tc = TOKEN_TILE
    return pl.pallas_call(
        functools.partial(_combine_kernel, d=d),
        grid=(n_out // tc,),
        in_specs=[pl.BlockSpec((tc, 2), lambda i: (i, 0)),
                  pl.BlockSpec((tc, d), lambda i: (i, 0)),
                  pl.BlockSpec((None, 1, mods.shape[-1]), lambda i: (geom.mod_row(i), 0, 0)),
                  pl.BlockSpec((tc * SUBLANES, LANES), lambda i: (i, 0)),
                  pl.BlockSpec((tc * SUBLANES, LANES), lambda i: (n // tc + i, 0))],
        out_specs=pl.BlockSpec((tc, d), lambda i: (i, 0)),
        out_shape=jax.ShapeDtypeStruct((n_out, d), F32),
        compiler_params=_cparams("parallel"),
        name="moe_combine",
    )(w_cols, xs, mods, y_tok, y_tok)


def _hier_moe(h3, xs, mods, geom, w_r1, b_r1, w_r2, b_r2, w13_all, w2_all, layer, n_out):
    n, d = xs.shape
    wt = jnp.zeros((ROUTER_ROWS, d), F32)
    wt = wt.at[0:N_GROUPS].set(w_r1.T)
    wt = wt.at[SUBLANES:].set(jnp.transpose(w_r2, (0, 2, 1)).reshape(N_EXPERTS, d))
    bias = jnp.zeros((ROUTER_ROWS,), F32).at[0:N_GROUPS].set(b_r1).at[SUBLANES:].set(b_r2.reshape(-1))
    bias = jnp.broadcast_to(bias[:, None], (ROUTER_ROWS, LANES))
    tri = jnp.asarray(np.triu(np.ones((ROUTER_TILE, ROUTER_TILE), np.float32), 1), BF16)
    ints, flt, cnt = _router(h3, wt, bias, tri)

    counts = cnt[:, 0].astype(jnp.int32)
    padded = ((counts + MOE_TILE - 1) // MOE_TILE) * MOE_TILE
    pend = jnp.cumsum(padded)
    pstart = pend - padded
    n_blocks = (2 * n) // MOE_TILE + N_EXPERTS
    n_rows = n_blocks * MOE_TILE
    blk_row0 = jnp.arange(n_blocks, dtype=jnp.int32) * MOE_TILE
    blk_exp = jnp.minimum(jnp.sum(pend[None, :] <= blk_row0[:, None], axis=1), N_EXPERTS - 1).astype(jnp.int32)

    n_used = (pend[-1] // MOE_TILE).astype(jnp.int32)
    pad_spans = jnp.concatenate([pstart + counts, pend[-1:], padded - counts, n_rows - pend[-1:]]).astype(jnp.int32)

    dest = _dest_rows(pstart.astype(jnp.int32), ints)
    row_map = _row_maps(pad_spans, dest[0], dest[1], n_rows)
    y_tok = _experts(blk_exp, n_used[None], row_map, h3, w13_all, w2_all, layer)
    return _combine(flt[0:2].T, xs, mods, y_tok, geom, n_out)


def _rope_tables(seq, identity_rows):
    ctx_len = identity_rows
    rows = seq // GRID_W
    row = jnp.repeat(jnp.arange(rows, dtype=F32), GRID_W)
    col = jnp.tile(jnp.arange(GRID_W, dtype=F32), rows)
    inv = ROPE_THETA ** (-jnp.arange(ROT_FREQS, dtype=F32) / ROT_FREQS)
    ang_r, ang_c = row[:, None] * inv, col[:, None] * inv
    cos_h = jnp.concatenate([jnp.cos(ang_r)] * 2 + [jnp.cos(ang_c)] * 2, axis=1)
    sin_h = jnp.concatenate([-jnp.sin(ang_r), jnp.sin(ang_r), -jnp.sin(ang_c), jnp.sin(ang_c)], axis=1)
    cos_t = jnp.concatenate([jnp.tile(cos_h, (1, 2)), jnp.ones((ctx_len, LANES), F32)], axis=0)
    sin_t = jnp.concatenate([jnp.tile(sin_h, (1, 2)), jnp.zeros((ctx_len, LANES), F32)], axis=0)
    return cos_t, sin_t


def _pad_heads_cols(w, n_heads):
    d = w.shape[0]
    w3 = w.reshape(d, n_heads, HEAD_DIM)
    return jnp.concatenate([w3, jnp.zeros_like(w3)], axis=2).reshape(d, n_heads * LANES)


def kernel(x, c, ctx, c_ctx, ada_w, ada_b, norm1_g, norm2_g, a_wqkv, a_wo, a_q_norm, a_k_norm, a_lambda_q1, a_lambda_k1, a_lambda_q2, a_lambda_k2, a_subln_g, b_wqkv, b_wo, b_q_norm, b_k_norm, b_sink, f_wo, r_w1, r_b1, r_w2, r_b2, e_w13, e_w2):
    batch, seq, d = x.shape
    ctx_len = ctx.shape[1]
    depth = ada_w.shape[0]
    n = batch * (seq + ctx_len)
    assert seq % TOKEN_TILE == 0 and ctx_len % TOKEN_TILE == 0 and n % ROUTER_TILE == 0
    geom = Geom(batch, seq, ctx_len)

    xs = jnp.concatenate([x.reshape(batch * seq, d), ctx.reshape(batch * ctx_len, d)], axis=0)
    mod_rows = ((batch + 1 + SUBLANES - 1) // SUBLANES) * SUBLANES
    cvec = jnp.zeros((mod_rows, d), F32).at[0:batch].set(c).at[batch].set(c_ctx)
    mods_all = _ada_table(cvec, ada_w, ada_b).reshape(depth, mod_rows, 1, 6 * d)
    cos_t, sin_t = _rope_tables(seq, PROJ_IN_TILE)

    for i in range(depth):
        kind, j = i % N_MIXERS, i // N_MIXERS
        mods = mods_all[i]
        if kind == 0:
            lam_init = 0.8 - 0.6 * math.exp(-0.3 * i)
            nh = d // LANES
            gains = jnp.stack([jnp.tile(a_q_norm[j], 2), jnp.tile(a_k_norm[j], 2)])
            q, k, v1 = _proj_in(xs, mods, norm1_g[i], a_wqkv[j].astype(BF16), gains, cos_t, sin_t, geom,
                                n_q=nh, n_k=nh, n_plain=nh, ones_after_plain=True)
            lam_vecs = jnp.stack([a_lambda_q1[j], a_lambda_k1[j], a_lambda_q2[j], a_lambda_k2[j]])
            o_lat, o_ctx = _diff_attention(q, k, v1, lam_vecs, a_subln_g[j], geom, lam_init)
            wo = a_wo[j].astype(BF16)
        elif kind == 1:
            q_heads = d // HEAD_DIM
            nq, nk = q_heads * HEAD_DIM, SWA_KV_HEADS * HEAD_DIM
            w = b_wqkv[j]
            w_pad = jnp.concatenate([_pad_heads_cols(w[:, :nq], q_heads),
                                     _pad_heads_cols(w[:, nq:nq + nk], SWA_KV_HEADS),
                                     _pad_heads_cols(w[:, nq + nk:], SWA_KV_HEADS)], axis=1).astype(BF16)
            zeros64 = jnp.zeros((HEAD_DIM,), F32)
            gains = jnp.stack([jnp.concatenate([b_q_norm[j], zeros64]), jnp.concatenate([b_k_norm[j], zeros64])])
            q, k, v = _proj_in(xs, mods, norm1_g[i], w_pad, gains, cos_t, sin_t, geom,
                               n_q=q_heads, n_k=SWA_KV_HEADS, n_plain=SWA_KV_HEADS)
            o_lat = o_ctx = _swa_attention(q, k, v, b_sink[j], geom)
            wo3 = b_wo[j].reshape(q_heads, HEAD_DIM, d)
            wo = jnp.concatenate([wo3, jnp.zeros_like(wo3)], axis=1).reshape(q_heads * LANES, d).astype(BF16)
        else:
            gd = d // FOURIER_GROUPS
            cd, msd = _dft_mats(gd, gd ** -0.5)
            eye = jnp.eye(FOURIER_GROUPS, dtype=BF16)
            w_cs = jnp.concatenate([jnp.kron(eye, cd), jnp.kron(eye, -msd)], axis=1)
            (z,) = _proj_in(xs, mods, norm1_g[i], w_cs, jnp.zeros((2, LANES), F32), cos_t, sin_t, geom,
                            n_q=0, n_k=0, n_plain=2 * d // LANES)
            o_lat = o_ctx = _seq_dft(z, _dft_mats(ctx_len, ctx_len ** -0.5), geom, d)
            wo = f_wo[j].astype(BF16)
        xs, h2 = _proj_out(o_lat, o_ctx, wo, xs, mods, norm2_g[i], geom)
        n_out = n if i < depth - 1 else batch * seq
        xs = _hier_moe(h2, xs, mods, geom, r_w1[i], r_b1[i], r_w2[i], r_b2[i], e_w13, e_w2, i, n_out)

    return xs.reshape(batch, seq, d)
```

```python
import functools
import math
from typing import NamedTuple

import numpy as np
import jax
import jax.numpy as jnp
from jax import lax
from jax.experimental import pallas as pl
from jax.experimental.pallas import tpu as pltpu

F32 = jnp.float32
BF16 = jnp.bfloat16
HIGHEST = lax.Precision.HIGHEST

GRID_W = 64
HEAD_DIM = 64
ROT_FREQS = HEAD_DIM // 4
ROPE_THETA = 10000.0
WINDOW = 128
N_MIXERS = 3
SWA_KV_HEADS = 4
FOURIER_GROUPS = 4
N_GROUPS = 4
EXPERTS_PER_GROUP = 8
N_EXPERTS = N_GROUPS * EXPERTS_PER_GROUP
EPS = 1e-6
NEG_INF = -1e30

LANES = 128
SUBLANES = 8
MXU_DIM = 256
TOKEN_TILE = 256
PROJ_IN_TILE = 512
MOE_HALF = 256
MOE_TILE = 2 * MOE_HALF
ROUTER_TILE = 512
VMEM_LIMIT = 48 * 1024 * 1024


def _cparams(*sem):
    return pltpu.CompilerParams(dimension_semantics=sem, vmem_limit_bytes=VMEM_LIMIT)


class Geom(NamedTuple):
    batch: int
    seq: int
    ctx_len: int

    @property
    def lat_tiles(self):
        return self.seq // TOKEN_TILE

    @property
    def n_lat_tiles(self):
        return self.batch * self.lat_tiles

    @property
    def n_tiles(self):
        return self.n_lat_tiles + self.batch * (self.ctx_len // TOKEN_TILE)

    def mod_row(self, i):
        return jnp.where(i < self.n_lat_tiles, i // self.lat_tiles, self.batch)

    def ctx_tile(self, b):
        return self.n_lat_tiles + b


def _ada_kernel(c_ref, w_ref, b_ref, o_ref):
    c = c_ref[...]
    s = c * (1.0 / (1.0 + jnp.exp(-c)))
    o_ref[...] = jnp.dot(s, w_ref[...], precision=HIGHEST, preferred_element_type=F32) + b_ref[...]


def _ada_table(cvec, ada_w, ada_b):
    depth, d, n6 = ada_w.shape
    rows = cvec.shape[0]
    tn = n6 // 4
    return pl.pallas_call(
        _ada_kernel,
        grid=(depth, n6 // tn),
        in_specs=[pl.BlockSpec((rows, d), lambda l, j: (0, 0)),
                  pl.BlockSpec((None, d, tn), lambda l, j: (l, 0, j)),
                  pl.BlockSpec((None, 1, tn), lambda l, j: (l, 0, j))],
        out_specs=pl.BlockSpec((None, rows, tn), lambda l, j: (l, 0, j)),
        out_shape=jax.ShapeDtypeStruct((depth, rows, n6), F32),
        compiler_params=_cparams("parallel", "parallel"),
        name="ada_table",
    )(cvec, ada_w, ada_b.reshape(depth, 1, n6))


def _head_mean_matrix():
    r = np.arange(LANES)
    return jnp.asarray((r[:, None] // HEAD_DIM == r[None, :] // HEAD_DIM).astype(np.float32) / HEAD_DIM, BF16)


def _proj_in_kernel(x_ref, mod_ref, g_ref, w_ref, hm_ref, gain_ref, cos_ref, sin_ref, *out_refs,
                    d, n_q, n_k, n_plain, ones_after_plain, q_scale):
    q_ref = out_refs[0] if n_q else None
    k_ref = out_refs[1] if n_k else None
    p_ref = out_refs[-2]
    y_ref = out_refs[-1]
    n_chunks = n_q + n_k + n_plain
    in_grid = pl.program_id(0) < pl.num_programs(0)

    @pl.when(in_grid)
    def _():
        x = x_ref[...]
        ms = jnp.mean(x * x, axis=-1, keepdims=True)
        h = x * lax.rsqrt(ms + EPS) * g_ref[...]
        h = h * (1.0 + mod_ref[:, d:2 * d]) + mod_ref[:, 0:d]
        hb = h.astype(BF16)
        for c2 in range(0, n_chunks, 2):
            cols = slice(c2 * LANES, (c2 + min(2, n_chunks - c2)) * LANES)
            y_ref[:, cols] = jnp.dot(hb, w_ref[:, cols], preferred_element_type=F32)

    @pl.when(in_grid)
    def _():
        lane = lax.broadcasted_iota(jnp.int32, (1, LANES), 1)
        first_half = (lane // ROT_FREQS) % 2 == 0
        for c in range(n_chunks):
            y = y_ref[:, c * LANES:(c + 1) * LANES]
            if c < n_q + n_k:
                msq = jnp.dot((y * y).astype(BF16), hm_ref[...], preferred_element_type=F32)
                is_q = c < n_q
                gain = gain_ref[0:1, :] if is_q else gain_ref[1:2, :]
                yn = y * lax.rsqrt(msq + EPS) * gain
                partner = jnp.where(first_half, pltpu.roll(yn, LANES - ROT_FREQS, 1), pltpu.roll(yn, ROT_FREQS, 1))
                out = yn * cos_ref[...] + partner * sin_ref[...]
                if is_q:
                    q_ref[:, c * LANES:(c + 1) * LANES] = (out * q_scale).astype(q_ref.dtype)
                else:
                    ck = c - n_q
                    k_ref[:, ck * LANES:(ck + 1) * LANES] = out.astype(k_ref.dtype)
            else:
                cp = c - n_q - n_k
                if ones_after_plain:
                    p_ref[:, 2 * cp * LANES:(2 * cp + 1) * LANES] = y.astype(p_ref.dtype)
                    p_ref[:, (2 * cp + 1) * LANES:(2 * cp + 2) * LANES] = jnp.ones(y.shape, p_ref.dtype)
                else:
                    p_ref[:, cp * LANES:(cp + 1) * LANES] = y.astype(p_ref.dtype)


Q_SCALE_LOG2 = HEAD_DIM ** -0.5 * math.log2(math.e)


def _proj_in(xs, mods, g, w, gains, cos_t, sin_t, geom, *, n_q, n_k, n_plain, ones_after_plain=False,
             q_scale=Q_SCALE_LOG2):
    n, d = xs.shape
    tm = PROJ_IN_TILE
    lat_tiles = geom.seq // tm
    n_lat = geom.batch * lat_tiles
    assert geom.seq % tm == 0 and n % tm == 0 and cos_t.shape[0] == geom.seq + tm

    def mod_row(i):
        return jnp.where(i < n_lat, i // lat_tiles, geom.batch)

    def rope_row(i):
        return jnp.where(i < n_lat, i % lat_tiles, lat_tiles)

    n_cols = (n_q + n_k + n_plain) * LANES
    assert w.shape == (d, n_cols)
    out_shape, out_specs = [], []
    for cnt in (n_q, n_k):
        if cnt:
            out_shape.append(jax.ShapeDtypeStruct((n, cnt * LANES), BF16))
            out_specs.append(pl.BlockSpec((tm, cnt * LANES), lambda i: (i, 0)))
    pw = n_plain * LANES * (2 if ones_after_plain else 1)
    out_shape.append(jax.ShapeDtypeStruct((n, pw), BF16))
    out_specs.append(pl.BlockSpec((tm, pw), lambda i: (i, 0)))
    kern = functools.partial(_proj_in_kernel, d=d, n_q=n_q, n_k=n_k, n_plain=n_plain,
                             ones_after_plain=ones_after_plain, q_scale=q_scale)
    return pl.pallas_call(
        kern,
        grid=(n // tm,),
        in_specs=[pl.BlockSpec((tm, d), lambda i: (i, 0)),
                  pl.BlockSpec((None, 1, mods.shape[-1]), lambda i: (mod_row(i), 0, 0)),
                  pl.BlockSpec((1, d), lambda i: (0, 0)),
                  pl.BlockSpec((d, n_cols), lambda i: (0, 0)),
                  pl.BlockSpec((LANES, LANES), lambda i: (0, 0)),
                  pl.BlockSpec((2, LANES), lambda i: (0, 0)),
                  pl.BlockSpec((tm, LANES), lambda i: (rope_row(i), 0)),
                  pl.BlockSpec((tm, LANES), lambda i: (rope_row(i), 0))],
        out_specs=out_specs,
        out_shape=out_shape,
        scratch_shapes=[pltpu.VMEM((tm, n_cols), F32)],
        compiler_params=_cparams("parallel"),
        name="proj_in",
    )(xs, mods, g.reshape(1, d), w, _head_mean_matrix(), gains, cos_t, sin_t)


DIFF_KV_CHUNK = 1024
DIFF_Q_TILE = 256
DIFF_HEADS_PER_STEP = 2


def _diff_attn_kernel(*refs, lam_init, n_kv, chunks, heads):
    lam_ref, q_ref = refs[0], refs[1]
    k_refs = refs[2:2 + n_kv]
    v_refs = refs[2 + n_kv:2 + 2 * n_kv]
    sg_ref, o_ref, s_ref, m_ref = refs[2 + 2 * n_kv:]
    dq = 2 * HEAD_DIM
    dv = o_ref.shape[1] // heads
    in_grid = pl.program_id(0) < pl.num_programs(0)

    @pl.when(in_grid)
    def _():
        for h in range(heads):
            q = q_ref[:, h * dq:(h + 1) * dq]
            lane = lax.broadcasted_iota(jnp.int32, q.shape, 1)
            zero = jnp.zeros_like(q)
            q_maps = (jnp.where(lane < HEAD_DIM, q, zero), jnp.where(lane >= HEAD_DIM, q, zero))
            for m in range(2):
                mx = None
                for slab, start, size, col in chunks:
                    s = lax.dot_general(q_maps[m], k_refs[slab][start:start + size, h * dq:(h + 1) * dq],
                                        (((1,), (1,)), ((), ())), preferred_element_type=F32)
                    s_ref[2 * h + m, :, col:col + size] = s
                    cm = jnp.max(s, axis=1, keepdims=True)
                    mx = cm if mx is None else jnp.maximum(mx, cm)
                m_ref[2 * h + m] = mx

    @pl.when(in_grid)
    def _():
        lam_v = lam_ref[...]
        lam = (jnp.exp(jnp.sum(lam_v[0:1] * lam_v[1:2], axis=1, keepdims=True))
               - jnp.exp(jnp.sum(lam_v[2:3] * lam_v[3:4], axis=1, keepdims=True)) + lam_init)
        for h in range(heads):
            outs = []
            for m in range(2):
                row_max = m_ref[2 * h + m]
                acc = None
                for slab, start, size, col in chunks:
                    p = jnp.exp2(s_ref[2 * h + m, :, col:col + size] - row_max)
                    pv = jnp.dot(p.astype(BF16), v_refs[slab][start:start + size, 2 * h * dv:2 * (h + 1) * dv],
                                 preferred_element_type=F32)
                    acc = pv if acc is None else acc + pv
                outs.append(acc[:, 0:dv] * (1.0 / acc[:, dv:dv + 1]))
            a = outs[0] - lam * outs[1]
            a = a * lax.rsqrt(jnp.mean(a * a, axis=-1, keepdims=True) + EPS) * sg_ref[...] * (1.0 - lam_init)
            o_ref[:, h * dv:(h + 1) * dv] = a.astype(o_ref.dtype)


def _diff_attention(q, k, v1, lam_vecs, sub_g, geom, lam_init):
    dq = q.shape[1]
    batch, seq, ctx_len = geom
    heads = dq // (2 * HEAD_DIM)
    dv = 2 * HEAD_DIM
    tq = DIFF_Q_TILE
    q_tiles = seq // tq
    ctx_blk = seq // ctx_len * batch
    lam_spec = pl.BlockSpec((4, HEAD_DIM), lambda *_: (0, 0))
    sg_spec = pl.BlockSpec((1, dv), lambda *_: (0, 0))
    kv_chunk = min(DIFF_KV_CHUNK, seq)
    assert seq % kv_chunk == 0
    chunks = [(0, c * kv_chunk, kv_chunk, c * kv_chunk) for c in range(seq // kv_chunk)]
    chunks.append((1, 0, ctx_len, seq))
    hp = DIFF_HEADS_PER_STEP
    assert heads % hp == 0
    o_lat = pl.pallas_call(
        functools.partial(_diff_attn_kernel, lam_init=lam_init, n_kv=2, chunks=chunks, heads=hp),
        grid=(batch, heads // hp, q_tiles),
        in_specs=[lam_spec,
                  pl.BlockSpec((tq, hp * 2 * HEAD_DIM), lambda b, h, i: (b * q_tiles + i, h)),
                  pl.BlockSpec((seq, hp * 2 * HEAD_DIM), lambda b, h, i: (b, h)),
                  pl.BlockSpec((ctx_len, hp * 2 * HEAD_DIM), lambda b, h, i: (ctx_blk + b, h)),
                  pl.BlockSpec((seq, hp * 2 * dv), lambda b, h, i: (b, h)),
                  pl.BlockSpec((ctx_len, hp * 2 * dv), lambda b, h, i: (ctx_blk + b, h)),
                  sg_spec],
        out_specs=pl.BlockSpec((tq, hp * dv), lambda b, h, i: (b * q_tiles + i, h)),
        out_shape=jax.ShapeDtypeStruct((batch * seq, heads * dv), BF16),
        scratch_shapes=[pltpu.VMEM((2 * hp, tq, seq + ctx_len), F32), pltpu.VMEM((2 * hp, tq, 1), F32)],
        compiler_params=_cparams("parallel", "parallel", "arbitrary"),
        name="diff_attention",
    )(lam_vecs, q, k, k, v1, v1, sub_g.reshape(1, dv))
    o_ctx = pl.pallas_call(
        functools.partial(_diff_attn_kernel, lam_init=lam_init, n_kv=1, chunks=[(0, 0, ctx_len, 0)], heads=1),
        grid=(batch, heads),
        in_specs=[lam_spec,
                  pl.BlockSpec((ctx_len, 2 * HEAD_DIM), lambda b, h: (ctx_blk + b, h)),
                  pl.BlockSpec((ctx_len, 2 * HEAD_DIM), lambda b, h: (ctx_blk + b, h)),
                  pl.BlockSpec((ctx_len, 2 * dv), lambda b, h: (ctx_blk + b, h)),
                  sg_spec],
        out_specs=pl.BlockSpec((ctx_len, dv), lambda b, h: (b, h)),
        out_shape=jax.ShapeDtypeStruct((batch * ctx_len, heads * dv), BF16),
        scratch_shapes=[pltpu.VMEM((2, ctx_len, ctx_len), F32), pltpu.VMEM((2, ctx_len, 1), F32)],
        compiler_params=_cparams("parallel", "parallel"),
        name="diff_attention_ctx",
    )(lam_vecs, q, k, v1, sub_g.reshape(1, dv))
    return o_lat, o_ctx


def _swa_kernel(sink_ref, q_ref, kl_ref, kc_ref, vl_ref, vc_ref, o_ref, *, latent_tiles, group):
    i = pl.program_id(1)
    tq = q_ref.shape[0]
    seq, ctx_len = kl_ref.shape[0], kc_ref.shape[0]
    span = tq + 2 * WINDOW
    start = pl.multiple_of(jnp.clip(i * tq - WINDOW, 0, seq - span), WINDOW)
    qpos = i * tq + lax.broadcasted_iota(jnp.int32, (tq, span), 0)
    kpos = start + lax.broadcasted_iota(jnp.int32, (tq, span), 1)
    valid = (jnp.abs(kpos - qpos) <= WINDOW) & (i < latent_tiles)
    bias = jnp.where(valid, 0.0, NEG_INF).astype(F32)
    nt = (((1,), (1,)), ((), ()))
    for h in range(SWA_KV_HEADS):
        hs = slice(h * LANES, (h + 1) * LANES)
        kw = kl_ref[pl.ds(start, span), hs]
        vw = vl_ref[pl.ds(start, span), hs]
        kc = kc_ref[:, hs]
        vc = vc_ref[:, hs]
        for g in range(group):
            hq = h * group + g
            q = q_ref[:, hq * LANES:(hq + 1) * LANES]
            s_w = lax.dot_general(q, kw, nt, preferred_element_type=F32) + bias
            s_c = lax.dot_general(q, kc, nt, preferred_element_type=F32)
            sink = sink_ref[hq] * math.log2(math.e)
            m = jnp.maximum(jnp.maximum(jnp.max(s_w, axis=1, keepdims=True), jnp.max(s_c, axis=1, keepdims=True)),
                            sink)
            p_w = jnp.exp2(s_w - m)
            p_c = jnp.exp2(s_c - m)
            denom = (jnp.sum(p_w, axis=1, keepdims=True) + jnp.sum(p_c, axis=1, keepdims=True)
                     + jnp.exp2(sink - m))
            o = (jnp.dot(p_w.astype(BF16), vw, preferred_element_type=F32)
                 + jnp.dot(p_c.astype(BF16), vc, preferred_element_type=F32)) * (1.0 / denom)
            o_ref[:, hq * LANES:(hq + 1) * LANES] = o.astype(o_ref.dtype)


def _swa_attention(q, k, v, sink, geom):
    n, dq = q.shape
    batch, seq, ctx_len = geom
    assert ctx_len == TOKEN_TILE
    lt = geom.lat_tiles
    q_heads = dq // LANES
    kvw = SWA_KV_HEADS * LANES
    ctx_blk = seq // ctx_len * batch
    kern = functools.partial(_swa_kernel, latent_tiles=lt, group=q_heads // SWA_KV_HEADS)

    def tile(b, i):
        return jnp.where(i < lt, b * lt + i, geom.ctx_tile(b))

    lat_kv = pl.BlockSpec((seq, kvw), lambda b, i: (b, 0))
    ctx_kv = pl.BlockSpec((ctx_len, kvw), lambda b, i: (ctx_blk + b, 0))
    return pl.pallas_call(
        kern,
        grid=(batch, lt + 1),
        in_specs=[pl.BlockSpec(memory_space=pltpu.SMEM),
                  pl.BlockSpec((TOKEN_TILE, dq), lambda b, i: (tile(b, i), 0)),
                  lat_kv, ctx_kv, lat_kv, ctx_kv],
        out_specs=pl.BlockSpec((TOKEN_TILE, dq), lambda b, i: (tile(b, i), 0)),
        out_shape=jax.ShapeDtypeStruct((n, dq), BF16),
        compiler_params=_cparams("parallel", "arbitrary"),
        name="swa_attention",
    )(sink, q, k, k, v, v)


def _seq_dft_kernel(base_ref, fine_c_ref, fine_s_ref, cc_ref, cs_ref, zc_ref, zs_ref, yc_ref, ys_ref, o_ref,
                    *, latent_tiles):
    m = pl.program_id(2)

    def mix(a_cos, a_msin, z_cos, z_sin):
        acc = jnp.dot(a_cos, z_cos[...], preferred_element_type=F32)
        acc = acc + jnp.dot(a_msin, z_sin[...], preferred_element_type=F32)
        o_ref[...] = acc.astype(o_ref.dtype)

    @pl.when(m < latent_tiles)
    def _():
        c1, s1 = base_ref[0:1, :], base_ref[1:2, :]
        c2, s2 = fine_c_ref[...], fine_s_ref[...]
        a_cos = (c2 * c1 - s2 * s1).astype(BF16)
        a_msin = (-(c2 * s1) - s2 * c1).astype(BF16)
        mix(a_cos, a_msin, zc_ref, zs_ref)

    @pl.when(m >= latent_tiles)
    def _():
        mix(cc_ref[...], cs_ref[...], yc_ref, ys_ref)


def _dft_angle_tables(n, rows, scale):
    k = jnp.arange(n, dtype=jnp.int32)[None, :]
    step = 2.0 * math.pi / n
    ang0 = (((jnp.arange(n // rows, dtype=jnp.int32) * rows)[:, None] * k) % n).astype(F32) * step
    ang1 = ((jnp.arange(rows, dtype=jnp.int32)[:, None] * k) % n).astype(F32) * step
    base = jnp.stack([jnp.cos(ang0), jnp.sin(ang0)], axis=1)
    return base, jnp.cos(ang1) * scale, jnp.sin(ang1) * scale


def _seq_dft(z, ctx_mats, geom, d):
    batch, seq, ctx_len = geom
    assert ctx_len == TOKEN_TILE and seq % ctx_len == 0
    lt = geom.lat_tiles
    tm = TOKEN_TILE
    tn = d // 2
    n_col = d // tn
    ctx_blk = seq // ctx_len * batch
    lat_tables = _dft_angle_tables(seq, tm, seq ** -0.5)
    fine = pl.BlockSpec((tm, seq), lambda b, j, m: (0, 0))
    ctx_a = pl.BlockSpec((ctx_len, ctx_len), lambda b, j, m: (0, 0))
    return pl.pallas_call(
        functools.partial(_seq_dft_kernel, latent_tiles=lt),
        grid=(batch, n_col, lt + 1),
        in_specs=[pl.BlockSpec((None, 2, seq), lambda b, j, m: (jnp.minimum(m, lt - 1), 0, 0)),
                  fine, fine, ctx_a, ctx_a,
                  pl.BlockSpec((seq, tn), lambda b, j, m: (b, j)),
                  pl.BlockSpec((seq, tn), lambda b, j, m: (b, n_col + j)),
                  pl.BlockSpec((ctx_len, tn), lambda b, j, m: (ctx_blk + b, j)),
                  pl.BlockSpec((ctx_len, tn), lambda b, j, m: (ctx_blk + b, n_col + j))],
        out_specs=pl.BlockSpec((tm, tn), lambda b, j, m: (jnp.where(m < lt, b * lt + m, geom.ctx_tile(b)), j)),
        out_shape=jax.ShapeDtypeStruct((z.shape[0], d), BF16),
        compiler_params=_cparams("parallel", "parallel", "arbitrary"),
        name="seq_dft",
    )(*lat_tables, *ctx_mats, z, z, z, z)


def _dft_mats(n, scale):
    idx = (jnp.arange(n, dtype=jnp.int32)[:, None] * jnp.arange(n, dtype=jnp.int32)[None, :]) % n
    ang = idx.astype(F32) * (2.0 * math.pi / n)
    return (jnp.cos(ang) * scale).astype(BF16), (-jnp.sin(ang) * scale).astype(BF16)


def _load_token_tiles(ref):
    rows = ref.shape[0] // SUBLANES
    return jnp.concatenate([ref[pl.ds(s, rows, stride=SUBLANES), :] for s in range(SUBLANES)], axis=1)


def _store_token_tiles(ref, val):
    rows = ref.shape[0] // SUBLANES
    for s in range(SUBLANES):
        ref[pl.ds(s, rows, stride=SUBLANES), :] = val[:, s * LANES:(s + 1) * LANES]


def _proj_out_kernel(ol_ref, oc_ref, w_ref, x_ref, mod_ref, g_ref, xo_ref, h_ref, *, d, n_lat_tiles):
    def finish(o_ref):
        y = jnp.dot(o_ref[...], w_ref[...], preferred_element_type=F32)
        x = x_ref[...] + mod_ref[:, 2 * d:3 * d] * y
        xo_ref[...] = x
        ms = jnp.mean(x * x, axis=-1, keepdims=True)
        h = x * lax.rsqrt(ms + EPS) * g_ref[...]
        _store_token_tiles(h_ref, h * (1.0 + mod_ref[:, 4 * d:5 * d]) + mod_ref[:, 3 * d:4 * d])

    @pl.when(pl.program_id(0) < n_lat_tiles)
    def _():
        finish(ol_ref)

    @pl.when(pl.program_id(0) >= n_lat_tiles)
    def _():
        finish(oc_ref)


def _proj_out(o_lat, o_ctx, w, xs, mods, g2, geom):
    n, d = xs.shape
    ko = o_lat.shape[1]
    tm = TOKEN_TILE
    nl = geom.n_lat_tiles
    ctx_off = nl if o_ctx.shape[0] == n else 0
    return pl.pallas_call(
        functools.partial(_proj_out_kernel, d=d, n_lat_tiles=nl),
        grid=(n // tm,),
        in_specs=[pl.BlockSpec((tm, ko), lambda i: (jnp.minimum(i, nl - 1), 0)),
                  pl.BlockSpec((tm, ko), lambda i: (jnp.maximum(i, nl) - nl + ctx_off, 0)),
                  pl.BlockSpec((ko, d), lambda i: (0, 0)),
                  pl.BlockSpec((tm, d), lambda i: (i, 0)),
                  pl.BlockSpec((None, 1, mods.shape[-1]), lambda i: (geom.mod_row(i), 0, 0)),
                  pl.BlockSpec((1, d), lambda i: (0, 0))],
        out_specs=[pl.BlockSpec((tm, d), lambda i: (i, 0)),
                   pl.BlockSpec((tm * SUBLANES, LANES), lambda i: (i, 0))],
        out_shape=[jax.ShapeDtypeStruct((n, d), F32), jax.ShapeDtypeStruct((n * SUBLANES, LANES), F32)],
        compiler_params=_cparams("parallel"),
        name="proj_out",
    )(o_lat, o_ctx, w, xs, mods, g2.reshape(1, d))


ROUTER_ROWS = SUBLANES + N_EXPERTS


def _router_kernel(h_ref, wt_ref, b_ref, tri_ref, ints_ref, flt_ref, cnt_ref, carry_ref):
    step = pl.program_id(0)
    tr = h_ref.shape[0] // SUBLANES

    @pl.when(step == 0)
    def _():
        carry_ref[...] = jnp.zeros(carry_ref.shape, F32)

    logits = lax.dot_general(wt_ref[...], _load_token_tiles(h_ref), (((1,), (1,)), ((), ())),
                             precision=HIGHEST, preferred_element_type=F32) + b_ref[:, 0:1]
    row8 = lax.broadcasted_iota(jnp.int32, (SUBLANES, tr), 0)
    lg = jnp.where(row8 < N_GROUPS, logits[0:SUBLANES], NEG_INF)
    lg_max = jnp.max(lg, axis=0, keepdims=True)
    pg = 1.0 / jnp.sum(jnp.exp(lg - lg_max), axis=0, keepdims=True)
    grp = jnp.min(jnp.where(lg == lg_max, row8, SUBLANES), axis=0, keepdims=True)
    l2 = jnp.zeros((EXPERTS_PER_GROUP, tr), F32)
    for g in range(N_GROUPS):
        lo = SUBLANES + g * EXPERTS_PER_GROUP
        l2 = l2 + jnp.where(grp == g, logits[lo:lo + EXPERTS_PER_GROUP], 0.0)
    l2_max = jnp.max(l2, axis=0, keepdims=True)
    j0 = jnp.min(jnp.where(l2 == l2_max, row8, SUBLANES), axis=0, keepdims=True)
    rest = jnp.where(row8 == j0, NEG_INF, l2)
    r_max = jnp.max(rest, axis=0, keepdims=True)
    j1 = jnp.min(jnp.where(rest == r_max, row8, SUBLANES), axis=0, keepdims=True)
    e1 = jnp.exp(r_max - l2_max)
    inv = 1.0 / (1.0 + e1)
    w0 = pg * inv
    w1 = pg * e1 * inv
    ex0 = grp * EXPERTS_PER_GROUP + j0
    ex1 = grp * EXPERTS_PER_GROUP + j1

    rows = lax.broadcasted_iota(jnp.int32, (N_EXPERTS, tr), 0)
    oh0 = (rows == ex0).astype(F32)
    oh1 = (rows == ex1).astype(F32)
    both = oh0 + oh1
    before = jnp.dot(both.astype(BF16), tri_ref[...], preferred_element_type=F32) + carry_ref[:, 0:1]
    rank0 = jnp.sum(oh0 * before, axis=0, keepdims=True)
    rank1 = jnp.sum(oh1 * before, axis=0, keepdims=True)
    carry_ref[...] = carry_ref[...] + jnp.sum(both, axis=1, keepdims=True)
    cnt_ref[...] = carry_ref[...]

    zi = jnp.zeros((SUBLANES - 4, tr), jnp.int32)
    ints_ref[...] = jnp.concatenate([ex0, ex1, rank0.astype(jnp.int32), rank1.astype(jnp.int32), zi], axis=0)
    flt_ref[...] = jnp.concatenate([w0, w1, jnp.zeros((SUBLANES - 2, tr), F32)], axis=0)


def _router(h3, wt, bias, tri):
    n = h3.shape[0] // SUBLANES
    d = wt.shape[1]
    tr = ROUTER_TILE
    return pl.pallas_call(
        _router_kernel,
        grid=(n // tr,),
        in_specs=[pl.BlockSpec((tr * SUBLANES, LANES), lambda i: (i, 0)),
                  pl.BlockSpec((ROUTER_ROWS, d), lambda i: (0, 0)),
                  pl.BlockSpec((ROUTER_ROWS, LANES), lambda i: (0, 0)),
                  pl.BlockSpec((tr, tr), lambda i: (0, 0))],
        out_specs=[pl.BlockSpec((SUBLANES, tr), lambda i: (0, i)),
                   pl.BlockSpec((SUBLANES, tr), lambda i: (0, i)),
                   pl.BlockSpec((N_EXPERTS, LANES), lambda i: (0, 0))],
        out_shape=[jax.ShapeDtypeStruct((SUBLANES, n), jnp.int32),
                   jax.ShapeDtypeStruct((SUBLANES, n), F32),
                   jax.ShapeDtypeStruct((N_EXPERTS, LANES), F32)],
        scratch_shapes=[pltpu.VMEM((N_EXPERTS, LANES), F32)],
        compiler_params=_cparams("arbitrary"),
        name="router",
    )(h3, wt, bias, tri)


def _dest_kernel(pstart_ref, ints_ref, o_ref):
    ints = ints_ref[...]
    ex = ints[0:2]
    base = jnp.zeros(ex.shape, jnp.int32)
    for e in range(N_EXPERTS):
        base = jnp.where(ex == e, pstart_ref[e], base)
    o_ref[...] = jnp.concatenate([base + ints[2:4], jnp.zeros((SUBLANES - 2, ints.shape[1]), jnp.int32)], axis=0)


def _dest_rows(pstart, ints):
    n = ints.shape[1]
    tn = 2048 if n % 2048 == 0 else ROUTER_TILE
    return pl.pallas_call(
        _dest_kernel,
        grid=(n // tn,),
        in_specs=[pl.BlockSpec(memory_space=pltpu.SMEM),
                  pl.BlockSpec((SUBLANES, tn), lambda i: (0, i))],
        out_specs=pl.BlockSpec((SUBLANES, tn), lambda i: (0, i)),
        out_shape=jax.ShapeDtypeStruct((SUBLANES, n), jnp.int32),
        compiler_params=_cparams("parallel"),
        name="dest_rows",
    )(pstart, ints)


def _row_maps_kernel(pad_ref, d0_ref, d1_ref, out_ref, *, n_tokens):
    step = pl.program_id(0)
    td = d0_ref.shape[0]
    n_spans = pad_ref.shape[0] // 2

    @pl.when(step == 0)
    def _():
        def span(e, carry):
            first = (pad_ref[e] >> 3) << 3
            groups = (pad_ref[e] + pad_ref[n_spans + e] - first) >> 3

            def fill(c, carry2):
                for u in range(SUBLANES):
                    row = first + c * SUBLANES + u
                    out_ref[row] = 2 * n_tokens + (row & (MOE_HALF - 1))
                return carry2
            lax.fori_loop(0, groups, fill, 0)
            return carry
        lax.fori_loop(0, n_spans, span, 0)

    base = step * td

    def place(r, carry):
        n = base + r
        out_ref[d0_ref[r]] = n
        out_ref[d1_ref[r]] = n_tokens + n
        return carry
    lax.fori_loop(0, td, place, 0, unroll=8)


def _row_maps(pad_spans, dest0, dest1, n_rows):
    n = dest0.shape[0]
    td = 2048 if n % 2048 == 0 else ROUTER_TILE
    smem = functools.partial(pl.BlockSpec, memory_space=pltpu.SMEM)
    return pl.pallas_call(
        functools.partial(_row_maps_kernel, n_tokens=n),
        grid=(n // td,),
        in_specs=[smem(), smem((td,), lambda i: (i,)), smem((td,), lambda i: (i,))],
        out_specs=smem(),
        out_shape=jax.ShapeDtypeStruct((n_rows,), jnp.int32),
        compiler_params=_cparams("arbitrary"),
        name="moe_row_maps",
    )(pad_spans, dest0, dest1)


def _expert_kernel(blk_exp_ref, n_used_ref, map_first, map_half1, map_next, map_prev, map_half0, map_last,
                   h_ref, w13_ref, w2_ref, y_ref, xbuf0, xbuf1, ybuf0, ybuf1, w13b, w2b, gsem, ssem,
                   *, d_expert, n_tokens):
    j = pl.program_id(0)
    n_used = n_used_ref[0]
    xbuf = (xbuf0, xbuf1)
    ybuf = (ybuf0, ybuf1)

    def tile_of(ref, r):
        return ref.at[pl.ds(pl.multiple_of(r * SUBLANES, SUBLANES), SUBLANES)]

    def gather_row(idx_ref, buf, r):
        out_row = idx_ref[r]
        tok = out_row - jnp.where(out_row >= 2 * n_tokens, 2 * n_tokens, jnp.where(out_row >= n_tokens, n_tokens, 0))
        pltpu.make_async_copy(tile_of(h_ref, tok), tile_of(xbuf[buf], r), gsem.at[buf]).start()

    def scatter_row(idx_ref, buf, r):
        pltpu.make_async_copy(tile_of(ybuf[buf], r), tile_of(y_ref, idx_ref[r]), ssem.at[buf]).start()

    def wait_gather(buf):
        pltpu.make_async_copy(h_ref.at[pl.ds(0, MOE_HALF * SUBLANES)], xbuf[buf], gsem.at[buf]).wait()

    def wait_scatter(buf):
        pltpu.make_async_copy(ybuf[buf], y_ref.at[pl.ds(0, MOE_HALF * SUBLANES)], ssem.at[buf]).wait()

    def half_block(buf, gather_idx, scatter_idx):
        for r in range(MOE_HALF):
            gather_row(gather_idx, 1 - buf, r)
            scatter_row(scatter_idx, 1 - buf, r)
        x = _load_token_tiles(xbuf[buf]).astype(BF16)
        gu = jnp.dot(x, w13b[...], preferred_element_type=F32)
        g = gu[:, 0:d_expert]
        u = gu[:, d_expert:2 * d_expert]
        a = g * (1.0 / (1.0 + jnp.exp(-g))) * u
        _store_token_tiles(ybuf[buf], jnp.dot(a.astype(BF16), w2b[...], preferred_element_type=F32))

    @pl.when(j < n_used)
    def _():
        @pl.when(j == 0)
        def _():
            ybuf0[...] = jnp.zeros(ybuf0.shape, F32)
            ybuf1[...] = jnp.zeros(ybuf1.shape, F32)

            def first_rows(r, carry):
                pltpu.make_async_copy(tile_of(ybuf0, r), tile_of(y_ref, 2 * n_tokens + r), ssem.at[0]).start()
                gather_row(map_first, 0, r)
                return carry
            lax.fori_loop(0, MOE_HALF, first_rows, 0)
            wait_scatter(0)

        @pl.when((j == 0) | (blk_exp_ref[j] != blk_exp_ref[jnp.maximum(j - 1, 0)]))
        def _():
            w13b[...] = w13_ref[...].astype(BF16)
            w2b[...] = w2_ref[...].astype(BF16)

        wait_gather(0)

        @pl.when(j > 0)
        def _():
            wait_scatter(0)

        half_block(0, map_half1, map_prev)
        wait_gather(1)
        wait_scatter(1)
        half_block(1, map_next, map_half0)

        @pl.when(j == n_used - 1)
        def _():
            wait_gather(0)
            wait_scatter(0)

            def last_scatter(r, carry):
                scatter_row(map_last, 1, r)
                return carry
            lax.fori_loop(0, MOE_HALF, last_scatter, 0)
            wait_scatter(1)


def _experts(blk_exp, n_used, row_map, h3, w13_all, w2_all, layer):
    n = h3.shape[0] // SUBLANES
    d, d_expert = w13_all.shape[2], w2_all.shape[2]
    n_steps = row_map.shape[0] // MOE_TILE
    smem = functools.partial(pl.BlockSpec, memory_space=pltpu.SMEM)

    def half_spec(half_of):
        def index(j, be, nu):
            last = 2 * nu[0] - 1
            return (jnp.clip(half_of(jnp.minimum(j, nu[0] - 1), last), 0, last),)
        return smem((MOE_HALF,), index)

    def expert(j, be, nu):
        return (layer, be[jnp.minimum(j, nu[0] - 1)], 0, 0)

    token_tile = (MOE_HALF * SUBLANES, LANES)
    grid_spec = pltpu.PrefetchScalarGridSpec(
        num_scalar_prefetch=2,
        grid=(n_steps,),
        in_specs=[half_spec(lambda j, last: 0),
                  half_spec(lambda j, last: 2 * j + 1),
                  half_spec(lambda j, last: 2 * j + 2),
                  half_spec(lambda j, last: 2 * j - 1),
                  half_spec(lambda j, last: 2 * j),
                  half_spec(lambda j, last: last),
                  pl.BlockSpec(memory_space=pl.ANY),
                  pl.BlockSpec((None, None, d, 2 * d_expert), expert),
                  pl.BlockSpec((None, None, d_expert, d), expert)],
        out_specs=pl.BlockSpec(memory_space=pl.ANY),
        scratch_shapes=[pltpu.VMEM(token_tile, F32), pltpu.VMEM(token_tile, F32),
                        pltpu.VMEM(token_tile, F32), pltpu.VMEM(token_tile, F32),
                        pltpu.VMEM((d, 2 * d_expert), BF16), pltpu.VMEM((d_expert, d), BF16),
                        pltpu.SemaphoreType.DMA((2,)), pltpu.SemaphoreType.DMA((2,))],
    )
    return pl.pallas_call(
        functools.partial(_expert_kernel, d_expert=d_expert, n_tokens=n),
        grid_spec=grid_spec,
        out_shape=jax.ShapeDtypeStruct(((2 * n + MOE_HALF) * SUBLANES, LANES), F32),
        compiler_params=_cparams("arbitrary"),
        name="moe_experts",
    )(blk_exp, n_used, row_map, row_map, row_map, row_map, row_map, row_map, h3, w13_all, w2_all)


def _combine_kernel(w_ref, x_ref, mod_ref, y0_ref, y1_ref, o_ref, *, d):
    out = w_ref[:, 0:1] * _load_token_tiles(y0_ref) + w_ref[:, 1:2] * _load_token_tiles(y1_ref)
    o_ref[...] = x_ref[...] + mod_ref[:, 5 * d:6 * d] * out


def _combine(w_cols, xs, mods, y_tok, geom, n_out):
    n, d = xs.shape
    tc = TOKEN_TILE
    return pl.pallas_call(
        functools.partial(_combine_kernel, d=d),
        grid=(n_out // tc,),
        in_specs=[pl.BlockSpec((tc, 2), lambda i: (i, 0)),
                  pl.BlockSpec((tc, d), lambda i: (i, 0)),
                  pl.BlockSpec((None, 1, mods.shape[-1]), lambda i: (geom.mod_row(i), 0, 0)),
                  pl.BlockSpec((tc * SUBLANES, LANES), lambda i: (i, 0)),
                  pl.BlockSpec((tc * SUBLANES, LANES), lambda i: (n // tc + i, 0))],
        out_specs=pl.BlockSpec((tc, d), lambda i: (i, 0)),
        out_shape=jax.ShapeDtypeStruct((n_out, d), F32),
        compiler_params=_cparams("parallel"),
        name="moe_combine",
    )(w_cols, xs, mods, y_tok, y_tok)


def _hier_moe(h3, xs, mods, geom, w_r1, b_r1, w_r2, b_r2, w13_all, w2_all, layer, n_out):
    n, d = xs.shape
    wt = jnp.zeros((ROUTER_ROWS, d), F32)
    wt = wt.at[0:N_GROUPS].set(w_r1.T)
    wt = wt.at[SUBLANES:].set(jnp.transpose(w_r2, (0, 2, 1)).reshape(N_EXPERTS, d))
    bias = jnp.zeros((ROUTER_ROWS,), F32).at[0:N_GROUPS].set(b_r1).at[SUBLANES:].set(b_r2.reshape(-1))
    bias = jnp.broadcast_to(bias[:, None], (ROUTER_ROWS, LANES))
    tri = jnp.asarray(np.triu(np.ones((ROUTER_TILE, ROUTER_TILE), np.float32), 1), BF16)
    ints, flt, cnt = _router(h3, wt, bias, tri)

    counts = cnt[:, 0].astype(jnp.int32)
    padded = ((counts + MOE_TILE - 1) // MOE_TILE) * MOE_TILE
    pend = jnp.cumsum(padded)
    pstart = pend - padded
    n_blocks = (2 * n) // MOE_TILE + N_EXPERTS
    n_rows = n_blocks * MOE_TILE
    blk_row0 = jnp.arange(n_blocks, dtype=jnp.int32) * MOE_TILE
    blk_exp = jnp.minimum(jnp.sum(pend[None, :] <= blk_row0[:, None], axis=1), N_EXPERTS - 1).astype(jnp.int32)

    n_used = (pend[-1] // MOE_TILE).astype(jnp.int32)
    pad_spans = jnp.concatenate([pstart + counts, pend[-1:], padded - counts, n_rows - pend[-1:]]).astype(jnp.int32)

    dest = _dest_rows(pstart.astype(jnp.int32), ints)
    row_map = _row_maps(pad_spans, dest[0], dest[1], n_rows)
    y_tok = _experts(blk_exp, n_used[None], row_map, h3, w13_all, w2_all, layer)
    return _combine(flt[0:2].T, xs, mods, y_tok, geom, n_out)


def _rope_tables(seq, identity_rows):
    ctx_len = identity_rows
    rows = seq // GRID_W
    row = jnp.repeat(jnp.arange(rows, dtype=F32), GRID_W)
    col = jnp.tile(jnp.arange(GRID_W, dtype=F32), rows)
    inv = ROPE_THETA ** (-jnp.arange(ROT_FREQS, dtype=F32) / ROT_FREQS)
    ang_r, ang_c = row[:, None] * inv, col[:, None] * inv
    cos_h = jnp.concatenate([jnp.cos(ang_r)] * 2 + [jnp.cos(ang_c)] * 2, axis=1)
    sin_h = jnp.concatenate([-jnp.sin(ang_r), jnp.sin(ang_r), -jnp.sin(ang_c), jnp.sin(ang_c)], axis=1)
    cos_t = jnp.concatenate([jnp.tile(cos_h, (1, 2)), jnp.ones((ctx_len, LANES), F32)], axis=0)
    sin_t = jnp.concatenate([jnp.tile(sin_h, (1, 2)), jnp.zeros((ctx_len, LANES), F32)], axis=0)
    return cos_t, sin_t


def _pad_heads_cols(w, n_heads):
    d = w.shape[0]
    w3 = w.reshape(d, n_heads, HEAD_DIM)
    return jnp.concatenate([w3, jnp.zeros_like(w3)], axis=2).reshape(d, n_heads * LANES)


def kernel(x, c, ctx, c_ctx, ada_w, ada_b, norm1_g, norm2_g, a_wqkv, a_wo, a_q_norm, a_k_norm, a_lambda_q1, a_lambda_k1, a_lambda_q2, a_lambda_k2, a_subln_g, b_wqkv, b_wo, b_q_norm, b_k_norm, b_sink, f_wo, r_w1, r_b1, r_w2, r_b2, e_w13, e_w2):
    batch, seq, d = x.shape
    ctx_len = ctx.shape[1]
    depth = ada_w.shape[0]
    n = batch * (seq + ctx_len)
    assert seq % TOKEN_TILE == 0 and ctx_len % TOKEN_TILE == 0 and n % ROUTER_TILE == 0
    geom = Geom(batch, seq, ctx_len)

    xs = jnp.concatenate([x.reshape(batch * seq, d), ctx.reshape(batch * ctx_len, d)], axis=0)
    mod_rows = ((batch + 1 + SUBLANES - 1) // SUBLANES) * SUBLANES
    cvec = jnp.zeros((mod_rows, d), F32).at[0:batch].set(c).at[batch].set(c_ctx)
    mods_all = _ada_table(cvec, ada_w, ada_b).reshape(depth, mod_rows, 1, 6 * d)
    cos_t, sin_t = _rope_tables(seq, PROJ_IN_TILE)

    for i in range(depth):
        kind, j = i % N_MIXERS, i // N_MIXERS
        mods = mods_all[i]
        if kind == 0:
            lam_init = 0.8 - 0.6 * math.exp(-0.3 * i)
            nh = d // LANES
            gains = jnp.stack([jnp.tile(a_q_norm[j], 2), jnp.tile(a_k_norm[j], 2)])
            q, k, v1 = _proj_in(xs, mods, norm1_g[i], a_wqkv[j].astype(BF16), gains, cos_t, sin_t, geom,
                                n_q=nh, n_k=nh, n_plain=nh, ones_after_plain=True)
            lam_vecs = jnp.stack([a_lambda_q1[j], a_lambda_k1[j], a_lambda_q2[j], a_lambda_k2[j]])
            o_lat, o_ctx = _diff_attention(q, k, v1, lam_vecs, a_subln_g[j], geom, lam_init)
            wo = a_wo[j].astype(BF16)
        elif kind == 1:
            q_heads = d // HEAD_DIM
            nq, nk = q_heads * HEAD_DIM, SWA_KV_HEADS * HEAD_DIM
            w = b_wqkv[j]
            w_pad = jnp.concatenate([_pad_heads_cols(w[:, :nq], q_heads),
                                     _pad_heads_cols(w[:, nq:nq + nk], SWA_KV_HEADS),
                                     _pad_heads_cols(w[:, nq + nk:], SWA_KV_HEADS)], axis=1).astype(BF16)
            zeros64 = jnp.zeros((HEAD_DIM,), F32)
            gains = jnp.stack([jnp.concatenate([b_q_norm[j], zeros64]), jnp.concatenate([b_k_norm[j], zeros64])])
            q, k, v = _proj_in(xs, mods, norm1_g[i], w_pad, gains, cos_t, sin_t, geom,
                               n_q=q_heads, n_k=SWA_KV_HEADS, n_plain=SWA_KV_HEADS)
            o_lat = o_ctx = _swa_attention(q, k, v, b_sink[j], geom)
            wo3 = b_wo[j].reshape(q_heads, HEAD_DIM, d)
            wo = jnp.concatenate([wo3, jnp.zeros_like(wo3)], axis=1).reshape(q_heads * LANES, d).astype(BF16)
        else:
            gd = d // FOURIER_GROUPS
            cd, msd = _dft_mats(gd, gd ** -0.5)
            eye = jnp.eye(FOURIER_GROUPS, dtype=BF16)
            w_cs = jnp.concatenate([jnp.kron(eye, cd), jnp.kron(eye, -msd)], axis=1)
            (z,) = _proj_in(xs, mods, norm1_g[i], w_cs, jnp.zeros((2, LANES), F32), cos_t, sin_t, geom,
                            n_q=0, n_k=0, n_plain=2 * d // LANES)
            o_lat = o_ctx = _seq_dft(z, _dft_mats(ctx_len, ctx_len ** -0.5), geom, d)
            wo = f_wo[j].astype(BF16)
        xs, h2 = _proj_out(o_lat, o_ctx, wo, xs, mods, norm2_g[i], geom)
        n_out = n if i < depth - 1 else batch * seq
        xs = _hier_moe(h2, xs, mods, geom, r_w1[i], r_b1[i], r_w2[i], r_b2[i], e_w13, e_w2, i, n_out)

    return xs.reshape(batch, seq, d)
```

```python
import functools
import math
from typing import NamedTuple

import numpy as np
import jax
import jax.numpy as jnp
from jax import lax
from jax.experimental import pallas as pl
from jax.experimental.pallas import tpu as pltpu

F32 = jnp.float32
BF16 = jnp.bfloat16
HIGHEST = lax.Precision.HIGHEST

GRID_W = 64
HEAD_DIM = 64
ROT_FREQS = HEAD_DIM // 4
ROPE_THETA = 10000.0
WINDOW = 128
N_MIXERS = 3
SWA_KV_HEADS = 4
FOURIER_GROUPS = 4
N_GROUPS = 4
EXPERTS_PER_GROUP = 8
N_EXPERTS = N_GROUPS * EXPERTS_PER_GROUP
EPS = 1e-6
NEG_INF = -1e30

LANES = 128
SUBLANES = 8
MXU_DIM = 256
TOKEN_TILE = 256
PROJ_IN_TILE = 512
MOE_HALF = 256
MOE_TILE = 2 * MOE_HALF
ROUTER_TILE = 512
VMEM_LIMIT = 48 * 1024 * 1024


def _cparams(*sem):
    return pltpu.CompilerParams(dimension_semantics=sem, vmem_limit_bytes=VMEM_LIMIT)


class Geom(NamedTuple):
    batch: int
    seq: int
    ctx_len: int

    @property
    def lat_tiles(self):
        return self.seq // TOKEN_TILE

    @property
    def n_lat_tiles(self):
        return self.batch * self.lat_tiles

    @property
    def n_tiles(self):
        return self.n_lat_tiles + self.batch * (self.ctx_len // TOKEN_TILE)

    def mod_row(self, i):
        return jnp.where(i < self.n_lat_tiles, i // self.lat_tiles, self.batch)

    def ctx_tile(self, b):
        return self.n_lat_tiles + b


def _ada_kernel(c_ref, w_ref, b_ref, o_ref):
    c = c_ref[...]
    s = c * (1.0 / (1.0 + jnp.exp(-c)))
    o_ref[...] = jnp.dot(s, w_ref[...], precision=HIGHEST, preferred_element_type=F32) + b_ref[...]


def _ada_table(cvec, ada_w, ada_b):
    depth, d, n6 = ada_w.shape
    rows = cvec.shape[0]
    tn = n6 // 4
    return pl.pallas_call(
        _ada_kernel,
        grid=(depth, n6 // tn),
        in_specs=[pl.BlockSpec((rows, d), lambda l, j: (0, 0)),
                  pl.BlockSpec((None, d, tn), lambda l, j: (l, 0, j)),
                  pl.BlockSpec((None, 1, tn), lambda l, j: (l, 0, j))],
        out_specs=pl.BlockSpec((None, rows, tn), lambda l, j: (l, 0, j)),
        out_shape=jax.ShapeDtypeStruct((depth, rows, n6), F32),
        compiler_params=_cparams("parallel", "parallel"),
        name="ada_table",
    )(cvec, ada_w, ada_b.reshape(depth, 1, n6))


def _head_mean_matrix():
    r = np.arange(LANES)
    return jnp.asarray((r[:, None] // HEAD_DIM == r[None, :] // HEAD_DIM).astype(np.float32) / HEAD_DIM, BF16)


def _proj_in_kernel(x_ref, mod_ref, g_ref, w_ref, hm_ref, gain_ref, cos_ref, sin_ref, *out_refs,
                    d, n_q, n_k, n_plain, ones_after_plain, q_scale):
    q_ref = out_refs[0] if n_q else None
    k_ref = out_refs[1] if n_k else None
    p_ref = out_refs[-2]
    y_ref = out_refs[-1]
    n_chunks = n_q + n_k + n_plain
    in_grid = pl.program_id(0) < pl.num_programs(0)

    @pl.when(in_grid)
    def _():
        x = x_ref[...]
        ms = jnp.mean(x * x, axis=-1, keepdims=True)
        h = x * lax.rsqrt(ms + EPS) * g_ref[...]
        h = h * (1.0 + mod_ref[:, d:2 * d]) + mod_ref[:, 0:d]
        hb = h.astype(BF16)
        for c2 in range(0, n_chunks, 2):
            cols = slice(c2 * LANES, (c2 + min(2, n_chunks - c2)) * LANES)
            y_ref[:, cols] = jnp.dot(hb, w_ref[:, cols], preferred_element_type=F32)

    @pl.when(in_grid)
    def _():
        lane = lax.broadcasted_iota(jnp.int32, (1, LANES), 1)
        first_half = (lane // ROT_FREQS) % 2 == 0
        for c in range(n_chunks):
            y = y_ref[:, c * LANES:(c + 1) * LANES]
            if c < n_q + n_k:
                msq = jnp.dot((y * y).astype(BF16), hm_ref[...], preferred_element_type=F32)
                is_q = c < n_q
                gain = gain_ref[0:1, :] if is_q else gain_ref[1:2, :]
                yn = y * lax.rsqrt(msq + EPS) * gain
                partner = jnp.where(first_half, pltpu.roll(yn, LANES - ROT_FREQS, 1), pltpu.roll(yn, ROT_FREQS, 1))
                out = yn * cos_ref[...] + partner * sin_ref[...]
                if is_q:
                    q_ref[:, c * LANES:(c + 1) * LANES] = (out * q_scale).astype(q_ref.dtype)
                else:
                    ck = c - n_q
                    k_ref[:, ck * LANES:(ck + 1) * LANES] = out.astype(k_ref.dtype)
            else:
                cp = c - n_q - n_k
                if ones_after_plain:
                    p_ref[:, 2 * cp * LANES:(2 * cp + 1) * LANES] = y.astype(p_ref.dtype)
                    p_ref[:, (2 * cp + 1) * LANES:(2 * cp + 2) * LANES] = jnp.ones(y.shape, p_ref.dtype)
                else:
                    p_ref[:, cp * LANES:(cp + 1) * LANES] = y.astype(p_ref.dtype)


Q_SCALE_LOG2 = HEAD_DIM ** -0.5 * math.log2(math.e)


def _proj_in(xs, mods, g, w, gains, cos_t, sin_t, geom, *, n_q, n_k, n_plain, ones_after_plain=False,
             q_scale=Q_SCALE_LOG2):
    n, d = xs.shape
    tm = PROJ_IN_TILE
    lat_tiles = geom.seq // tm
    n_lat = geom.batch * lat_tiles
    assert geom.seq % tm == 0 and n % tm == 0 and cos_t.shape[0] == geom.seq + tm

    def mod_row(i):
        return jnp.where(i < n_lat, i // lat_tiles, geom.batch)

    def rope_row(i):
        return jnp.where(i < n_lat, i % lat_tiles, lat_tiles)

    n_cols = (n_q + n_k + n_plain) * LANES
    assert w.shape == (d, n_cols)
    out_shape, out_specs = [], []
    for cnt in (n_q, n_k):
        if cnt:
            out_shape.append(jax.ShapeDtypeStruct((n, cnt * LANES), BF16))
            out_specs.append(pl.BlockSpec((tm, cnt * LANES), lambda i: (i, 0)))
    pw = n_plain * LANES * (2 if ones_after_plain else 1)
    out_shape.append(jax.ShapeDtypeStruct((n, pw), BF16))
    out_specs.append(pl.BlockSpec((tm, pw), lambda i: (i, 0)))
    kern = functools.partial(_proj_in_kernel, d=d, n_q=n_q, n_k=n_k, n_plain=n_plain,
                             ones_after_plain=ones_after_plain, q_scale=q_scale)
    return pl.pallas_call(
        kern,
        grid=(n // tm,),
        in_specs=[pl.BlockSpec((tm, d), lambda i: (i, 0)),
                  pl.BlockSpec((None, 1, mods.shape[-1]), lambda i: (mod_row(i), 0, 0)),
                  pl.BlockSpec((1, d), lambda i: (0, 0)),
                  pl.BlockSpec((d, n_cols), lambda i: (0, 0)),
                  pl.BlockSpec((LANES, LANES), lambda i: (0, 0)),
                  pl.BlockSpec((2, LANES), lambda i: (0, 0)),
                  pl.BlockSpec((tm, LANES), lambda i: (rope_row(i), 0)),
                  pl.BlockSpec((tm, LANES), lambda i: (rope_row(i), 0))],
        out_specs=out_specs,
        out_shape=out_shape,
        scratch_shapes=[pltpu.VMEM((tm, n_cols), F32)],
        compiler_params=_cparams("parallel"),
        name="proj_in",
    )(xs, mods, g.reshape(1, d), w, _head_mean_matrix(), gains, cos_t, sin_t)


DIFF_KV_CHUNK = 1024
DIFF_Q_TILE = 256
DIFF_HEADS_PER_STEP = 2


def _diff_attn_kernel(*refs, lam_init, n_kv, chunks, heads):
    lam_ref, q_ref = refs[0], refs[1]
    k_refs = refs[2:2 + n_kv]
    v_refs = refs[2 + n_kv:2 + 2 * n_kv]
    sg_ref, o_ref, s_ref, m_ref = refs[2 + 2 * n_kv:]
    dq = 2 * HEAD_DIM
    dv = o_ref.shape[1] // heads
    in_grid = pl.program_id(0) < pl.num_programs(0)

    @pl.when(in_grid)
    def _():
        for h in range(heads):
            q = q_ref[:, h * dq:(h + 1) * dq]
            lane = lax.broadcasted_iota(jnp.int32, q.shape, 1)
            zero = jnp.zeros_like(q)
            q_maps = (jnp.where(lane < HEAD_DIM, q, zero), jnp.where(lane >= HEAD_DIM, q, zero))
            for m in range(2):
                mx = None
                for slab, start, size, col in chunks:
                    s = lax.dot_general(q_maps[m], k_refs[slab][start:start + size, h * dq:(h + 1) * dq],
                                        (((1,), (1,)), ((), ())), preferred_element_type=F32)
                    s_ref[2 * h + m, :, col:col + size] = s
                    cm = jnp.max(s, axis=1, keepdims=True)
                    mx = cm if mx is None else jnp.maximum(mx, cm)
                m_ref[2 * h + m] = mx

    @pl.when(in_grid)
    def _():
        lam_v = lam_ref[...]
        lam = (jnp.exp(jnp.sum(lam_v[0:1] * lam_v[1:2], axis=1, keepdims=True))
               - jnp.exp(jnp.sum(lam_v[2:3] * lam_v[3:4], axis=1, keepdims=True)) + lam_init)
        for h in range(heads):
            outs = []
            for m in range(2):
                row_max = m_ref[2 * h + m]
                acc = None
                for slab, start, size, col in chunks:
                    p = jnp.exp2(s_ref[2 * h + m, :, col:col + size] - row_max)
                    pv = jnp.dot(p.astype(BF16), v_refs[slab][start:start + size, 2 * h * dv:2 * (h + 1) * dv],
                                 preferred_element_type=F32)
                    acc = pv if acc is None else acc + pv
                outs.append(acc[:, 0:dv] * (1.0 / acc[:, dv:dv + 1]))
            a = outs[0] - lam * outs[1]
            a = a * lax.rsqrt(jnp.mean(a * a, axis=-1, keepdims=True) + EPS) * sg_ref[...] * (1.0 - lam_init)
            o_ref[:, h * dv:(h + 1) * dv] = a.astype(o_ref.dtype)


def _diff_attention(q, k, v1, lam_vecs, sub_g, geom, lam_init):
    dq = q.shape[1]
    batch, seq, ctx_len = geom
    heads = dq // (2 * HEAD_DIM)
    dv = 2 * HEAD_DIM
    tq = DIFF_Q_TILE
    q_tiles = seq // tq
    ctx_blk = seq // ctx_len * batch
    lam_spec = pl.BlockSpec((4, HEAD_DIM), lambda *_: (0, 0))
    sg_spec = pl.BlockSpec((1, dv), lambda *_: (0, 0))
    kv_chunk = min(DIFF_KV_CHUNK, seq)
    assert seq % kv_chunk == 0
    chunks = [(0, c * kv_chunk, kv_chunk, c * kv_chunk) for c in range(seq // kv_chunk)]
    chunks.append((1, 0, ctx_len, seq))
    hp = DIFF_HEADS_PER_STEP
    assert heads % hp == 0
    o_lat = pl.pallas_call(
        functools.partial(_diff_attn_kernel, lam_init=lam_init, n_kv=2, chunks=chunks, heads=hp),
        grid=(batch, heads // hp, q_tiles),
        in_specs=[lam_spec,
                  pl.BlockSpec((tq, hp * 2 * HEAD_DIM), lambda b, h, i: (b * q_tiles + i, h)),
                  pl.BlockSpec((seq, hp * 2 * HEAD_DIM), lambda b, h, i: (b, h)),
                  pl.BlockSpec((ctx_len, hp * 2 * HEAD_DIM), lambda b, h, i: (ctx_blk + b, h)),
                  pl.BlockSpec((seq, hp * 2 * dv), lambda b, h, i: (b, h)),
                  pl.BlockSpec((ctx_len, hp * 2 * dv), lambda b, h, i: (ctx_blk + b, h)),
                  sg_spec],
        out_specs=pl.BlockSpec((tq, hp * dv), lambda b, h, i: (b * q_tiles + i, h)),
        out_shape=jax.ShapeDtypeStruct((batch * seq, heads * dv), BF16),
        scratch_shapes=[pltpu.VMEM((2 * hp, tq, seq + ctx_len), F32), pltpu.VMEM((2 * hp, tq, 1), F32)],
        compiler_params=_cparams("parallel", "parallel", "arbitrary"),
        name="diff_attention",
    )(lam_vecs, q, k, k, v1, v1, sub_g.reshape(1, dv))
    o_ctx = pl.pallas_call(
        functools.partial(_diff_attn_kernel, lam_init=lam_init, n_kv=1, chunks=[(0, 0, ctx_len, 0)], heads=1),
        grid=(batch, heads),
        in_specs=[lam_spec,
                  pl.BlockSpec((ctx_len, 2 * HEAD_DIM), lambda b, h: (ctx_blk + b, h)),
                  pl.BlockSpec((ctx_len, 2 * HEAD_DIM), lambda b, h: (ctx_blk + b, h)),
                  pl.BlockSpec((ctx_len, 2 * dv), lambda b, h: (ctx_blk + b, h)),
                  sg_spec],
        out_specs=pl.BlockSpec((ctx_len, dv), lambda b, h: (b, h)),
        out_shape=jax.ShapeDtypeStruct((batch * ctx_len, heads * dv), BF16),
        scratch_shapes=[pltpu.VMEM((2, ctx_len, ctx_len), F32), pltpu.VMEM((2, ctx_len, 1), F32)],
        compiler_params=_cparams("parallel", "parallel"),
        name="diff_attention_ctx",
    )(lam_vecs, q, k, v1, sub_g.reshape(1, dv))
    return o_lat, o_ctx


def _swa_kernel(sink_ref, q_ref, kl_ref, kc_ref, vl_ref, vc_ref, o_ref, *, latent_tiles, group):
    i = pl.program_id(1)
    tq = q_ref.shape[0]
    seq, ctx_len = kl_ref.shape[0], kc_ref.shape[0]
    span = tq + 2 * WINDOW
    start = pl.multiple_of(jnp.clip(i * tq - WINDOW, 0, seq - span), WINDOW)
    qpos = i * tq + lax.broadcasted_iota(jnp.int32, (tq, span), 0)
    kpos = start + lax.broadcasted_iota(jnp.int32, (tq, span), 1)
    valid = (jnp.abs(kpos - qpos) <= WINDOW) & (i < latent_tiles)
    bias = jnp.where(valid, 0.0, NEG_INF).astype(F32)
    nt = (((1,), (1,)), ((), ()))
    for h in range(SWA_KV_HEADS):
        hs = slice(h * LANES, (h + 1) * LANES)
        kw = kl_ref[pl.ds(start, span), hs]
        vw = vl_ref[pl.ds(start, span), hs]
        kc = kc_ref[:, hs]
        vc = vc_ref[:, hs]
        for g in range(group):
            hq = h * group + g
            q = q_ref[:, hq * LANES:(hq + 1) * LANES]
            s_w = lax.dot_general(q, kw, nt, preferred_element_type=F32) + bias
            s_c = lax.dot_general(q, kc, nt, preferred_element_type=F32)
            sink = sink_ref[hq] * math.log2(math.e)
            m = jnp.maximum(jnp.maximum(jnp.max(s_w, axis=1, keepdims=True), jnp.max(s_c, axis=1, keepdims=True)),
                            sink)
            p_w = jnp.exp2(s_w - m)
            p_c = jnp.exp2(s_c - m)
            denom = (jnp.sum(p_w, axis=1, keepdims=True) + jnp.sum(p_c, axis=1, keepdims=True)
                     + jnp.exp2(sink - m))
            o = (jnp.dot(p_w.astype(BF16), vw, preferred_element_type=F32)
                 + jnp.dot(p_c.astype(BF16), vc, preferred_element_type=F32)) * (1.0 / denom)
            o_ref[:, hq * LANES:(hq + 1) * LANES] = o.astype(o_ref.dtype)


def _swa_attention(q, k, v, sink, geom):
    n, dq = q.shape
    batch, seq, ctx_len = geom
    assert ctx_len == TOKEN_TILE
    lt = geom.lat_tiles
    q_heads = dq // LANES
    kvw = SWA_KV_HEADS * LANES
    ctx_blk = seq // ctx_len * batch
    kern = functools.partial(_swa_kernel, latent_tiles=lt, group=q_heads // SWA_KV_HEADS)

    def tile(b, i):
        return jnp.where(i < lt, b * lt + i, geom.ctx_tile(b))

    lat_kv = pl.BlockSpec((seq, kvw), lambda b, i: (b, 0))
    ctx_kv = pl.BlockSpec((ctx_len, kvw), lambda b, i: (ctx_blk + b, 0))
    return pl.pallas_call(
        kern,
        grid=(batch, lt + 1),
        in_specs=[pl.BlockSpec(memory_space=pltpu.SMEM),
                  pl.BlockSpec((TOKEN_TILE, dq), lambda b, i: (tile(b, i), 0)),
                  lat_kv, ctx_kv, lat_kv, ctx_kv],
        out_specs=pl.BlockSpec((TOKEN_TILE, dq), lambda b, i: (tile(b, i), 0)),
        out_shape=jax.ShapeDtypeStruct((n, dq), BF16),
        compiler_params=_cparams("parallel", "arbitrary"),
        name="swa_attention",
    )(sink, q, k, k, v, v)


def _seq_dft_kernel(base_ref, fine_c_ref, fine_s_ref, cc_ref, cs_ref, zc_ref, zs_ref, yc_ref, ys_ref, o_ref,
                    *, latent_tiles):
    m = pl.program_id(2)

    def mix(a_cos, a_msin, z_cos, z_sin):
        acc = jnp.dot(a_cos, z_cos[...], preferred_element_type=F32)
        acc = acc + jnp.dot(a_msin, z_sin[...], preferred_element_type=F32)
        o_ref[...] = acc.astype(o_ref.dtype)

    @pl.when(m < latent_tiles)
    def _():
        c1, s1 = base_ref[0:1, :], base_ref[1:2, :]
        c2, s2 = fine_c_ref[...], fine_s_ref[...]
        a_cos = (c2 * c1 - s2 * s1).astype(BF16)
        a_msin = (-(c2 * s1) - s2 * c1).astype(BF16)
        mix(a_cos, a_msin, zc_ref, zs_ref)

    @pl.when(m >= latent_tiles)
    def _():
        mix(cc_ref[...], cs_ref[...], yc_ref, ys_ref)


def _dft_angle_tables(n, rows, scale):
    k = jnp.arange(n, dtype=jnp.int32)[None, :]
    step = 2.0 * math.pi / n
    ang0 = (((jnp.arange(n // rows, dtype=jnp.int32) * rows)[:, None] * k) % n).astype(F32) * step
    ang1 = ((jnp.arange(rows, dtype=jnp.int32)[:, None] * k) % n).astype(F32) * step
    base = jnp.stack([jnp.cos(ang0), jnp.sin(ang0)], axis=1)
    return base, jnp.cos(ang1) * scale, jnp.sin(ang1) * scale


def _seq_dft(z, ctx_mats, geom, d):
    batch, seq, ctx_len = geom
    assert ctx_len == TOKEN_TILE and seq % ctx_len == 0
    lt = geom.lat_tiles
    tm = TOKEN_TILE
    tn = d // 2
    n_col = d // tn
    ctx_blk = seq // ctx_len * batch
    lat_tables = _dft_angle_tables(seq, tm, seq ** -0.5)
    fine = pl.BlockSpec((tm, seq), lambda b, j, m: (0, 0))
    ctx_a = pl.BlockSpec((ctx_len, ctx_len), lambda b, j, m: (0, 0))
    return pl.pallas_call(
        functools.partial(_seq_dft_kernel, latent_tiles=lt),
        grid=(batch, n_col, lt + 1),
        in_specs=[pl.BlockSpec((None, 2, seq), lambda b, j, m: (jnp.minimum(m, lt - 1), 0, 0)),
                  fine, fine, ctx_a, ctx_a,
                  pl.BlockSpec((seq, tn), lambda b, j, m: (b, j)),
                  pl.BlockSpec((seq, tn), lambda b, j, m: (b, n_col + j)),
                  pl.BlockSpec((ctx_len, tn), lambda b, j, m: (ctx_blk + b, j)),
                  pl.BlockSpec((ctx_len, tn), lambda b, j, m: (ctx_blk + b, n_col + j))],
        out_specs=pl.BlockSpec((tm, tn), lambda b, j, m: (jnp.where(m < lt, b * lt + m, geom.ctx_tile(b)), j)),
        out_shape=jax.ShapeDtypeStruct((z.shape[0], d), BF16),
        compiler_params=_cparams("parallel", "parallel", "arbitrary"),
        name="seq_dft",
    )(*lat_tables, *ctx_mats, z, z, z, z)


def _dft_mats(n, scale):
    idx = (jnp.arange(n, dtype=jnp.int32)[:, None] * jnp.arange(n, dtype=jnp.int32)[None, :]) % n
    ang = idx.astype(F32) * (2.0 * math.pi / n)
    return (jnp.cos(ang) * scale).astype(BF16), (-jnp.sin(ang) * scale).astype(BF16)


def _load_token_tiles(ref):
    rows = ref.shape[0] // SUBLANES
    return jnp.concatenate([ref[pl.ds(s, rows, stride=SUBLANES), :] for s in range(SUBLANES)], axis=1)


def _store_token_tiles(ref, val):
    rows = ref.shape[0] // SUBLANES
    for s in range(SUBLANES):
        ref[pl.ds(s, rows, stride=SUBLANES), :] = val[:, s * LANES:(s + 1) * LANES]


def _proj_out_kernel(ol_ref, oc_ref, w_ref, x_ref, mod_ref, g_ref, xo_ref, h_ref, *, d, n_lat_tiles):
    def finish(o_ref):
        y = jnp.dot(o_ref[...], w_ref[...], preferred_element_type=F32)
        x = x_ref[...] + mod_ref[:, 2 * d:3 * d] * y
        xo_ref[...] = x
        ms = jnp.mean(x * x, axis=-1, keepdims=True)
        h = x * lax.rsqrt(ms + EPS) * g_ref[...]
        _store_token_tiles(h_ref, h * (1.0 + mod_ref[:, 4 * d:5 * d]) + mod_ref[:, 3 * d:4 * d])

    @pl.when(pl.program_id(0) < n_lat_tiles)
    def _():
        finish(ol_ref)

    @pl.when(pl.program_id(0) >= n_lat_tiles)
    def _():
        finish(oc_ref)


def _proj_out(o_lat, o_ctx, w, xs, mods, g2, geom):
    n, d = xs.shape
    ko = o_lat.shape[1]
    tm = TOKEN_TILE
    nl = geom.n_lat_tiles
    ctx_off = nl if o_ctx.shape[0] == n else 0
    return pl.pallas_call(
        functools.partial(_proj_out_kernel, d=d, n_lat_tiles=nl),
        grid=(n // tm,),
        in_specs=[pl.BlockSpec((tm, ko), lambda i: (jnp.minimum(i, nl - 1), 0)),
                  pl.BlockSpec((tm, ko), lambda i: (jnp.maximum(i, nl) - nl + ctx_off, 0)),
                  pl.BlockSpec((ko, d), lambda i: (0, 0)),
                  pl.BlockSpec((tm, d), lambda i: (i, 0)),
                  pl.BlockSpec((None, 1, mods.shape[-1]), lambda i: (geom.mod_row(i), 0, 0)),
                  pl.BlockSpec((1, d), lambda i: (0, 0))],
        out_specs=[pl.BlockSpec((tm, d), lambda i: (i, 0)),
                   pl.BlockSpec((tm * SUBLANES, LANES), lambda i: (i, 0))],
        out_shape=[jax.ShapeDtypeStruct((n, d), F32), jax.ShapeDtypeStruct((n * SUBLANES, LANES), F32)],
        compiler_params=_cparams("parallel"),
        name="proj_out",
    )(o_lat, o_ctx, w, xs, mods, g2.reshape(1, d))


ROUTER_ROWS = SUBLANES + N_EXPERTS


def _router_kernel(h_ref, wt_ref, b_ref, tri_ref, ints_ref, flt_ref, cnt_ref, carry_ref):
    step = pl.program_id(0)
    tr = h_ref.shape[0] // SUBLANES

    @pl.when(step == 0)
    def _():
        carry_ref[...] = jnp.zeros(carry_ref.shape, F32)

    h = _load_token_tiles(h_ref)
    w = wt_ref[...]
    h_hi, w_hi = h.astype(BF16), w.astype(BF16)
    h_lo = (h - h_hi.astype(F32)).astype(BF16)
    w_lo = (w - w_hi.astype(F32)).astype(BF16)
    nt = (((1,), (1,)), ((), ()))
    logits = (lax.dot_general(w_hi, h_hi, nt, preferred_element_type=F32)
              + lax.dot_general(w_hi, h_lo, nt, preferred_element_type=F32)
              + lax.dot_general(w_lo, h_hi, nt, preferred_element_type=F32)) + b_ref[:, 0:1]
    row8 = lax.broadcasted_iota(jnp.int32, (SUBLANES, tr), 0)
    lg = jnp.where(row8 < N_GROUPS, logits[0:SUBLANES], NEG_INF)
    lg_max = jnp.max(lg, axis=0, keepdims=True)
    pg = 1.0 / jnp.sum(jnp.exp(lg - lg_max), axis=0, keepdims=True)
    grp = jnp.min(jnp.where(lg == lg_max, row8, SUBLANES), axis=0, keepdims=True)
    l2 = jnp.zeros((EXPERTS_PER_GROUP, tr), F32)
    for g in range(N_GROUPS):
        lo = SUBLANES + g * EXPERTS_PER_GROUP
        l2 = l2 + jnp.where(grp == g, logits[lo:lo + EXPERTS_PER_GROUP], 0.0)
    l2_max = jnp.max(l2, axis=0, keepdims=True)
    j0 = jnp.min(jnp.where(l2 == l2_max, row8, SUBLANES), axis=0, keepdims=True)
    rest = jnp.where(row8 == j0, NEG_INF, l2)
    r_max = jnp.max(rest, axis=0, keepdims=True)
    j1 = jnp.min(jnp.where(rest == r_max, row8, SUBLANES), axis=0, keepdims=True)
    e1 = jnp.exp(r_max - l2_max)
    inv = 1.0 / (1.0 + e1)
    w0 = pg * inv
    w1 = pg * e1 * inv
    ex0 = grp * EXPERTS_PER_GROUP + j0
    ex1 = grp * EXPERTS_PER_GROUP + j1

    rows = lax.broadcasted_iota(jnp.int32, (N_EXPERTS, tr), 0)
    oh0 = (rows == ex0).astype(F32)
    oh1 = (rows == ex1).astype(F32)
    both = oh0 + oh1
    before = jnp.dot(both.astype(BF16), tri_ref[...], preferred_element_type=F32) + carry_ref[:, 0:1]
    rank0 = jnp.sum(oh0 * before, axis=0, keepdims=True)
    rank1 = jnp.sum(oh1 * before, axis=0, keepdims=True)
    carry_ref[...] = carry_ref[...] + jnp.sum(both, axis=1, keepdims=True)
    cnt_ref[...] = carry_ref[...]

    zi = jnp.zeros((SUBLANES - 4, tr), jnp.int32)
    ints_ref[...] = jnp.concatenate([ex0, ex1, rank0.astype(jnp.int32), rank1.astype(jnp.int32), zi], axis=0)
    flt_ref[...] = jnp.concatenate([w0, w1, jnp.zeros((SUBLANES - 2, tr), F32)], axis=0)


def _router(h3, wt, bias, tri):
    n = h3.shape[0] // SUBLANES
    d = wt.shape[1]
    tr = ROUTER_TILE
    return pl.pallas_call(
        _router_kernel,
        grid=(n // tr,),
        in_specs=[pl.BlockSpec((tr * SUBLANES, LANES), lambda i: (i, 0)),
                  pl.BlockSpec((ROUTER_ROWS, d), lambda i: (0, 0)),
                  pl.BlockSpec((ROUTER_ROWS, LANES), lambda i: (0, 0)),
                  pl.BlockSpec((tr, tr), lambda i: (0, 0))],
        out_specs=[pl.BlockSpec((SUBLANES, tr), lambda i: (0, i)),
                   pl.BlockSpec((SUBLANES, tr), lambda i: (0, i)),
                   pl.BlockSpec((N_EXPERTS, LANES), lambda i: (0, 0))],
        out_shape=[jax.ShapeDtypeStruct((SUBLANES, n), jnp.int32),
                   jax.ShapeDtypeStruct((SUBLANES, n), F32),
                   jax.ShapeDtypeStruct((N_EXPERTS, LANES), F32)],
        scratch_shapes=[pltpu.VMEM((N_EXPERTS, LANES), F32)],
        compiler_params=_cparams("arbitrary"),
        name="router",
    )(h3, wt, bias, tri)


def _dest_kernel(pstart_ref, ints_ref, o_ref):
    ints = ints_ref[...]
    ex = ints[0:2]
    base = jnp.zeros(ex.shape, jnp.int32)
    for e in range(N_EXPERTS):
        base = jnp.where(ex == e, pstart_ref[e], base)
    o_ref[...] = jnp.concatenate([base + ints[2:4], jnp.zeros((SUBLANES - 2, ints.shape[1]), jnp.int32)], axis=0)


def _dest_rows(pstart, ints):
    n = ints.shape[1]
    tn = 2048 if n % 2048 == 0 else ROUTER_TILE
    return pl.pallas_call(
        _dest_kernel,
        grid=(n // tn,),
        in_specs=[pl.BlockSpec(memory_space=pltpu.SMEM),
                  pl.BlockSpec((SUBLANES, tn), lambda i: (0, i))],
        out_specs=pl.BlockSpec((SUBLANES, tn), lambda i: (0, i)),
        out_shape=jax.ShapeDtypeStruct((SUBLANES, n), jnp.int32),
        compiler_params=_cparams("parallel"),
        name="dest_rows",
    )(pstart, ints)


def _row_maps_kernel(pad_ref, d0_ref, d1_ref, out_ref, *, n_tokens):
    step = pl.program_id(0)
    td = d0_ref.shape[0]
    n_spans = pad_ref.shape[0] // 2

    @pl.when(step == 0)
    def _():
        def span(e, carry):
            first = (pad_ref[e] >> 3) << 3
            groups = (pad_ref[e] + pad_ref[n_spans + e] - first) >> 3

            def fill(c, carry2):
                for u in range(SUBLANES):
                    row = first + c * SUBLANES + u
                    out_ref[row] = 2 * n_tokens + (row & (MOE_HALF - 1))
                return carry2
            lax.fori_loop(0, groups, fill, 0)
            return carry
        lax.fori_loop(0, n_spans, span, 0)

    base = step * td

    def place(r, carry):
        n = base + r
        out_ref[d0_ref[r]] = n
        out_ref[d1_ref[r]] = n_tokens + n
        return carry
    lax.fori_loop(0, td, place, 0, unroll=8)


def _row_maps(pad_spans, dest0, dest1, n_rows):
    n = dest0.shape[0]
    td = 2048 if n % 2048 == 0 else ROUTER_TILE
    smem = functools.partial(pl.BlockSpec, memory_space=pltpu.SMEM)
    return pl.pallas_call(
        functools.partial(_row_maps_kernel, n_tokens=n),
        grid=(n // td,),
        in_specs=[smem(), smem((td,), lambda i: (i,)), smem((td,), lambda i: (i,))],
        out_specs=smem(),
        out_shape=jax.ShapeDtypeStruct((n_rows,), jnp.int32),
        compiler_params=_cparams("arbitrary"),
        name="moe_row_maps",
    )(pad_spans, dest0, dest1)


def _expert_kernel(blk_exp_ref, n_used_ref, map_first, map_half1, map_next, map_prev, map_half0, map_last,
                   h_ref, w13_ref, w2_ref, y_ref, xbuf0, xbuf1, ybuf0, ybuf1, w13b, w2b, gsem, ssem,
                   *, d_expert, n_tokens):
    j = pl.program_id(0)
    n_used = n_used_ref[0]
    xbuf = (xbuf0, xbuf1)
    ybuf = (ybuf0, ybuf1)

    def tile_of(ref, r):
        return ref.at[pl.ds(pl.multiple_of(r * SUBLANES, SUBLANES), SUBLANES)]

    def gather_row(idx_ref, buf, r):
        out_row = idx_ref[r]
        tok = out_row - jnp.where(out_row >= 2 * n_tokens, 2 * n_tokens, jnp.where(out_row >= n_tokens, n_tokens, 0))
        pltpu.make_async_copy(tile_of(h_ref, tok), tile_of(xbuf[buf], r), gsem.at[buf]).start()

    def scatter_row(idx_ref, buf, r):
        pltpu.make_async_copy(tile_of(ybuf[buf], r), tile_of(y_ref, idx_ref[r]), ssem.at[buf]).start()

    def wait_gather(buf):
        pltpu.make_async_copy(h_ref.at[pl.ds(0, MOE_HALF * SUBLANES)], xbuf[buf], gsem.at[buf]).wait()

    def wait_scatter(buf):
        pltpu.make_async_copy(ybuf[buf], y_ref.at[pl.ds(0, MOE_HALF * SUBLANES)], ssem.at[buf]).wait()

    def half_block(buf, gather_idx, scatter_idx):
        for r in range(MOE_HALF):
            gather_row(gather_idx, 1 - buf, r)
            scatter_row(scatter_idx, 1 - buf, r)
        x = _load_token_tiles(xbuf[buf]).astype(BF16)
        gu = jnp.dot(x, w13b[...], preferred_element_type=F32)
        g = gu[:, 0:d_expert]
        u = gu[:, d_expert:2 * d_expert]
        a = g * (1.0 / (1.0 + jnp.exp(-g))) * u
        _store_token_tiles(ybuf[buf], jnp.dot(a.astype(BF16), w2b[...], preferred_element_type=F32))

    @pl.when(j < n_used)
    def _():
        @pl.when(j == 0)
        def _():
            ybuf0[...] = jnp.zeros(ybuf0.shape, F32)
            ybuf1[...] = jnp.zeros(ybuf1.shape, F32)

            def first_rows(r, carry):
                pltpu.make_async_copy(tile_of(ybuf0, r), tile_of(y_ref, 2 * n_tokens + r), ssem.at[0]).start()
                gather_row(map_first, 0, r)
                return carry
            lax.fori_loop(0, MOE_HALF, first_rows, 0)
            wait_scatter(0)

        @pl.when((j == 0) | (blk_exp_ref[j] != blk_exp_ref[jnp.maximum(j - 1, 0)]))
        def _():
            w13b[...] = w13_ref[...].astype(BF16)
            w2b[...] = w2_ref[...].astype(BF16)

        wait_gather(0)

        @pl.when(j > 0)
        def _():
            wait_scatter(0)

        half_block(0, map_half1, map_prev)
        wait_gather(1)
        wait_scatter(1)
        half_block(1, map_next, map_half0)

        @pl.when(j == n_used - 1)
        def _():
            wait_gather(0)
            wait_scatter(0)

            def last_scatter(r, carry):
                scatter_row(map_last, 1, r)
                return carry
            lax.fori_loop(0, MOE_HALF, last_scatter, 0)
            wait_scatter(1)


def _experts(blk_exp, n_used, row_map, h3, w13_all, w2_all, layer):
    n = h3.shape[0] // SUBLANES
    d, d_expert = w13_all.shape[2], w2_all.shape[2]
    n_steps = row_map.shape[0] // MOE_TILE
    smem = functools.partial(pl.BlockSpec, memory_space=pltpu.SMEM)

    def half_spec(half_of):
        def index(j, be, nu):
            last = 2 * nu[0] - 1
            return (jnp.clip(half_of(jnp.minimum(j, nu[0] - 1), last), 0, last),)
        return smem((MOE_HALF,), index)

    def expert(j, be, nu):
        return (layer, be[jnp.minimum(j, nu[0] - 1)], 0, 0)

    token_tile = (MOE_HALF * SUBLANES, LANES)
    grid_spec = pltpu.PrefetchScalarGridSpec(
        num_scalar_prefetch=2,
        grid=(n_steps,),
        in_specs=[half_spec(lambda j, last: 0),
                  half_spec(lambda j, last: 2 * j + 1),
                  half_spec(lambda j, last: 2 * j + 2),
                  half_spec(lambda j, last: 2 * j - 1),
                  half_spec(lambda j, last: 2 * j),
                  half_spec(lambda j, last: last),
                  pl.BlockSpec(memory_space=pl.ANY),
                  pl.BlockSpec((None, None, d, 2 * d_expert), expert),
                  pl.BlockSpec((None, None, d_expert, d), expert)],
        out_specs=pl.BlockSpec(memory_space=pl.ANY),
        scratch_shapes=[pltpu.VMEM(token_tile, F32), pltpu.VMEM(token_tile, F32),
                        pltpu.VMEM(token_tile, F32), pltpu.VMEM(token_tile, F32),
                        pltpu.VMEM((d, 2 * d_expert), BF16), pltpu.VMEM((d_expert, d), BF16),
                        pltpu.SemaphoreType.DMA((2,)), pltpu.SemaphoreType.DMA((2,))],
    )
    return pl.pallas_call(
        functools.partial(_expert_kernel, d_expert=d_expert, n_tokens=n),
        grid_spec=grid_spec,
        out_shape=jax.ShapeDtypeStruct(((2 * n + MOE_HALF) * SUBLANES, LANES), F32),
        compiler_params=_cparams("arbitrary"),
        name="moe_experts",
    )(blk_exp, n_used, row_map, row_map, row_map, row_map, row_map, row_map, h3, w13_all, w2_all)


def _combine_kernel(w_ref, x_ref, mod_ref, y0_ref, y1_ref, o_ref, *, d):
    out = w_ref[:, 0:1] * _load_token_tiles(y0_ref) + w_ref[:, 1:2] * _load_token_tiles(y1_ref)
    o_ref[...] = x_ref[...] + mod_ref[:, 5 * d:6 * d] * out


def _combine(w_cols, xs, mods, y_tok, geom, n_out):
    n, d = xs.shape
    tc = TOKEN_TILE
    return pl.pallas_call(
        functools.partial(_combine_kernel, d=d),
        grid=(n_out // tc,),
        in_specs=[pl.BlockSpec((tc, 2), lambda i: (i, 0)),
                  pl.BlockSpec((tc, d), lambda i: (i, 0)),
                  pl.BlockSpec((None, 1, mods.shape[-1]), lambda i: (geom.mod_row(i), 0, 0)),
                  pl.BlockSpec((tc * SUBLANES, LANES), lambda i: (i, 0)),
                  pl.BlockSpec((tc * SUBLANES, LANES), lambda i: (n // tc + i, 0))],
        out_specs=pl.BlockSpec((tc, d), lambda i: (i, 0)),
        out_shape=jax.ShapeDtypeStruct((n_out, d), F32),
        compiler_params=_cparams("parallel"),
        name="moe_combine",
    )(w_cols, xs, mods, y_tok, y_tok)


def _hier_moe(h3, xs, mods, geom, w_r1, b_r1, w_r2, b_r2, w13_all, w2_all, layer, n_out):
    n, d = xs.shape
    wt = jnp.zeros((ROUTER_ROWS, d), F32)
    wt = wt.at[0:N_GROUPS].set(w_r1.T)
    wt = wt.at[SUBLANES:].set(jnp.transpose(w_r2, (0, 2, 1)).reshape(N_EXPERTS, d))
    bias = jnp.zeros((ROUTER_ROWS,), F32).at[0:N_GROUPS].set(b_r1).at[SUBLANES:].set(b_r2.reshape(-1))
    bias = jnp.broadcast_to(bias[:, None], (ROUTER_ROWS, LANES))
    tri = jnp.asarray(np.triu(np.ones((ROUTER_TILE, ROUTER_TILE), np.float32), 1), BF16)
    ints, flt, cnt = _router(h3, wt, bias, tri)

    counts = cnt[:, 0].astype(jnp.int32)
    padded = ((counts + MOE_TILE - 1) // MOE_TILE) * MOE_TILE
    pend = jnp.cumsum(padded)
    pstart = pend - padded
    n_blocks = (2 * n) // MOE_TILE + N_EXPERTS
    n_rows = n_blocks * MOE_TILE
    blk_row0 = jnp.arange(n_blocks, dtype=jnp.int32) * MOE_TILE
    blk_exp = jnp.minimum(jnp.sum(pend[None, :] <= blk_row0[:, None], axis=1), N_EXPERTS - 1).astype(jnp.int32)

    n_used = (pend[-1] // MOE_TILE).astype(jnp.int32)
    pad_spans = jnp.concatenate([pstart + counts, pend[-1:], padded - counts, n_rows - pend[-1:]]).astype(jnp.int32)

    dest = _dest_rows(pstart.astype(jnp.int32), ints)
    row_map = _row_maps(pad_spans, dest[0], dest[1], n_rows)
    y_tok = _experts(blk_exp, n_used[None], row_map, h3, w13_all, w2_all, layer)
    return _combine(flt[0:2].T, xs, mods, y_tok, geom, n_out)


def _rope_tables(seq, identity_rows):
    ctx_len = identity_rows
    rows = seq // GRID_W
    row = jnp.repeat(jnp.arange(rows, dtype=F32), GRID_W)
    col = jnp.tile(jnp.arange(GRID_W, dtype=F32), rows)
    inv = ROPE_THETA ** (-jnp.arange(ROT_FREQS, dtype=F32) / ROT_FREQS)
    ang_r, ang_c = row[:, None] * inv, col[:, None] * inv
    cos_h = jnp.concatenate([jnp.cos(ang_r)] * 2 + [jnp.cos(ang_c)] * 2, axis=1)
    sin_h = jnp.concatenate([-jnp.sin(ang_r), jnp.sin(ang_r), -jnp.sin(ang_c), jnp.sin(ang_c)], axis=1)
    cos_t = jnp.concatenate([jnp.tile(cos_h, (1, 2)), jnp.ones((ctx_len, LANES), F32)], axis=0)
    sin_t = jnp.concatenate([jnp.tile(sin_h, (1, 2)), jnp.zeros((ctx_len, LANES), F32)], axis=0)
    return cos_t, sin_t


def _pad_heads_cols(w, n_heads):
    d = w.shape[0]
    w3 = w.reshape(d, n_heads, HEAD_DIM)
    return jnp.concatenate([w3, jnp.zeros_like(w3)], axis=2).reshape(d, n_heads * LANES)


def kernel(x, c, ctx, c_ctx, ada_w, ada_b, norm1_g, norm2_g, a_wqkv, a_wo, a_q_norm, a_k_norm, a_lambda_q1, a_lambda_k1, a_lambda_q2, a_lambda_k2, a_subln_g, b_wqkv, b_wo, b_q_norm, b_k_norm, b_sink, f_wo, r_w1, r_b1, r_w2, r_b2, e_w13, e_w2):
    batch, seq, d = x.shape
    ctx_len = ctx.shape[1]
    depth = ada_w.shape[0]
    n = batch * (seq + ctx_len)
    assert seq % TOKEN_TILE == 0 and ctx_len % TOKEN_TILE == 0 and n % ROUTER_TILE == 0
    geom = Geom(batch, seq, ctx_len)

    xs = jnp.concatenate([x.reshape(batch * seq, d), ctx.reshape(batch * ctx_len, d)], axis=0)
    mod_rows = ((batch + 1 + SUBLANES - 1) // SUBLANES) * SUBLANES
    cvec = jnp.zeros((mod_rows, d), F32).at[0:batch].set(c).at[batch].set(c_ctx)
    mods_all = _ada_table(cvec, ada_w, ada_b).reshape(depth, mod_rows, 1, 6 * d)
    cos_t, sin_t = _rope_tables(seq, PROJ_IN_TILE)

    for i in range(depth):
        kind, j = i % N_MIXERS, i // N_MIXERS
        mods = mods_all[i]
        if kind == 0:
            lam_init = 0.8 - 0.6 * math.exp(-0.3 * i)
            nh = d // LANES
            gains = jnp.stack([jnp.tile(a_q_norm[j], 2), jnp.tile(a_k_norm[j], 2)])
            q, k, v1 = _proj_in(xs, mods, norm1_g[i], a_wqkv[j].astype(BF16), gains, cos_t, sin_t, geom,
                                n_q=nh, n_k=nh, n_plain=nh, ones_after_plain=True)
            lam_vecs = jnp.stack([a_lambda_q1[j], a_lambda_k1[j], a_lambda_q2[j], a_lambda_k2[j]])
            o_lat, o_ctx = _diff_attention(q, k, v1, lam_vecs, a_subln_g[j], geom, lam_init)
            wo = a_wo[j].astype(BF16)
        elif kind == 1:
            q_heads = d // HEAD_DIM
            nq, nk = q_heads * HEAD_DIM, SWA_KV_HEADS * HEAD_DIM
            w = b_wqkv[j]
            w_pad = jnp.concatenate([_pad_heads_cols(w[:, :nq], q_heads),
                                     _pad_heads_cols(w[:, nq:nq + nk], SWA_KV_HEADS),
                                     _pad_heads_cols(w[:, nq + nk:], SWA_KV_HEADS)], axis=1).astype(BF16)
            zeros64 = jnp.zeros((HEAD_DIM,), F32)
            gains = jnp.stack([jnp.concatenate([b_q_norm[j], zeros64]), jnp.concatenate([b_k_norm[j], zeros64])])
            q, k, v = _proj_in(xs, mods, norm1_g[i], w_pad, gains, cos_t, sin_t, geom,
                               n_q=q_heads, n_k=SWA_KV_HEADS, n_plain=SWA_KV_HEADS)
            o_lat = o_ctx = _swa_attention(q, k, v, b_sink[j], geom)
            wo3 = b_wo[j].reshape(q_heads, HEAD_DIM, d)
            wo = jnp.concatenate([wo3, jnp.zeros_like(wo3)], axis=1).reshape(q_heads * LANES, d).astype(BF16)
        else:
            gd = d // FOURIER_GROUPS
            cd, msd = _dft_mats(gd, gd ** -0.5)
            eye = jnp.eye(FOURIER_GROUPS, dtype=BF16)
            w_cs = jnp.concatenate([jnp.kron(eye, cd), jnp.kron(eye, -msd)], axis=1)
            (z,) = _proj_in(xs, mods, norm1_g[i], w_cs, jnp.zeros((2, LANES), F32), cos_t, sin_t, geom,
                            n_q=0, n_k=0, n_plain=2 * d // LANES)
            o_lat = o_ctx = _seq_dft(z, _dft_mats(ctx_len, ctx_len ** -0.5), geom, d)
            wo = f_wo[j].astype(BF16)
        xs, h2 = _proj_out(o_lat, o_ctx, wo, xs, mods, norm2_g[i], geom)
        n_out = n if i < depth - 1 else batch * seq
        xs = _hier_moe(h2, xs, mods, geom, r_w1[i], r_b1[i], r_w2[i], r_b2[i], e_w13, e_w2, i, n_out)

    return xs.reshape(batch, seq, d)
```

```python
import functools
import math
from typing import NamedTuple

import numpy as np
import jax
import jax.numpy as jnp
from jax import lax
from jax.experimental import pallas as pl
from jax.experimental.pallas import tpu as pltpu

F32 = jnp.float32
BF16 = jnp.bfloat16
HIGHEST = lax.Precision.HIGHEST

GRID_W = 64
HEAD_DIM = 64
ROT_FREQS = HEAD_DIM // 4
ROPE_THETA = 10000.0
WINDOW = 128
N_MIXERS = 3
SWA_KV_HEADS = 4
FOURIER_GROUPS = 4
N_GROUPS = 4
EXPERTS_PER_GROUP = 8
N_EXPERTS = N_GROUPS * EXPERTS_PER_GROUP
EPS = 1e-6
NEG_INF = -1e30

LANES = 128
SUBLANES = 8
MXU_DIM = 256
TOKEN_TILE = 256
PROJ_IN_TILE = 512
MOE_HALF = 256
MOE_TILE = 2 * MOE_HALF
ROUTER_TILE = 512
VMEM_LIMIT = 48 * 1024 * 1024


def _cparams(*sem):
    return pltpu.CompilerParams(dimension_semantics=sem, vmem_limit_bytes=VMEM_LIMIT)


class Geom(NamedTuple):
    batch: int
    seq: int
    ctx_len: int

    @property
    def lat_tiles(self):
        return self.seq // TOKEN_TILE

    @property
    def n_lat_tiles(self):
        return self.batch * self.lat_tiles

    @property
    def n_tiles(self):
        return self.n_lat_tiles + self.batch * (self.ctx_len // TOKEN_TILE)

    def mod_row(self, i):
        return jnp.where(i < self.n_lat_tiles, i // self.lat_tiles, self.batch)

    def ctx_tile(self, b):
        return self.n_lat_tiles + b


def _ada_kernel(c_ref, w_ref, b_ref, o_ref):
    c = c_ref[...]
    s = c * (1.0 / (1.0 + jnp.exp(-c)))
    o_ref[...] = jnp.dot(s, w_ref[...], precision=HIGHEST, preferred_element_type=F32) + b_ref[...]


def _ada_table(cvec, ada_w, ada_b):
    depth, d, n6 = ada_w.shape
    rows = cvec.shape[0]
    tn = n6 // 4
    return pl.pallas_call(
        _ada_kernel,
        grid=(depth, n6 // tn),
        in_specs=[pl.BlockSpec((rows, d), lambda l, j: (0, 0)),
                  pl.BlockSpec((None, d, tn), lambda l, j: (l, 0, j)),
                  pl.BlockSpec((None, 1, tn), lambda l, j: (l, 0, j))],
        out_specs=pl.BlockSpec((None, rows, tn), lambda l, j: (l, 0, j)),
        out_shape=jax.ShapeDtypeStruct((depth, rows, n6), F32),
        compiler_params=_cparams("parallel", "parallel"),
        name="ada_table",
    )(cvec, ada_w, ada_b.reshape(depth, 1, n6))


def _head_mean_matrix():
    r = np.arange(LANES)
    return jnp.asarray((r[:, None] // HEAD_DIM == r[None, :] // HEAD_DIM).astype(np.float32) / HEAD_DIM, BF16)


def _proj_in_kernel(x_ref, mod_ref, g_ref, w_ref, hm_ref, gain_ref, cos_ref, sin_ref, *out_refs,
                    d, n_q, n_k, n_plain, ones_after_plain, q_scale):
    q_ref = out_refs[0] if n_q else None
    k_ref = out_refs[1] if n_k else None
    p_ref = out_refs[-2]
    y_ref = out_refs[-1]
    n_chunks = n_q + n_k + n_plain
    in_grid = pl.program_id(0) < pl.num_programs(0)

    @pl.when(in_grid)
    def _():
        x = x_ref[...]
        ms = jnp.mean(x * x, axis=-1, keepdims=True)
        h = x * lax.rsqrt(ms + EPS) * g_ref[...]
        h = h * (1.0 + mod_ref[:, d:2 * d]) + mod_ref[:, 0:d]
        hb = h.astype(BF16)
        for c2 in range(0, n_chunks, 2):
            cols = slice(c2 * LANES, (c2 + min(2, n_chunks - c2)) * LANES)
            y_ref[:, cols] = jnp.dot(hb, w_ref[:, cols], preferred_element_type=F32)

    @pl.when(in_grid)
    def _():
        lane = lax.broadcasted_iota(jnp.int32, (1, LANES), 1)
        first_half = (lane // ROT_FREQS) % 2 == 0
        for c in range(n_chunks):
            y = y_ref[:, c * LANES:(c + 1) * LANES]
            if c < n_q + n_k:
                msq = jnp.dot((y * y).astype(BF16), hm_ref[...], preferred_element_type=F32)
                is_q = c < n_q
                gain = gain_ref[0:1, :] if is_q else gain_ref[1:2, :]
                yn = y * lax.rsqrt(msq + EPS) * gain
                partner = jnp.where(first_half, pltpu.roll(yn, LANES - ROT_FREQS, 1), pltpu.roll(yn, ROT_FREQS, 1))
                out = yn * cos_ref[...] + partner * sin_ref[...]
                if is_q:
                    q_ref[:, c * LANES:(c + 1) * LANES] = (out * q_scale).astype(q_ref.dtype)
                else:
                    ck = c - n_q
                    k_ref[:, ck * LANES:(ck + 1) * LANES] = out.astype(k_ref.dtype)
            else:
                cp = c - n_q - n_k
                if ones_after_plain:
                    p_ref[:, 2 * cp * LANES:(2 * cp + 1) * LANES] = y.astype(p_ref.dtype)
                    p_ref[:, (2 * cp + 1) * LANES:(2 * cp + 2) * LANES] = jnp.ones(y.shape, p_ref.dtype)
                else:
                    p_ref[:, cp * LANES:(cp + 1) * LANES] = y.astype(p_ref.dtype)


Q_SCALE_LOG2 = HEAD_DIM ** -0.5 * math.log2(math.e)


def _proj_in(xs, mods, g, w, gains, cos_t, sin_t, geom, *, n_q, n_k, n_plain, ones_after_plain=False,
             q_scale=Q_SCALE_LOG2):
    n, d = xs.shape
    tm = PROJ_IN_TILE
    lat_tiles = geom.seq // tm
    n_lat = geom.batch * lat_tiles
    assert geom.seq % tm == 0 and n % tm == 0 and cos_t.shape[0] == geom.seq + tm

    def mod_row(i):
        return jnp.where(i < n_lat, i // lat_tiles, geom.batch)

    def rope_row(i):
        return jnp.where(i < n_lat, i % lat_tiles, lat_tiles)

    n_cols = (n_q + n_k + n_plain) * LANES
    assert w.shape == (d, n_cols)
    out_shape, out_specs = [], []
    for cnt in (n_q, n_k):
        if cnt:
            out_shape.append(jax.ShapeDtypeStruct((n, cnt * LANES), BF16))
            out_specs.append(pl.BlockSpec((tm, cnt * LANES), lambda i: (i, 0)))
    pw = n_plain * LANES * (2 if ones_after_plain else 1)
    out_shape.append(jax.ShapeDtypeStruct((n, pw), BF16))
    out_specs.append(pl.BlockSpec((tm, pw), lambda i: (i, 0)))
    kern = functools.partial(_proj_in_kernel, d=d, n_q=n_q, n_k=n_k, n_plain=n_plain,
                             ones_after_plain=ones_after_plain, q_scale=q_scale)
    return pl.pallas_call(
        kern,
        grid=(n // tm,),
        in_specs=[pl.BlockSpec((tm, d), lambda i: (i, 0)),
                  pl.BlockSpec((None, 1, mods.shape[-1]), lambda i: (mod_row(i), 0, 0)),
                  pl.BlockSpec((1, d), lambda i: (0, 0)),
                  pl.BlockSpec((d, n_cols), lambda i: (0, 0)),
                  pl.BlockSpec((LANES, LANES), lambda i: (0, 0)),
                  pl.BlockSpec((2, LANES), lambda i: (0, 0)),
                  pl.BlockSpec((tm, LANES), lambda i: (rope_row(i), 0)),
                  pl.BlockSpec((tm, LANES), lambda i: (rope_row(i), 0))],
        out_specs=out_specs,
        out_shape=out_shape,
        scratch_shapes=[pltpu.VMEM((tm, n_cols), F32)],
        compiler_params=_cparams("parallel"),
        name="proj_in",
    )(xs, mods, g.reshape(1, d), w, _head_mean_matrix(), gains, cos_t, sin_t)


DIFF_KV_CHUNK = 1024
DIFF_Q_TILE = 256
DIFF_HEADS_PER_STEP = 2


def _diff_attn_kernel(*refs, lam_init, n_kv, chunks, heads):
    lam_ref, q_ref = refs[0], refs[1]
    k_refs = refs[2:2 + n_kv]
    v_refs = refs[2 + n_kv:2 + 2 * n_kv]
    sg_ref, o_ref, s_ref, m_ref = refs[2 + 2 * n_kv:]
    dq = 2 * HEAD_DIM
    dv = o_ref.shape[1] // heads
    in_grid = pl.program_id(0) < pl.num_programs(0)

    @pl.when(in_grid)
    def _():
        for h in range(heads):
            q = q_ref[:, h * dq:(h + 1) * dq]
            lane = lax.broadcasted_iota(jnp.int32, q.shape, 1)
            zero = jnp.zeros_like(q)
            q_maps = (jnp.where(lane < HEAD_DIM, q, zero), jnp.where(lane >= HEAD_DIM, q, zero))
            for m in range(2):
                mx = None
                for slab, start, size, col in chunks:
                    s = lax.dot_general(q_maps[m], k_refs[slab][start:start + size, h * dq:(h + 1) * dq],
                                        (((1,), (1,)), ((), ())), preferred_element_type=F32)
                    s_ref[2 * h + m, :, col:col + size] = s
                    cm = jnp.max(s, axis=1, keepdims=True)
                    mx = cm if mx is None else jnp.maximum(mx, cm)
                m_ref[2 * h + m] = mx

    @pl.when(in_grid)
    def _():
        lam_v = lam_ref[...]
        lam = (jnp.exp(jnp.sum(lam_v[0:1] * lam_v[1:2], axis=1, keepdims=True))
               - jnp.exp(jnp.sum(lam_v[2:3] * lam_v[3:4], axis=1, keepdims=True)) + lam_init)
        for h in range(heads):
            outs = []
            for m in range(2):
                row_max = m_ref[2 * h + m]
                acc = None
                for slab, start, size, col in chunks:
                    p = jnp.exp2(s_ref[2 * h + m, :, col:col + size] - row_max)
                    pv = jnp.dot(p.astype(BF16), v_refs[slab][start:start + size, 2 * h * dv:2 * (h + 1) * dv],
                                 preferred_element_type=F32)
                    acc = pv if acc is None else acc + pv
                outs.append(acc[:, 0:dv] * (1.0 / acc[:, dv:dv + 1]))
            a = outs[0] - lam * outs[1]
            a = a * lax.rsqrt(jnp.mean(a * a, axis=-1, keepdims=True) + EPS) * sg_ref[...] * (1.0 - lam_init)
            o_ref[:, h * dv:(h + 1) * dv] = a.astype(o_ref.dtype)


def _diff_attention(q, k, v1, lam_vecs, sub_g, geom, lam_init):
    dq = q.shape[1]
    batch, seq, ctx_len = geom
    heads = dq // (2 * HEAD_DIM)
    dv = 2 * HEAD_DIM
    tq = DIFF_Q_TILE
    q_tiles = seq // tq
    ctx_blk = seq // ctx_len * batch
    lam_spec = pl.BlockSpec((4, HEAD_DIM), lambda *_: (0, 0))
    sg_spec = pl.BlockSpec((1, dv), lambda *_: (0, 0))
    kv_chunk = min(DIFF_KV_CHUNK, seq)
    assert seq % kv_chunk == 0
    chunks = [(0, c * kv_chunk, kv_chunk, c * kv_chunk) for c in range(seq // kv_chunk)]
    chunks.append((1, 0, ctx_len, seq))
    hp = DIFF_HEADS_PER_STEP
    assert heads % hp == 0
    o_lat = pl.pallas_call(
        functools.partial(_diff_attn_kernel, lam_init=lam_init, n_kv=2, chunks=chunks, heads=hp),
        grid=(batch, heads // hp, q_tiles),
        in_specs=[lam_spec,
                  pl.BlockSpec((tq, hp * 2 * HEAD_DIM), lambda b, h, i: (b * q_tiles + i, h)),
                  pl.BlockSpec((seq, hp * 2 * HEAD_DIM), lambda b, h, i: (b, h)),
                  pl.BlockSpec((ctx_len, hp * 2 * HEAD_DIM), lambda b, h, i: (ctx_blk + b, h)),
                  pl.BlockSpec((seq, hp * 2 * dv), lambda b, h, i: (b, h)),
                  pl.BlockSpec((ctx_len, hp * 2 * dv), lambda b, h, i: (ctx_blk + b, h)),
                  sg_spec],
        out_specs=pl.BlockSpec((tq, hp * dv), lambda b, h, i: (b * q_tiles + i, h)),
        out_shape=jax.ShapeDtypeStruct((batch * seq, heads * dv), BF16),
        scratch_shapes=[pltpu.VMEM((2 * hp, tq, seq + ctx_len), F32), pltpu.VMEM((2 * hp, tq, 1), F32)],
        compiler_params=_cparams("parallel", "parallel", "arbitrary"),
        name="diff_attention",
    )(lam_vecs, q, k, k, v1, v1, sub_g.reshape(1, dv))
    o_ctx = pl.pallas_call(
        functools.partial(_diff_attn_kernel, lam_init=lam_init, n_kv=1, chunks=[(0, 0, ctx_len, 0)], heads=1),
        grid=(batch, heads),
        in_specs=[lam_spec,
                  pl.BlockSpec((ctx_len, 2 * HEAD_DIM), lambda b, h: (ctx_blk + b, h)),
                  pl.BlockSpec((ctx_len, 2 * HEAD_DIM), lambda b, h: (ctx_blk + b, h)),
                  pl.BlockSpec((ctx_len, 2 * dv), lambda b, h: (ctx_blk + b, h)),
                  sg_spec],
        out_specs=pl.BlockSpec((ctx_len, dv), lambda b, h: (b, h)),
        out_shape=jax.ShapeDtypeStruct((batch * ctx_len, heads * dv), BF16),
        scratch_shapes=[pltpu.VMEM((2, ctx_len, ctx_len), F32), pltpu.VMEM((2, ctx_len, 1), F32)],
        compiler_params=_cparams("parallel", "parallel"),
        name="diff_attention_ctx",
    )(lam_vecs, q, k, v1, sub_g.reshape(1, dv))
    return o_lat, o_ctx


def _swa_kernel(sink_ref, q_ref, kl_ref, kc_ref, vl_ref, vc_ref, o_ref, *, latent_tiles, group):
    i = pl.program_id(1)
    tq = q_ref.shape[0]
    seq, ctx_len = kl_ref.shape[0], kc_ref.shape[0]
    span = tq + 2 * WINDOW
    start = pl.multiple_of(jnp.clip(i * tq - WINDOW, 0, seq - span), WINDOW)
    qpos = i * tq + lax.broadcasted_iota(jnp.int32, (tq, span), 0)
    kpos = start + lax.broadcasted_iota(jnp.int32, (tq, span), 1)
    valid = (jnp.abs(kpos - qpos) <= WINDOW) & (i < latent_tiles)
    bias = jnp.where(valid, 0.0, NEG_INF).astype(F32)
    nt = (((1,), (1,)), ((), ()))
    for h in range(SWA_KV_HEADS):
        hs = slice(h * LANES, (h + 1) * LANES)
        kw = kl_ref[pl.ds(start, span), hs]
        vw = vl_ref[pl.ds(start, span), hs]
        kc = kc_ref[:, hs]
        vc = vc_ref[:, hs]
        for g in range(group):
            hq = h * group + g
            q = q_ref[:, hq * LANES:(hq + 1) * LANES]
            s_w = lax.dot_general(q, kw, nt, preferred_element_type=F32) + bias
            s_c = lax.dot_general(q, kc, nt, preferred_element_type=F32)
            sink = sink_ref[hq] * math.log2(math.e)
            m = jnp.maximum(jnp.maximum(jnp.max(s_w, axis=1, keepdims=True), jnp.max(s_c, axis=1, keepdims=True)),
                            sink)
            p_w = jnp.exp2(s_w - m)
            p_c = jnp.exp2(s_c - m)
            denom = (jnp.sum(p_w, axis=1, keepdims=True) + jnp.sum(p_c, axis=1, keepdims=True)
                     + jnp.exp2(sink - m))
            o = (jnp.dot(p_w.astype(BF16), vw, preferred_element_type=F32)
                 + jnp.dot(p_c.astype(BF16), vc, preferred_element_type=F32)) * (1.0 / denom)
            o_ref[:, hq * LANES:(hq + 1) * LANES] = o.astype(o_ref.dtype)


def _swa_attention(q, k, v, sink, geom):
    n, dq = q.shape
    batch, seq, ctx_len = geom
    assert ctx_len == TOKEN_TILE
    lt = geom.lat_tiles
    q_heads = dq // LANES
    kvw = SWA_KV_HEADS * LANES
    ctx_blk = seq // ctx_len * batch
    kern = functools.partial(_swa_kernel, latent_tiles=lt, group=q_heads // SWA_KV_HEADS)

    def tile(b, i):
        return jnp.where(i < lt, b * lt + i, geom.ctx_tile(b))

    lat_kv = pl.BlockSpec((seq, kvw), lambda b, i: (b, 0))
    ctx_kv = pl.BlockSpec((ctx_len, kvw), lambda b, i: (ctx_blk + b, 0))
    return pl.pallas_call(
        kern,
        grid=(batch, lt + 1),
        in_specs=[pl.BlockSpec(memory_space=pltpu.SMEM),
                  pl.BlockSpec((TOKEN_TILE, dq), lambda b, i: (tile(b, i), 0)),
                  lat_kv, ctx_kv, lat_kv, ctx_kv],
        out_specs=pl.BlockSpec((TOKEN_TILE, dq), lambda b, i: (tile(b, i), 0)),
        out_shape=jax.ShapeDtypeStruct((n, dq), BF16),
        compiler_params=_cparams("parallel", "arbitrary"),
        name="swa_attention",
    )(sink, q, k, k, v, v)


def _seq_dft_kernel(base_ref, fine_c_ref, fine_s_ref, cc_ref, cs_ref, zc_ref, zs_ref, yc_ref, ys_ref, o_ref,
                    *, latent_tiles):
    m = pl.program_id(2)

    def mix(a_cos, a_msin, z_cos, z_sin):
        acc = jnp.dot(a_cos, z_cos[...], preferred_element_type=F32)
        acc = acc + jnp.dot(a_msin, z_sin[...], preferred_element_type=F32)
        o_ref[...] = acc.astype(o_ref.dtype)

    @pl.when(m < latent_tiles)
    def _():
        c1, s1 = base_ref[0:1, :], base_ref[1:2, :]
        c2, s2 = fine_c_ref[...], fine_s_ref[...]
        a_cos = (c2 * c1 - s2 * s1).astype(BF16)
        a_msin = (-(c2 * s1) - s2 * c1).astype(BF16)
        mix(a_cos, a_msin, zc_ref, zs_ref)

    @pl.when(m >= latent_tiles)
    def _():
        mix(cc_ref[...], cs_ref[...], yc_ref, ys_ref)


def _dft_angle_tables(n, rows, scale):
    k = jnp.arange(n, dtype=jnp.int32)[None, :]
    step = 2.0 * math.pi / n
    ang0 = (((jnp.arange(n // rows, dtype=jnp.int32) * rows)[:, None] * k) % n).astype(F32) * step
    ang1 = ((jnp.arange(rows, dtype=jnp.int32)[:, None] * k) % n).astype(F32) * step
    base = jnp.stack([jnp.cos(ang0), jnp.sin(ang0)], axis=1)
    return base, jnp.cos(ang1) * scale, jnp.sin(ang1) * scale


def _seq_dft(z, ctx_mats, geom, d):
    batch, seq, ctx_len = geom
    assert ctx_len == TOKEN_TILE and seq % ctx_len == 0
    lt = geom.lat_tiles
    tm = TOKEN_TILE
    tn = d // 2
    n_col = d // tn
    ctx_blk = seq // ctx_len * batch
    lat_tables = _dft_angle_tables(seq, tm, seq ** -0.5)
    fine = pl.BlockSpec((tm, seq), lambda b, j, m: (0, 0))
    ctx_a = pl.BlockSpec((ctx_len, ctx_len), lambda b, j, m: (0, 0))
    return pl.pallas_call(
        functools.partial(_seq_dft_kernel, latent_tiles=lt),
        grid=(batch, n_col, lt + 1),
        in_specs=[pl.BlockSpec((None, 2, seq), lambda b, j, m: (jnp.minimum(m, lt - 1), 0, 0)),
                  fine, fine, ctx_a, ctx_a,
                  pl.BlockSpec((seq, tn), lambda b, j, m: (b, j)),
                  pl.BlockSpec((seq, tn), lambda b, j, m: (b, n_col + j)),
                  pl.BlockSpec((ctx_len, tn), lambda b, j, m: (ctx_blk + b, j)),
                  pl.BlockSpec((ctx_len, tn), lambda b, j, m: (ctx_blk + b, n_col + j))],
        out_specs=pl.BlockSpec((tm, tn), lambda b, j, m: (jnp.where(m < lt, b * lt + m, geom.ctx_tile(b)), j)),
        out_shape=jax.ShapeDtypeStruct((z.shape[0], d), BF16),
        compiler_params=_cparams("parallel", "parallel", "arbitrary"),
        name="seq_dft",
    )(*lat_tables, *ctx_mats, z, z, z, z)


def _dft_mats(n, scale):
    idx = (jnp.arange(n, dtype=jnp.int32)[:, None] * jnp.arange(n, dtype=jnp.int32)[None, :]) % n
    ang = idx.astype(F32) * (2.0 * math.pi / n)
    return (jnp.cos(ang) * scale).astype(BF16), (-jnp.sin(ang) * scale).astype(BF16)


def _load_token_tiles(ref):
    rows = ref.shape[0] // SUBLANES
    return jnp.concatenate([ref[pl.ds(s, rows, stride=SUBLANES), :] for s in range(SUBLANES)], axis=1)


def _store_token_tiles(ref, val):
    rows = ref.shape[0] // SUBLANES
    for s in range(SUBLANES):
        ref[pl.ds(s, rows, stride=SUBLANES), :] = val[:, s * LANES:(s + 1) * LANES]


def _proj_out_kernel(ol_ref, oc_ref, w_ref, x_ref, mod_ref, g_ref, xo_ref, h_ref, *, d, n_lat_tiles):
    def finish(o_ref):
        y = jnp.dot(o_ref[...], w_ref[...], preferred_element_type=F32)
        x = x_ref[...] + mod_ref[:, 2 * d:3 * d] * y
        xo_ref[...] = x
        ms = jnp.mean(x * x, axis=-1, keepdims=True)
        h = x * lax.rsqrt(ms + EPS) * g_ref[...]
        _store_token_tiles(h_ref, h * (1.0 + mod_ref[:, 4 * d:5 * d]) + mod_ref[:, 3 * d:4 * d])

    @pl.when(pl.program_id(0) < n_lat_tiles)
    def _():
        finish(ol_ref)

    @pl.when(pl.program_id(0) >= n_lat_tiles)
    def _():
        finish(oc_ref)


def _proj_out(o_lat, o_ctx, w, xs, mods, g2, geom):
    n, d = xs.shape
    ko = o_lat.shape[1]
    tm = TOKEN_TILE
    nl = geom.n_lat_tiles
    ctx_off = nl if o_ctx.shape[0] == n else 0
    return pl.pallas_call(
        functools.partial(_proj_out_kernel, d=d, n_lat_tiles=nl),
        grid=(n // tm,),
        in_specs=[pl.BlockSpec((tm, ko), lambda i: (jnp.minimum(i, nl - 1), 0)),
                  pl.BlockSpec((tm, ko), lambda i: (jnp.maximum(i, nl) - nl + ctx_off, 0)),
                  pl.BlockSpec((ko, d), lambda i: (0, 0)),
                  pl.BlockSpec((tm, d), lambda i: (i, 0)),
                  pl.BlockSpec((None, 1, mods.shape[-1]), lambda i: (geom.mod_row(i), 0, 0)),
                  pl.BlockSpec((1, d), lambda i: (0, 0))],
        out_specs=[pl.BlockSpec((tm, d), lambda i: (i, 0)),
                   pl.BlockSpec((tm * SUBLANES, LANES), lambda i: (i, 0))],
        out_shape=[jax.ShapeDtypeStruct((n, d), F32), jax.ShapeDtypeStruct((n * SUBLANES, LANES), F32)],
        compiler_params=_cparams("parallel"),
        name="proj_out",
    )(o_lat, o_ctx, w, xs, mods, g2.reshape(1, d))


ROUTER_ROWS = SUBLANES + N_EXPERTS


def _router_kernel(h_ref, wt_ref, b_ref, tri_ref, ints_ref, flt_ref, cnt_ref, carry_ref):
    step = pl.program_id(0)
    tr = h_ref.shape[0] // SUBLANES

    @pl.when(step == 0)
    def _():
        carry_ref[...] = jnp.zeros(carry_ref.shape, F32)

    h = _load_token_tiles(h_ref)
    w = wt_ref[...]
    h_hi, w_hi = h.astype(BF16), w.astype(BF16)
    h_lo = (h - h_hi.astype(F32)).astype(BF16)
    w_lo = (w - w_hi.astype(F32)).astype(BF16)
    nt = (((1,), (1,)), ((), ()))
    logits = (lax.dot_general(w_hi, h_hi, nt, preferred_element_type=F32)
              + lax.dot_general(w_hi, h_lo, nt, preferred_element_type=F32)
              + lax.dot_general(w_lo, h_hi, nt, preferred_element_type=F32)) + b_ref[:, 0:1]
    row8 = lax.broadcasted_iota(jnp.int32, (SUBLANES, tr), 0)
    lg = jnp.where(row8 < N_GROUPS, logits[0:SUBLANES], NEG_INF)
    lg_max = jnp.max(lg, axis=0, keepdims=True)
    pg = 1.0 / jnp.sum(jnp.exp(lg - lg_max), axis=0, keepdims=True)
    grp = jnp.min(jnp.where(lg == lg_max, row8, SUBLANES), axis=0, keepdims=True)
    l2 = jnp.zeros((EXPERTS_PER_GROUP, tr), F32)
    for g in range(N_GROUPS):
        lo = SUBLANES + g * EXPERTS_PER_GROUP
        l2 = l2 + jnp.where(grp == g, logits[lo:lo + EXPERTS_PER_GROUP], 0.0)
    l2_max = jnp.max(l2, axis=0, keepdims=True)
    j0 = jnp.min(jnp.where(l2 == l2_max, row8, SUBLANES), axis=0, keepdims=True)
    rest = jnp.where(row8 == j0, NEG_INF, l2)
    r_max = jnp.max(rest, axis=0, keepdims=True)
    j1 = jnp.min(jnp.where(rest == r_max, row8, SUBLANES), axis=0, keepdims=True)
    e1 = jnp.exp(r_max - l2_max)
    inv = 1.0 / (1.0 + e1)
    w0 = pg * inv
    w1 = pg * e1 * inv
    ex0 = grp * EXPERTS_PER_GROUP + j0
    ex1 = grp * EXPERTS_PER_GROUP + j1

    rows = lax.broadcasted_iota(jnp.int32, (N_EXPERTS, tr), 0)
    oh0 = (rows == ex0).astype(F32)
    oh1 = (rows == ex1).astype(F32)
    both = oh0 + oh1
    before = jnp.dot(both.astype(BF16), tri_ref[...], preferred_element_type=F32) + carry_ref[:, 0:1]
    rank0 = jnp.sum(oh0 * before, axis=0, keepdims=True)
    rank1 = jnp.sum(oh1 * before, axis=0, keepdims=True)
    carry_ref[...] = carry_ref[...] + jnp.sum(both, axis=1, keepdims=True)
    cnt_ref[...] = carry_ref[...]

    zi = jnp.zeros((SUBLANES - 4, tr), jnp.int32)
    ints_ref[...] = jnp.concatenate([ex0, ex1, rank0.astype(jnp.int32), rank1.astype(jnp.int32), zi], axis=0)
    flt_ref[...] = jnp.concatenate([w0, w1, jnp.zeros((SUBLANES - 2, tr), F32)], axis=0)


def _router(h3, wt, bias, tri):
    n = h3.shape[0] // SUBLANES
    d = wt.shape[1]
    tr = ROUTER_TILE
    return pl.pallas_call(
        _router_kernel,
        grid=(n // tr,),
        in_specs=[pl.BlockSpec((tr * SUBLANES, LANES), lambda i: (i, 0)),
                  pl.BlockSpec((ROUTER_ROWS, d), lambda i: (0, 0)),
                  pl.BlockSpec((ROUTER_ROWS, LANES), lambda i: (0, 0)),
                  pl.BlockSpec((tr, tr), lambda i: (0, 0))],
        out_specs=[pl.BlockSpec((SUBLANES, tr), lambda i: (0, i)),
                   pl.BlockSpec((SUBLANES, tr), lambda i: (0, i)),
                   pl.BlockSpec((N_EXPERTS, LANES), lambda i: (0, 0))],
        out_shape=[jax.ShapeDtypeStruct((SUBLANES, n), jnp.int32),
                   jax.ShapeDtypeStruct((SUBLANES, n), F32),
                   jax.ShapeDtypeStruct((N_EXPERTS, LANES), F32)],
        scratch_shapes=[pltpu.VMEM((N_EXPERTS, LANES), F32)],
        compiler_params=_cparams("arbitrary"),
        name="router",
    )(h3, wt, bias, tri)


def _dest_kernel(pstart_ref, ints_ref, o_ref):
    ints = ints_ref[...]
    ex = ints[0:2]
    base = jnp.zeros(ex.shape, jnp.int32)
    for e in range(N_EXPERTS):
        base = jnp.where(ex == e, pstart_ref[e], base)
    o_ref[...] = jnp.concatenate([base + ints[2:4], jnp.zeros((SUBLANES - 2, ints.shape[1]), jnp.int32)], axis=0)


def _dest_rows(pstart, ints):
    n = ints.shape[1]
    tn = 2048 if n % 2048 == 0 else ROUTER_TILE
    return pl.pallas_call(
        _dest_kernel,
        grid=(n // tn,),
        in_specs=[pl.BlockSpec(memory_space=pltpu.SMEM),
                  pl.BlockSpec((SUBLANES, tn), lambda i: (0, i))],
        out_specs=pl.BlockSpec((SUBLANES, tn), lambda i: (0, i)),
        out_shape=jax.ShapeDtypeStruct((SUBLANES, n), jnp.int32),
        compiler_params=_cparams("parallel"),
        name="dest_rows",
    )(pstart, ints)


def _row_maps_kernel(pad_ref, d0_ref, d1_ref, out_ref, *, n_tokens):
    step = pl.program_id(0)
    td = d0_ref.shape[0]
    n_spans = pad_ref.shape[0] // 2

    @pl.when(step == 0)
    def _():
        def span(e, carry):
            first = (pad_ref[e] >> 3) << 3
            groups = (pad_ref[e] + pad_ref[n_spans + e] - first) >> 3

            def fill(c, carry2):
                for u in range(SUBLANES):
                    row = first + c * SUBLANES + u
                    out_ref[row] = 2 * n_tokens + (row & (MOE_HALF - 1))
                return carry2
            lax.fori_loop(0, groups, fill, 0)
            return carry
        lax.fori_loop(0, n_spans, span, 0)

    base = step * td

    def place(r, carry):
        n = base + r
        out_ref[d0_ref[r]] = n
        out_ref[d1_ref[r]] = n_tokens + n
        return carry
    lax.fori_loop(0, td, place, 0, unroll=8)


def _row_maps(pad_spans, dest0, dest1, n_rows):
    n = dest0.shape[0]
    td = 2048 if n % 2048 == 0 else ROUTER_TILE
    smem = functools.partial(pl.BlockSpec, memory_space=pltpu.SMEM)
    return pl.pallas_call(
        functools.partial(_row_maps_kernel, n_tokens=n),
        grid=(n // td,),
        in_specs=[smem(), smem((td,), lambda i: (i,)), smem((td,), lambda i: (i,))],
        out_specs=smem(),
        out_shape=jax.ShapeDtypeStruct((n_rows,), jnp.int32),
        compiler_params=_cparams("arbitrary"),
        name="moe_row_maps",
    )(pad_spans, dest0, dest1)


def _expert_kernel(blk_exp_ref, n_used_ref, map_first, map_half1, map_next, map_prev, map_half0, map_last,
                   h_ref, w13_ref, w2_ref, y_ref, xbuf0, xbuf1, ybuf0, ybuf1, w13b, w2b, gsem, ssem,
                   *, d_expert, n_tokens):
    j = pl.program_id(0)
    n_used = n_used_ref[0]
    xbuf = (xbuf0, xbuf1)
    ybuf = (ybuf0, ybuf1)

    def tile_of(ref, r):
        return ref.at[pl.ds(pl.multiple_of(r * SUBLANES, SUBLANES), SUBLANES)]

    def gather_row(idx_ref, buf, r):
        out_row = idx_ref[r]
        tok = out_row - jnp.where(out_row >= 2 * n_tokens, 2 * n_tokens, jnp.where(out_row >= n_tokens, n_tokens, 0))
        pltpu.make_async_copy(tile_of(h_ref, tok), tile_of(xbuf[buf], r), gsem.at[buf]).start(
            priority=r % 2 if isinstance(r, int) else 0)

    def scatter_row(idx_ref, buf, r):
        pltpu.make_async_copy(tile_of(ybuf[buf], r), tile_of(y_ref, idx_ref[r]), ssem.at[buf]).start(
            priority=r % 2 if isinstance(r, int) else 0)

    def wait_gather(buf):
        pltpu.make_async_copy(h_ref.at[pl.ds(0, MOE_HALF * SUBLANES)], xbuf[buf], gsem.at[buf]).wait()

    def wait_scatter(buf):
        pltpu.make_async_copy(ybuf[buf], y_ref.at[pl.ds(0, MOE_HALF * SUBLANES)], ssem.at[buf]).wait()

    def half_block(buf, gather_idx, scatter_idx):
        for r in range(MOE_HALF):
            gather_row(gather_idx, 1 - buf, r)
            scatter_row(scatter_idx, 1 - buf, r)
        x = _load_token_tiles(xbuf[buf]).astype(BF16)
        gu = jnp.dot(x, w13b[...], preferred_element_type=F32)
        g = gu[:, 0:d_expert]
        u = gu[:, d_expert:2 * d_expert]
        a = g * (1.0 / (1.0 + jnp.exp(-g))) * u
        _store_token_tiles(ybuf[buf], jnp.dot(a.astype(BF16), w2b[...], preferred_element_type=F32))

    @pl.when(j < n_used)
    def _():
        @pl.when(j == 0)
        def _():
            ybuf0[...] = jnp.zeros(ybuf0.shape, F32)
            ybuf1[...] = jnp.zeros(ybuf1.shape, F32)

            def first_rows(r, carry):
                pltpu.make_async_copy(tile_of(ybuf0, r), tile_of(y_ref, 2 * n_tokens + r), ssem.at[0]).start()
                gather_row(map_first, 0, r)
                return carry
            lax.fori_loop(0, MOE_HALF, first_rows, 0)
            wait_scatter(0)

        @pl.when((j == 0) | (blk_exp_ref[j] != blk_exp_ref[jnp.maximum(j - 1, 0)]))
        def _():
            w13b[...] = w13_ref[...].astype(BF16)
            w2b[...] = w2_ref[...].astype(BF16)

        wait_gather(0)

        @pl.when(j > 0)
        def _():
            wait_scatter(0)

        half_block(0, map_half1, map_prev)
        wait_gather(1)
        wait_scatter(1)
        half_block(1, map_next, map_half0)

        @pl.when(j == n_used - 1)
        def _():
            wait_gather(0)
            wait_scatter(0)

            def last_scatter(r, carry):
                scatter_row(map_last, 1, r)
                return carry
            lax.fori_loop(0, MOE_HALF, last_scatter, 0)
            wait_scatter(1)


def _experts(blk_exp, n_used, row_map, h3, w13_all, w2_all, layer):
    n = h3.shape[0] // SUBLANES
    d, d_expert = w13_all.shape[2], w2_all.shape[2]
    n_steps = row_map.shape[0] // MOE_TILE
    smem = functools.partial(pl.BlockSpec, memory_space=pltpu.SMEM)

    def half_spec(half_of):
        def index(j, be, nu):
            last = 2 * nu[0] - 1
            return (jnp.clip(half_of(jnp.minimum(j, nu[0] - 1), last), 0, last),)
        return smem((MOE_HALF,), index)

    def expert(j, be, nu):
        return (layer, be[jnp.minimum(j, nu[0] - 1)], 0, 0)

    token_tile = (MOE_HALF * SUBLANES, LANES)
    grid_spec = pltpu.PrefetchScalarGridSpec(
        num_scalar_prefetch=2,
        grid=(n_steps,),
        in_specs=[half_spec(lambda j, last: 0),
                  half_spec(lambda j, last: 2 * j + 1),
                  half_spec(lambda j, last: 2 * j + 2),
                  half_spec(lambda j, last: 2 * j - 1),
                  half_spec(lambda j, last: 2 * j),
                  half_spec(lambda j, last: last),
                  pl.BlockSpec(memory_space=pl.ANY),
                  pl.BlockSpec((None, None, d, 2 * d_expert), expert),
                  pl.BlockSpec((None, None, d_expert, d), expert)],
        out_specs=pl.BlockSpec(memory_space=pl.ANY),
        scratch_shapes=[pltpu.VMEM(token_tile, F32), pltpu.VMEM(token_tile, F32),
                        pltpu.VMEM(token_tile, F32), pltpu.VMEM(token_tile, F32),
                        pltpu.VMEM((d, 2 * d_expert), BF16), pltpu.VMEM((d_expert, d), BF16),
                        pltpu.SemaphoreType.DMA((2,)), pltpu.SemaphoreType.DMA((2,))],
    )
    return pl.pallas_call(
        functools.partial(_expert_kernel, d_expert=d_expert, n_tokens=n),
        grid_spec=grid_spec,
        out_shape=jax.ShapeDtypeStruct(((2 * n + MOE_HALF) * SUBLANES, LANES), F32),
        compiler_params=_cparams("arbitrary"),
        name="moe_experts",
    )(blk_exp, n_used, row_map, row_map, row_map, row_map, row_map, row_map, h3, w13_all, w2_all)


def _combine_kernel(w_ref, x_ref, mod_ref, y0_ref, y1_ref, o_ref, *, d):
    out = w_ref[:, 0:1] * _load_token_tiles(y0_ref) + w_ref[:, 1:2] * _load_token_tiles(y1_ref)
    o_ref[...] = x_ref[...] + mod_ref[:, 5 * d:6 * d] * out


def _combine(w_cols, xs, mods, y_tok, geom, n_out):
    n, d = xs.shape
    tc = TOKEN_TILE
    return pl.pallas_call(
        functools.partial(_combine_kernel, d=d),
        grid=(n_out // tc,),
        in_specs=[pl.BlockSpec((tc, 2), lambda i: (i, 0)),
                  pl.BlockSpec((tc, d), lambda i: (i, 0)),
                  pl.BlockSpec((None, 1, mods.shape[-1]), lambda i: (geom.mod_row(i), 0, 0)),
                  pl.BlockSpec((tc * SUBLANES, LANES), lambda i: (i, 0)),
                  pl.BlockSpec((tc * SUBLANES, LANES), lambda i: (n // tc + i, 0))],
        out_specs=pl.BlockSpec((tc, d), lambda i: (i, 0)),
        out_shape=jax.ShapeDtypeStruct((n_out, d), F32),
        compiler_params=_cparams("parallel"),
        name="moe_combine",
    )(w_cols, xs, mods, y_tok, y_tok)


def _hier_moe(h3, xs, mods, geom, w_r1, b_r1, w_r2, b_r2, w13_all, w2_all, layer, n_out):
    n, d = xs.shape
    wt = jnp.zeros((ROUTER_ROWS, d), F32)
    wt = wt.at[0:N_GROUPS].set(w_r1.T)
    wt = wt.at[SUBLANES:].set(jnp.transpose(w_r2, (0, 2, 1)).reshape(N_EXPERTS, d))
    bias = jnp.zeros((ROUTER_ROWS,), F32).at[0:N_GROUPS].set(b_r1).at[SUBLANES:].set(b_r2.reshape(-1))
    bias = jnp.broadcast_to(bias[:, None], (ROUTER_ROWS, LANES))
    tri = jnp.asarray(np.triu(np.ones((ROUTER_TILE, ROUTER_TILE), np.float32), 1), BF16)
    ints, flt, cnt = _router(h3, wt, bias, tri)

    counts = cnt[:, 0].astype(jnp.int32)
    padded = ((counts + MOE_TILE - 1) // MOE_TILE) * MOE_TILE
    pend = jnp.cumsum(padded)
    pstart = pend - padded
    n_blocks = (2 * n) // MOE_TILE + N_EXPERTS
    n_rows = n_blocks * MOE_TILE
    blk_row0 = jnp.arange(n_blocks, dtype=jnp.int32) * MOE_TILE
    blk_exp = jnp.minimum(jnp.sum(pend[None, :] <= blk_row0[:, None], axis=1), N_EXPERTS - 1).astype(jnp.int32)

    n_used = (pend[-1] // MOE_TILE).astype(jnp.int32)
    pad_spans = jnp.concatenate([pstart + counts, pend[-1:], padded - counts, n_rows - pend[-1:]]).astype(jnp.int32)

    dest = _dest_rows(pstart.astype(jnp.int32), ints)
    row_map = _row_maps(pad_spans, dest[0], dest[1], n_rows)
    y_tok = _experts(blk_exp, n_used[None], row_map, h3, w13_all, w2_all, layer)
    return _combine(flt[0:2].T, xs, mods, y_tok, geom, n_out)


def _rope_tables(seq, identity_rows):
    ctx_len = identity_rows
    rows = seq // GRID_W
    row = jnp.repeat(jnp.arange(rows, dtype=F32), GRID_W)
    col = jnp.tile(jnp.arange(GRID_W, dtype=F32), rows)
    inv = ROPE_THETA ** (-jnp.arange(ROT_FREQS, dtype=F32) / ROT_FREQS)
    ang_r, ang_c = row[:, None] * inv, col[:, None] * inv
    cos_h = jnp.concatenate([jnp.cos(ang_r)] * 2 + [jnp.cos(ang_c)] * 2, axis=1)
    sin_h = jnp.concatenate([-jnp.sin(ang_r), jnp.sin(ang_r), -jnp.sin(ang_c), jnp.sin(ang_c)], axis=1)
    cos_t = jnp.concatenate([jnp.tile(cos_h, (1, 2)), jnp.ones((ctx_len, LANES), F32)], axis=0)
    sin_t = jnp.concatenate([jnp.tile(sin_h, (1, 2)), jnp.zeros((ctx_len, LANES), F32)], axis=0)
    return cos_t, sin_t


def _pad_heads_cols(w, n_heads):
    d = w.shape[0]
    w3 = w.reshape(d, n_heads, HEAD_DIM)
    return jnp.concatenate([w3, jnp.zeros_like(w3)], axis=2).reshape(d, n_heads * LANES)


def kernel(x, c, ctx, c_ctx, ada_w, ada_b, norm1_g, norm2_g, a_wqkv, a_wo, a_q_norm, a_k_norm, a_lambda_q1, a_lambda_k1, a_lambda_q2, a_lambda_k2, a_subln_g, b_wqkv, b_wo, b_q_norm, b_k_norm, b_sink, f_wo, r_w1, r_b1, r_w2, r_b2, e_w13, e_w2):
    batch, seq, d = x.shape
    ctx_len = ctx.shape[1]
    depth = ada_w.shape[0]
    n = batch * (seq + ctx_len)
    assert seq % TOKEN_TILE == 0 and ctx_len % TOKEN_TILE == 0 and n % ROUTER_TILE == 0
    geom = Geom(batch, seq, ctx_len)

    xs = jnp.concatenate([x.reshape(batch * seq, d), ctx.reshape(batch * ctx_len, d)], axis=0)
    mod_rows = ((batch + 1 + SUBLANES - 1) // SUBLANES) * SUBLANES
    cvec = jnp.zeros((mod_rows, d), F32).at[0:batch].set(c).at[batch].set(c_ctx)
    mods_all = _ada_table(cvec, ada_w, ada_b).reshape(depth, mod_rows, 1, 6 * d)
    cos_t, sin_t = _rope_tables(seq, PROJ_IN_TILE)

    for i in range(depth):
        kind, j = i % N_MIXERS, i // N_MIXERS
        mods = mods_all[i]
        if kind == 0:
            lam_init = 0.8 - 0.6 * math.exp(-0.3 * i)
            nh = d // LANES
            gains = jnp.stack([jnp.tile(a_q_norm[j], 2), jnp.tile(a_k_norm[j], 2)])
            q, k, v1 = _proj_in(xs, mods, norm1_g[i], a_wqkv[j].astype(BF16), gains, cos_t, sin_t, geom,
                                n_q=nh, n_k=nh, n_plain=nh, ones_after_plain=True)
            lam_vecs = jnp.stack([a_lambda_q1[j], a_lambda_k1[j], a_lambda_q2[j], a_lambda_k2[j]])
            o_lat, o_ctx = _diff_attention(q, k, v1, lam_vecs, a_subln_g[j], geom, lam_init)
            wo = a_wo[j].astype(BF16)
        elif kind == 1:
            q_heads = d // HEAD_DIM
            nq, nk = q_heads * HEAD_DIM, SWA_KV_HEADS * HEAD_DIM
            w = b_wqkv[j]
            w_pad = jnp.concatenate([_pad_heads_cols(w[:, :nq], q_heads),
                                     _pad_heads_cols(w[:, nq:nq + nk], SWA_KV_HEADS),
                                     _pad_heads_cols(w[:, nq + nk:], SWA_KV_HEADS)], axis=1).astype(BF16)
            zeros64 = jnp.zeros((HEAD_DIM,), F32)
            gains = jnp.stack([jnp.concatenate([b_q_norm[j], zeros64]), jnp.concatenate([b_k_norm[j], zeros64])])
            q, k, v = _proj_in(xs, mods, norm1_g[i], w_pad, gains, cos_t, sin_t, geom,
                               n_q=q_heads, n_k=SWA_KV_HEADS, n_plain=SWA_KV_HEADS)
            o_lat = o_ctx = _swa_attention(q, k, v, b_sink[j], geom)
            wo3 = b_wo[j].reshape(q_heads, HEAD_DIM, d)
            wo = jnp.concatenate([wo3, jnp.zeros_like(wo3)], axis=1).reshape(q_heads * LANES, d).astype(BF16)
        else:
            gd = d // FOURIER_GROUPS
            cd, msd = _dft_mats(gd, gd ** -0.5)
            eye = jnp.eye(FOURIER_GROUPS, dtype=BF16)
            w_cs = jnp.concatenate([jnp.kron(eye, cd), jnp.kron(eye, -msd)], axis=1)
            (z,) = _proj_in(xs, mods, norm1_g[i], w_cs, jnp.zeros((2, LANES), F32), cos_t, sin_t, geom,
                            n_q=0, n_k=0, n_plain=2 * d // LANES)
            o_lat = o_ctx = _seq_dft(z, _dft_mats(ctx_len, ctx_len ** -0.5), geom, d)
            wo = f_wo[j].astype(BF16)
        xs, h2 = _proj_out(o_lat, o_ctx, wo, xs, mods, norm2_g[i], geom)
        n_out = n if i < depth - 1 else batch * seq
        xs = _hier_moe(h2, xs, mods, geom, r_w1[i], r_b1[i], r_w2[i], r_b2[i], e_w13, e_w2, i, n_out)

    return xs.reshape(batch, seq, d)
```
